```python
import jax, jax.numpy as jnp
from jax import lax
import numpy as np

D_MODEL = 2048
BATCH = 4
SEQ = 4096
DEPTH = 2

GRID_W = 64
CTX_LEN = 256
HEAD_DIM = 128
F_GROUPS = 4
F_W = F_GROUPS * HEAD_DIM
NA_HEADS = 8
NA_W = NA_HEADS * HEAD_DIM
NA_KH = 8
NA_KW = 16
C_GROUPS = 4
C_W = C_GROUPS * HEAD_DIM
CHUNK = 128
D_FF = 5632
CONV_W = 3
EPS = 1e-6
NEG_INF = -1e30

OFF_F = 0
OFF_Q = OFF_F + F_W
OFF_K = OFF_Q + NA_W
OFF_V = OFF_K + NA_W
OFF_C = OFF_V + NA_W
OFF_GATE = OFF_C + 2 * C_W
IN_W = OFF_GATE + 3 * D_MODEL

kernel_name = "hybrid_fourier_natten_gmlp_dit_block"


def rms_norm(x, w):
    x32 = x.astype(jnp.float32)
    y = x32 * lax.rsqrt(jnp.mean(x32 * x32, axis=-1, keepdims=True) + EPS)
    return (y * w.astype(jnp.float32)).astype(x.dtype)


def modulate(h, shift, scale):
    return h * (1 + scale) + shift


def split_proj(p):
    return (p[..., OFF_F:OFF_Q], p[..., OFF_Q:OFF_K], p[..., OFF_K:OFF_V],
            p[..., OFF_V:OFF_C], p[..., OFF_C:OFF_GATE], p[..., OFF_GATE:])


def fourier_mix(xf):
    b, n, _ = xf.shape
    xg = xf.astype(jnp.float32).reshape(b, n, F_GROUPS, HEAD_DIM)
    y = jnp.fft.fftn(xg, axes=(1, 3), norm="ortho").real
    return y.reshape(b, n, F_W).astype(xf.dtype)


def spatial_gating(z, norm_w, ws, bs):
    b, n, _ = z.shape
    z = jax.nn.gelu(z, approximate=False)
    u, v = jnp.split(z, 2, axis=-1)
    v32 = v.astype(jnp.float32).reshape(b, n, C_GROUPS, HEAD_DIM)
    v32 = (v32 - v32.mean(-1, keepdims=True)) * lax.rsqrt(v32.var(-1, keepdims=True) + EPS)
    vn = (v32.reshape(b, n, C_W) * norm_w.astype(jnp.float32)).astype(z.dtype)
    vc = vn.reshape(b, n // CHUNK, CHUNK, C_GROUPS, HEAD_DIM)
    sp = jnp.einsum('gij,bcjgd->bcigd', ws, vc) + bs.T[:, :, None]
    return u * sp.reshape(b, n, C_W)


def depthwise_conv(x, w, bias):
    n = x.shape[1]
    pad = CONV_W // 2
    xp = jnp.pad(x, ((0, 0), (pad, pad), (0, 0)))
    y = bias
    for tap in range(CONV_W):
        y = y + xp[:, tap:tap + n] * w[tap]
    return y


def conv_ffn(h, w_up, conv_w, conv_b, w_down):
    a, gv = jnp.split(h @ w_up, 2, axis=-1)
    a = depthwise_conv(a, conv_w, conv_b)
    return (jax.nn.silu(a) * gv) @ w_down


def context_attention(q, k, v):
    b, l, _ = q.shape
    qh = q.reshape(b, l, NA_HEADS, HEAD_DIM)
    kh = k.reshape(b, l, NA_HEADS, HEAD_DIM)
    vh = v.reshape(b, l, NA_HEADS, HEAD_DIM)
    s = jnp.einsum('bqhd,bkhd->bhqk', qh, kh, preferred_element_type=jnp.float32) * (HEAD_DIM ** -0.5)
    p = jax.nn.softmax(s, axis=-1).astype(v.dtype)
    return jnp.einsum('bhqk,bkhd->bqhd', p, vh).reshape(b, l, NA_W)


def neighborhood_attention(q, k, v, k_ctx, v_ctx, rpb):
    b, s, _ = q.shape
    rows = s // GRID_W
    kh = min(NA_KH, rows)
    scale = HEAD_DIM ** -0.5
    qg = q.reshape(b, rows, GRID_W, NA_HEADS, HEAD_DIM)
    kg = k.reshape(b, rows, GRID_W, NA_HEADS, HEAD_DIM)
    vg = v.reshape(b, rows, GRID_W, NA_HEADS, HEAD_DIM)
    kc = k_ctx.reshape(b, -1, NA_HEADS, HEAD_DIM)
    vc = v_ctx.reshape(b, -1, NA_HEADS, HEAD_DIM)
    cols = np.arange(GRID_W)
    c_start = np.clip(cols - NA_KW // 2, 0, GRID_W - NA_KW)
    col_in = (cols[None, :] >= c_start[:, None]) & (cols[None, :] < c_start[:, None] + NA_KW)
    dc_idx = np.clip(cols[None, :] - cols[:, None] + NA_KW - 1, 0, 2 * NA_KW - 2)

    def row_block(r):
        r_start = jnp.clip(r - kh // 2, 0, rows - kh)
        k_blk = lax.dynamic_slice_in_dim(kg, r_start, kh, axis=1)
        v_blk = lax.dynamic_slice_in_dim(vg, r_start, kh, axis=1)
        q_row = lax.dynamic_index_in_dim(qg, r, axis=1, keepdims=False)
        s_win = jnp.einsum('bqhd,bkwhd->bhqkw', q_row, k_blk,
                           preferred_element_type=jnp.float32) * scale
        dr_idx = r_start + jnp.arange(kh) - r + NA_KH - 1
        bias = rpb[:, dr_idx[None, :, None], dc_idx[:, None, :]]
        s_win = jnp.where(col_in[:, None, :], s_win + bias.astype(jnp.float32), NEG_INF)
        s_ctx = jnp.einsum('bqhd,blhd->bhql', q_row, kc,
                           preferred_element_type=jnp.float32) * scale
        scores = jnp.concatenate([s_win.reshape(b, NA_HEADS, GRID_W, kh * GRID_W), s_ctx], axis=-1)
        p = jax.nn.softmax(scores, axis=-1).astype(v.dtype)
        p_win = p[..., :kh * GRID_W].reshape(b, NA_HEADS, GRID_W, kh, GRID_W)
        p_ctx = p[..., kh * GRID_W:]
        return (jnp.einsum('bhqkw,bkwhd->bqhd', p_win, v_blk)
                + jnp.einsum('bhql,blhd->bqhd', p_ctx, vc))

    out = lax.map(row_block, jnp.arange(rows))
    return jnp.moveaxis(out, 0, 1).reshape(b, s, NA_W)


def gated_merge(f, att, z, gates, norm_w, ws, bs, w_f_out, w_na_out, w_c_out, w_o):
    a_f = fourier_mix(f) @ w_f_out
    a_na = att @ w_na_out
    a_c = spatial_gating(z, norm_w, ws, bs) @ w_c_out
    g_f, g_na, g_c = jnp.split(jax.nn.sigmoid(gates), 3, axis=-1)
    return (g_f * a_f + g_na * a_na + g_c * a_c) @ w_o


def setup_inputs(seed: int = 0) -> dict:
    key = jax.random.key(seed)
    ks = jax.random.split(key, 24)
    f32 = jnp.float32
    nrm = lambda k, shape, s: jax.random.normal(k, shape, f32) * s
    d = D_MODEL
    return {
        "x": nrm(ks[0], (BATCH, SEQ, d), 1.0),
        "c": nrm(ks[1], (BATCH, d), 1.0),
        "ctx": nrm(ks[2], (BATCH, CTX_LEN, d), 1.0),
        "c_ctx": nrm(ks[3], (d,), 1.0),
        "ada_w": nrm(ks[4], (DEPTH, d, 6 * d), 0.5 * d ** -0.5),
        "ada_b": nrm(ks[5], (DEPTH, 6 * d), 0.01),
        "norm1_w": 1.0 + nrm(ks[6], (DEPTH, d), 0.02),
        "norm2_w": 1.0 + nrm(ks[7], (DEPTH, d), 0.02),
        "w_in": nrm(ks[8], (DEPTH, d, IN_W), d ** -0.5),
        "na_rpb": nrm(ks[9], (DEPTH, NA_HEADS, 2 * NA_KH - 1, 2 * NA_KW - 1), 0.1),
        "gmlp_norm_w": 1.0 + nrm(ks[10], (DEPTH, C_W), 0.02),
        "gmlp_ws": nrm(ks[11], (DEPTH, C_GROUPS, CHUNK, CHUNK), CHUNK ** -0.5),
        "gmlp_bs": 1.0 + nrm(ks[12], (DEPTH, C_GROUPS, CHUNK), 0.02),
        "w_f_out": nrm(ks[13], (DEPTH, F_W, d), F_W ** -0.5),
        "w_na_out": nrm(ks[14], (DEPTH, NA_W, d), NA_W ** -0.5),
        "w_c_out": nrm(ks[15], (DEPTH, C_W, d), C_W ** -0.5),
        "w_o": nrm(ks[16], (DEPTH, d, d), d ** -0.5),
        "ffn_up": nrm(ks[17], (DEPTH, d, 2 * D_FF), d ** -0.5),
        "ffn_conv_w": nrm(ks[18], (DEPTH, CONV_W, D_FF), CONV_W ** -0.5),
        "ffn_conv_b": nrm(ks[19], (DEPTH, D_FF), 0.01),
        "ffn_down": nrm(ks[20], (DEPTH, D_FF, d), D_FF ** -0.5),
        "final_norm_w": 1.0 + nrm(ks[21], (d,), 0.02),
    }


def reference(x, c, ctx, c_ctx, ada_w, ada_b, norm1_w, norm2_w, w_in, na_rpb, gmlp_norm_w,
              gmlp_ws, gmlp_bs, w_f_out, w_na_out, w_c_out, w_o, ffn_up, ffn_conv_w, ffn_conv_b,
              ffn_down, final_norm_w):
    xc = ctx
    silu_c = jax.nn.silu(c)
    silu_cc = jax.nn.silu(c_ctx)
    for l in range(DEPTH):
        last = l == DEPTH - 1
        mod_lat = (silu_c @ ada_w[l] + ada_b[l])[:, None, :]
        mod_ctx = silu_cc @ ada_w[l] + ada_b[l]
        sh1, sc1, g1, sh2, sc2, g2 = jnp.split(mod_lat, 6, axis=-1)
        csh1, csc1, cg1, csh2, csc2, cg2 = jnp.split(mod_ctx, 6, axis=-1)
        mix_w = (gmlp_norm_w[l], gmlp_ws[l], gmlp_bs[l], w_f_out[l], w_na_out[l], w_c_out[l], w_o[l])
        ffn_w = (ffn_up[l], ffn_conv_w[l], ffn_conv_b[l], ffn_down[l])

        hc = modulate(rms_norm(xc, norm1_w[l]), csh1, csc1)
        if last:
            k_ctx, v_ctx = jnp.split(hc @ w_in[l][:, OFF_K:OFF_C], 2, axis=-1)
        else:
            f_c, q_c, k_ctx, v_ctx, z_c, gate_c = split_proj(hc @ w_in[l])
            att_c = context_attention(q_c, k_ctx, v_ctx)
            xc = xc + cg1 * gated_merge(f_c, att_c, z_c, gate_c, *mix_w)
            hc2 = modulate(rms_norm(xc, norm2_w[l]), csh2, csc2)
            xc = xc + cg2 * conv_ffn(hc2, *ffn_w)

        h = modulate(rms_norm(x, norm1_w[l]), sh1, sc1)
        f, q, k, v, z, gate = split_proj(h @ w_in[l])
        att = neighborhood_attention(q, k, v, k_ctx, v_ctx, na_rpb[l])
        x = x + g1 * gated_merge(f, att, z, gate, *mix_w)
        h2 = modulate(rms_norm(x, norm2_w[l]), sh2, sc2)
        x = x + g2 * conv_ffn(h2, *ffn_w)
    return rms_norm(x, final_norm_w)
```

```python
import functools
import math

import numpy as np
import jax
import jax.numpy as jnp
from jax import lax
from jax.experimental import pallas as pl
from jax.experimental.pallas import tpu as pltpu

F32 = jnp.float32
BF16 = jnp.bfloat16

D_MODEL = 2048
BATCH = 4
SEQ = 4096
DEPTH = 2
GRID_W = 64
ROWS = SEQ // GRID_W
CTX_LEN = 256
HEAD_DIM = 128
F_GROUPS = 4
F_W = F_GROUPS * HEAD_DIM
NA_HEADS = 8
NA_W = NA_HEADS * HEAD_DIM
NA_KH = 8
NA_KW = 16
C_GROUPS = 4
C_W = C_GROUPS * HEAD_DIM
CHUNK = 128
D_FF = 5632
CONV_W = 3
EPS = 1e-6
NEG_INF = -1e30
IN_W = F_W + 3 * NA_W + 2 * C_W + 3 * D_MODEL

N_LAT = BATCH * SEQ
N_CTX = BATCH * CTX_LEN
N_TOK = N_LAT + N_CTX
MOD_ROWS = 8

TM = 1024
TN = 512
TM_MERGE = 256
HALO = 16
QROWS = 4
KROWS = 12
VMEM_LIMIT = 56 * 1024 * 1024


def _cparams(sem):
    return pltpu.CompilerParams(dimension_semantics=sem, vmem_limit_bytes=VMEM_LIMIT)


def _mod_row(i, tm):
    return jnp.minimum((i * tm) // SEQ, BATCH)


def _norm_mod(x, nw, sh, sc):
    ms = jnp.mean(x * x, axis=-1, keepdims=True)
    return (x * lax.rsqrt(ms + EPS) * nw) * (1.0 + sc) + sh


def _mod_kernel(c_ref, w_ref, b_ref, o_ref):
    c = c_ref[...]
    s = (c * jax.nn.sigmoid(c)).astype(BF16)
    o_ref[0] = jnp.dot(s, w_ref[0].astype(BF16), preferred_element_type=F32) + b_ref[0]


def _modulation(cvec, ada_w, ada_b):
    tn = 1024
    return pl.pallas_call(
        _mod_kernel,
        grid=(DEPTH, 6 * D_MODEL // tn),
        in_specs=[
            pl.BlockSpec((MOD_ROWS, D_MODEL), lambda l, j: (0, 0)),
            pl.BlockSpec((1, D_MODEL, tn), lambda l, j: (l, 0, j)),
            pl.BlockSpec((1, 1, tn), lambda l, j: (l, 0, j)),
        ],
        out_specs=pl.BlockSpec((1, MOD_ROWS, tn), lambda l, j: (l, 0, j)),
        out_shape=jax.ShapeDtypeStruct((DEPTH, MOD_ROWS, 6 * D_MODEL), F32),
        compiler_params=_cparams(("arbitrary", "arbitrary")),
        name="modulation",
    )(cvec, ada_w, ada_b.reshape(DEPTH, 1, 6 * D_MODEL))


_J_QKV = F_W // TN
_J_Z = (F_W + 3 * NA_W) // TN
_J_GATE = (F_W + 3 * NA_W + 2 * C_W) // TN
_J_END = IN_W // TN


def _win_kernel(x_ref, sh_ref, sc_ref, nw_ref, w_ref, f_ref, qkv_ref, z_ref, g_ref, h_scr):
    i = pl.program_id(0)
    j = pl.program_id(1)

    @pl.when(j == 0)
    def _():
        r = _mod_row(i, TM)
        h = _norm_mod(x_ref[...], nw_ref[...], sh_ref[pl.ds(r, 1), :], sc_ref[pl.ds(r, 1), :])
        h_scr[...] = h.astype(BF16)

    acc = jnp.dot(h_scr[...], w_ref[...], preferred_element_type=F32)

    @pl.when(j < _J_QKV)
    def _():
        f_ref[...] = acc.astype(BF16)

    @pl.when((j >= _J_QKV) & (j < _J_Z))
    def _():
        qkv_ref[...] = acc.astype(BF16)

    @pl.when((j >= _J_Z) & (j < _J_GATE))
    def _():
        z_ref[...] = (0.5 * acc * (1.0 + lax.erf(acc * math.sqrt(0.5)))).astype(BF16)

    @pl.when(j >= _J_GATE)
    def _():
        g_ref[...] = jax.nn.sigmoid(acc).astype(BF16)


def _in_projection(x_all, mod, norm_w, w_bf16):
    def out_spec(j0, j1):
        return pl.BlockSpec((TM, TN), lambda i, j: (i, jnp.clip(j - j0, 0, j1 - j0 - 1)))

    return pl.pallas_call(
        _win_kernel,
        grid=(N_TOK // TM, _J_END),
        in_specs=[
            pl.BlockSpec((TM, D_MODEL), lambda i, j: (i, 0)),
            pl.BlockSpec((MOD_ROWS, D_MODEL), lambda i, j: (0, 0)),
            pl.BlockSpec((MOD_ROWS, D_MODEL), lambda i, j: (0, 1)),
            pl.BlockSpec((1, D_MODEL), lambda i, j: (0, 0)),
            pl.BlockSpec((D_MODEL, TN), lambda i, j: (0, j)),
        ],
        out_specs=[out_spec(0, _J_QKV), out_spec(_J_QKV, _J_Z), out_spec(_J_Z, _J_GATE),
                   out_spec(_J_GATE, _J_END)],
        out_shape=[
            jax.ShapeDtypeStruct((N_TOK, F_W), BF16),
            jax.ShapeDtypeStruct((N_TOK, 3 * NA_W), BF16),
            jax.ShapeDtypeStruct((N_TOK, 2 * C_W), BF16),
            jax.ShapeDtypeStruct((N_TOK, 3 * D_MODEL), BF16),
        ],
        scratch_shapes=[pltpu.VMEM((TM, D_MODEL), BF16)],
        compiler_params=_cparams(("arbitrary", "arbitrary")),
        name="in_projection",
    )(x_all, mod, mod, norm_w.reshape(1, D_MODEL), w_bf16)


_N_QBLK = ROWS // QROWS
_BIAS_VARIANTS = (0, 1, _N_QBLK - 1)


def _key_row0(nb):
    return min(max(QROWS * nb - NA_KH // 2, 0), ROWS - KROWS)


def _bias_kernel(rpb_ref, o_ref, pair_scr):
    h = pl.program_id(0)
    qc = lax.broadcasted_iota(jnp.int32, (GRID_W, 2 * GRID_W), 0)
    lane = lax.broadcasted_iota(jnp.int32, (GRID_W, 2 * GRID_W), 1)
    kc = lane & (GRID_W - 1)
    low = lane < GRID_W
    dc = jnp.clip(kc - qc + NA_KW - 1, 0, 2 * NA_KW - 2)
    c_start = jnp.clip(qc - NA_KW // 2, 0, GRID_W - NA_KW)
    col_in = (kc >= c_start) & (kc < c_start + NA_KW)
    n_dr = 2 * NA_KH - 1
    for dr0 in range(-1, n_dr):
        acc = jnp.zeros((GRID_W, 2 * GRID_W), F32)
        for d in range(2 * NA_KW - 1):
            lo = rpb_ref[h, dr0, d] if 0 <= dr0 < n_dr else 0.0
            hi = rpb_ref[h, dr0 + 1, d] if 0 <= dr0 + 1 < n_dr else 0.0
            acc = jnp.where(dc == d, jnp.where(low, lo, hi), acc)
        pair_scr[dr0 + 1] = acc
    for vi, nb in enumerate(_BIAS_VARIANTS):
        r0 = _key_row0(nb)
        for i in range(QROWS):
            r = QROWS * nb + i
            r_start = min(max(r - NA_KH // 2, 0), ROWS - NA_KH)
            for tp in range(KROWS // 2):
                kr = r0 + 2 * tp
                ok_lo = r_start <= kr < r_start + NA_KH
                ok_hi = r_start <= kr + 1 < r_start + NA_KH
                dr0 = kr - r + NA_KH - 1
                if not (ok_lo or ok_hi):
                    tile = jnp.full((GRID_W, 2 * GRID_W), NEG_INF, F32)
                else:
                    valid = col_in
                    if not ok_lo:
                        valid = valid & jnp.logical_not(low)
                    if not ok_hi:
                        valid = valid & low
                    tile = jnp.where(valid, pair_scr[dr0 + 1], NEG_INF)
                o_ref[vi, 0, i * GRID_W:(i + 1) * GRID_W, tp * 2 * GRID_W:(tp + 1) * 2 * GRID_W] = tile


def _bias_table(rpb):
    nv = len(_BIAS_VARIANTS)
    return pl.pallas_call(
        _bias_kernel,
        grid=(NA_HEADS,),
        in_specs=[pl.BlockSpec(memory_space=pltpu.SMEM)],
        out_specs=pl.BlockSpec((nv, 1, QROWS * GRID_W, KROWS * GRID_W), lambda h: (0, h, 0, 0)),
        out_shape=jax.ShapeDtypeStruct((nv, NA_HEADS, QROWS * GRID_W, KROWS * GRID_W), F32),
        scratch_shapes=[pltpu.VMEM((2 * NA_KH, GRID_W, 2 * GRID_W), F32)],
        compiler_params=_cparams(("arbitrary",)),
        name="attn_bias_table",
    )(rpb)


_QK_DIMS = (((1,), (1,)), ((), ()))
_SCALE = HEAD_DIM ** -0.5


def _attn_kernel(q_ref, k_ref, v_ref, kc_ref, vc_ref, bias_ref, o_ref):
    nb = pl.program_id(2)
    r0 = jnp.clip(QROWS * nb - NA_KH // 2, 0, ROWS - KROWS)
    start = pl.multiple_of(r0 * GRID_W, GRID_W)
    q = (q_ref[...].astype(F32) * _SCALE).astype(BF16)
    kw = k_ref[pl.ds(start, KROWS * GRID_W), :]
    vw = v_ref[pl.ds(start, KROWS * GRID_W), :]
    s_w = lax.dot_general(q, kw, _QK_DIMS, preferred_element_type=F32) + bias_ref[0, 0]
    s_c = lax.dot_general(q, kc_ref[...], _QK_DIMS, preferred_element_type=F32)
    m = jnp.maximum(jnp.max(s_w, axis=-1, keepdims=True), jnp.max(s_c, axis=-1, keepdims=True))
    p_w = jnp.exp(s_w - m)
    p_c = jnp.exp(s_c - m)
    denom = jnp.sum(p_w, axis=-1, keepdims=True) + jnp.sum(p_c, axis=-1, keepdims=True)
    o = (jnp.dot(p_w.astype(BF16), vw, preferred_element_type=F32)
         + jnp.dot(p_c.astype(BF16), vc_ref[...], preferred_element_type=F32))
    o_ref[...] = (o / denom).astype(BF16)


def _attention(qkv, bias):
    qb = QROWS * GRID_W
    ctx0 = N_LAT // CTX_LEN

    def variant(nb):
        return jnp.where(nb == 0, 0, jnp.where(nb == _N_QBLK - 1, 2, 1))

    return pl.pallas_call(
        _attn_kernel,
        grid=(BATCH, NA_HEADS, _N_QBLK),
        in_specs=[
            pl.BlockSpec((qb, HEAD_DIM), lambda b, h, nb: (b * _N_QBLK + nb, h)),
            pl.BlockSpec((SEQ, HEAD_DIM), lambda b, h, nb: (b, NA_HEADS + h)),
            pl.BlockSpec((SEQ, HEAD_DIM), lambda b, h, nb: (b, 2 * NA_HEADS + h)),
            pl.BlockSpec((CTX_LEN, HEAD_DIM), lambda b, h, nb: (ctx0 + b, NA_HEADS + h)),
            pl.BlockSpec((CTX_LEN, HEAD_DIM), lambda b, h, nb: (ctx0 + b, 2 * NA_HEADS + h)),
            pl.BlockSpec((1, 1, qb, KROWS * GRID_W), lambda b, h, nb: (variant(nb), h, 0, 0)),
        ],
        out_specs=pl.BlockSpec((qb, HEAD_DIM), lambda b, h, nb: (b * _N_QBLK + nb, h)),
        out_shape=jax.ShapeDtypeStruct((N_TOK, NA_W), BF16),
        compiler_params=_cparams(("arbitrary", "arbitrary", "arbitrary")),
        name="neighborhood_attention",
    )(qkv, qkv, qkv, qkv, qkv, bias)


def _ctx_attn_kernel(att_in_ref, q_ref, k_ref, v_ref, o_ref):
    del att_in_ref
    q = (q_ref[...].astype(F32) * _SCALE).astype(BF16)
    s = lax.dot_general(q, k_ref[...], _QK_DIMS, preferred_element_type=F32)
    p = jnp.exp(s - jnp.max(s, axis=-1, keepdims=True))
    o = jnp.dot(p.astype(BF16), v_ref[...], preferred_element_type=F32)
    o_ref[...] = (o / jnp.sum(p, axis=-1, keepdims=True)).astype(BF16)


def _context_attention(att, qkv):
    ctx0 = N_LAT // CTX_LEN
    return pl.pallas_call(
        _ctx_attn_kernel,
        grid=(BATCH, NA_HEADS),
        in_specs=[
            pl.BlockSpec(memory_space=pl.ANY),
            pl.BlockSpec((CTX_LEN, HEAD_DIM), lambda b, h: (ctx0 + b, h)),
            pl.BlockSpec((CTX_LEN, HEAD_DIM), lambda b, h: (ctx0 + b, NA_HEADS + h)),
            pl.BlockSpec((CTX_LEN, HEAD_DIM), lambda b, h: (ctx0 + b, 2 * NA_HEADS + h)),
        ],
        out_specs=pl.BlockSpec((CTX_LEN, HEAD_DIM), lambda b, h: (ctx0 + b, h)),
        out_shape=jax.ShapeDtypeStruct((N_TOK, NA_W), BF16),
        input_output_aliases={0: 0},
        compiler_params=_cparams(("arbitrary", "arbitrary")),
        name="context_attention",
    )(att, qkv, qkv, qkv)


def _dft_tables():
    n1 = GRID_W
    idx = np.arange(n1)
    ang1 = 2 * np.pi * np.outer(idx, idx) / n1
    w1 = np.concatenate([np.cos(ang1), -np.sin(ang1)], axis=0)
    q = idx[:, None, None]
    p = idx[None, :, None]
    b = idx[None, None, :]
    ang2 = 2 * np.pi * b * (n1 * p + q) / SEQ
    wc, ws = np.cos(ang2), np.sin(ang2)
    m2 = np.concatenate([np.concatenate([wc, ws], axis=2),
                         np.concatenate([-ws, wc], axis=2)], axis=1)
    d = np.arange(HEAD_DIM)
    ang3 = 2 * np.pi * np.outer(d, d) / HEAD_DIM
    cs = np.concatenate([np.cos(ang3), np.sin(ang3)], axis=0) / math.sqrt(SEQ * HEAD_DIM)
    n = np.arange(CTX_LEN)
    angc = 2 * np.pi * np.outer(n, n) / CTX_LEN
    t1 = np.concatenate([np.cos(angc), np.sin(angc)], axis=0)
    cs_ctx = np.concatenate([np.cos(ang3), -np.sin(ang3)], axis=0) / math.sqrt(CTX_LEN * HEAD_DIM)
    as_bf16 = lambda a: jnp.asarray(a, dtype=F32).astype(BF16)
    return as_bf16(w1), as_bf16(m2), as_bf16(cs), as_bf16(t1), as_bf16(cs_ctx)


def _fourier_kernel(x_ref, w1_ref, m2_ref, cs_ref, o_ref, re_scr, im_scr, y_scr):
    n1 = GRID_W
    w1 = w1_ref[...]
    for b in range(n1):
        a = jnp.dot(w1, x_ref[:, b * F_W:(b + 1) * F_W], preferred_element_type=F32)
        for g in range(F_GROUPS):
            sl = slice(g * HEAD_DIM, (g + 1) * HEAD_DIM)
            re_scr[g, pl.ds(b, n1, stride=n1), :] = a[:n1, sl]
            im_scr[g, pl.ds(b, n1, stride=n1), :] = a[n1:, sl]

    cs = cs_ref[...]

    def per_q(q, carry):
        row = pl.multiple_of(q * n1, n1)
        a = jnp.concatenate(
            [jnp.concatenate([re_scr[g, pl.ds(row, n1), :] for g in range(F_GROUPS)], axis=1),
             jnp.concatenate([im_scr[g, pl.ds(row, n1), :] for g in range(F_GROUPS)], axis=1)], axis=0)
        z = jnp.dot(m2_ref[q], a.astype(BF16), preferred_element_type=F32)
        lhs = jnp.concatenate(
            [jnp.concatenate([z[:n1, g * HEAD_DIM:(g + 1) * HEAD_DIM],
                              z[n1:, g * HEAD_DIM:(g + 1) * HEAD_DIM]], axis=1)
             for g in range(F_GROUPS)], axis=0).astype(BF16)
        y = jnp.dot(lhs, cs, preferred_element_type=F32)
        for g in range(F_GROUPS):
            y_scr[g, pl.ds(q, n1, stride=n1), :] = y[g * n1:(g + 1) * n1]
        return carry

    lax.fori_loop(0, n1, per_q, 0)
    for g in range(F_GROUPS):
        o_ref[:, g * HEAD_DIM:(g + 1) * HEAD_DIM] = y_scr[g].astype(BF16)


def _fourier_latent(f_all, w1, m2, cs):
    n1 = GRID_W
    fx = f_all.reshape(N_TOK // n1, n1 * F_W)
    return pl.pallas_call(
        _fourier_kernel,
        grid=(BATCH,),
        in_specs=[
            pl.BlockSpec((n1, n1 * F_W), lambda b: (b, 0)),
            pl.BlockSpec((2 * n1, n1), lambda b: (0, 0)),
            pl.BlockSpec((n1, 2 * n1, 2 * n1), lambda b: (0, 0, 0)),
            pl.BlockSpec((2 * HEAD_DIM, HEAD_DIM), lambda b: (0, 0)),
        ],
        out_specs=pl.BlockSpec((SEQ, F_W), lambda b: (b, 0)),
        out_shape=jax.ShapeDtypeStruct((N_TOK, F_W), BF16),
        scratch_shapes=[pltpu.VMEM((F_GROUPS, SEQ, HEAD_DIM), F32)] * 3,
        compiler_params=_cparams(("arbitrary",)),
        name="fourier_latent",
    )(fx, w1, m2, cs)


def _fourier_ctx_kernel(y_in_ref, x_ref, t1_ref, cs_ref, o_ref):
    del y_in_ref
    t = jnp.dot(t1_ref[...], x_ref[...], preferred_element_type=F32)
    cs = cs_ref[...]
    for g in range(F_GROUPS):
        sl = slice(g * HEAD_DIM, (g + 1) * HEAD_DIM)
        lhs = jnp.concatenate([t[:CTX_LEN, sl], t[CTX_LEN:, sl]], axis=1).astype(BF16)
        o_ref[:, sl] = jnp.dot(lhs, cs, preferred_element_type=F32).astype(BF16)


def _fourier_context(y, f_all, t1, cs_ctx):
    ctx0 = N_LAT // CTX_LEN
    return pl.pallas_call(
        _fourier_ctx_kernel,
        grid=(BATCH,),
        in_specs=[
            pl.BlockSpec(memory_space=pl.ANY),
            pl.BlockSpec((CTX_LEN, F_W), lambda b: (ctx0 + b, 0)),
            pl.BlockSpec((2 * CTX_LEN, CTX_LEN), lambda b: (0, 0)),
            pl.BlockSpec((2 * HEAD_DIM, HEAD_DIM), lambda b: (0, 0)),
        ],
        out_specs=pl.BlockSpec((CTX_LEN, F_W), lambda b: (ctx0 + b, 0)),
        out_shape=jax.ShapeDtypeStruct((N_TOK, F_W), BF16),
        input_output_aliases={0: 0},
        compiler_params=_cparams(("arbitrary",)),
        name="fourier_context",
    )(y, f_all, t1, cs_ctx)


def _gating_kernel(z_ref, nw_ref, ws_ref, bs_ref, o_ref):
    nw = nw_ref[...]
    for g in range(C_GROUPS):
        sl = slice(g * HEAD_DIM, (g + 1) * HEAD_DIM)
        v = z_ref[:, C_W + g * HEAD_DIM:C_W + (g + 1) * HEAD_DIM].astype(F32)
        mu = jnp.mean(v, axis=-1, keepdims=True)
        vc = v - mu
        var = jnp.mean(vc * vc, axis=-1, keepdims=True)
        vn = (vc * lax.rsqrt(var + EPS) * nw[:, sl]).astype(BF16)
        w = ws_ref[g]
        bias = bs_ref[g]
        for c in range(TM // CHUNK):
            rows = slice(c * CHUNK, (c + 1) * CHUNK)
            sp = jnp.dot(w, vn[rows], preferred_element_type=F32) + bias
            o_ref[rows, sl] = (z_ref[rows, sl].astype(F32) * sp).astype(BF16)


def _spatial_gating(z, norm_w, ws, bs):
    bs_exp = jnp.broadcast_to(bs[:, :, None], (C_GROUPS, CHUNK, HEAD_DIM))
    return pl.pallas_call(
        _gating_kernel,
        grid=(N_TOK // TM,),
        in_specs=[
            pl.BlockSpec((TM, 2 * C_W), lambda i: (i, 0)),
            pl.BlockSpec((1, C_W), lambda i: (0, 0)),
            pl.BlockSpec((C_GROUPS, CHUNK, CHUNK), lambda i: (0, 0, 0)),
            pl.BlockSpec((C_GROUPS, CHUNK, HEAD_DIM), lambda i: (0, 0, 0)),
        ],
        out_specs=pl.BlockSpec((TM, C_W), lambda i: (i, 0)),
        out_shape=jax.ShapeDtypeStruct((N_TOK, C_W), BF16),
        compiler_params=_cparams(("arbitrary",)),
        name="spatial_gating",
    )(z, norm_w.reshape(1, C_W), ws.astype(BF16), bs_exp)


def _merge_kernel(y_ref, att_ref, sg_ref, gate_ref, x_ref, g1_ref, wf_ref, wna_ref, wc_ref, wo_ref,
                  o_ref, m_scr):
    i = pl.program_id(0)
    r = _mod_row(i, TM_MERGE)
    y = y_ref[...]
    att = att_ref[...]
    sg = sg_ref[...]
    for c in range(D_MODEL // TN):
        sl = slice(c * TN, (c + 1) * TN)
        a_f = jnp.dot(y, wf_ref[:, sl], preferred_element_type=F32)
        a_na = jnp.dot(att, wna_ref[:, sl], preferred_element_type=F32)
        a_c = jnp.dot(sg, wc_ref[:, sl], preferred_element_type=F32)
        g_f = gate_ref[:, c * TN:(c + 1) * TN].astype(F32)
        g_na = gate_ref[:, D_MODEL + c * TN:D_MODEL + (c + 1) * TN].astype(F32)
        g_c = gate_ref[:, 2 * D_MODEL + c * TN:2 * D_MODEL + (c + 1) * TN].astype(F32)
        m_scr[:, sl] = (g_f * a_f + g_na * a_na + g_c * a_c).astype(BF16)
    m = m_scr[...]
    for c in range(D_MODEL // TN):
        sl = slice(c * TN, (c + 1) * TN)
        o = jnp.dot(m, wo_ref[:, sl], preferred_element_type=F32)
        o_ref[:, sl] = x_ref[:, sl] + g1_ref[pl.ds(r, 1), sl] * o


def _merge(y, att, sg, gate, x_all, mod, wf, wna, wc, wo, n_rows):
    tm = TM_MERGE
    resident = lambda shape: pl.BlockSpec(shape, lambda i: (0, 0), pipeline_mode=pl.Buffered(1))
    return pl.pallas_call(
        _merge_kernel,
        grid=(n_rows // tm,),
        in_specs=[
            pl.BlockSpec((tm, F_W), lambda i: (i, 0)),
            pl.BlockSpec((tm, NA_W), lambda i: (i, 0)),
            pl.BlockSpec((tm, C_W), lambda i: (i, 0)),
            pl.BlockSpec((tm, 3 * D_MODEL), lambda i: (i, 0)),
            pl.BlockSpec((tm, D_MODEL), lambda i: (i, 0)),
            pl.BlockSpec((MOD_ROWS, D_MODEL), lambda i: (0, 2)),
            resident((F_W, D_MODEL)),
            resident((NA_W, D_MODEL)),
            resident((C_W, D_MODEL)),
            resident((D_MODEL, D_MODEL)),
        ],
        out_specs=pl.BlockSpec((tm, D_MODEL), lambda i: (i, 0)),
        out_shape=jax.ShapeDtypeStruct((n_rows, D_MODEL), F32),
        scratch_shapes=[pltpu.VMEM((tm, D_MODEL), BF16)],
        compiler_params=_cparams(("arbitrary",)),
        name="gated_merge",
    )(y, att, sg, gate, x_all, mod, wf, wna, wc, wo)


_FF_TILES = D_FF // TN


def _ffn_up_kernel(x_ref, xp_ref, xn_ref, sh_ref, sc_ref, nw_ref, wa_ref, wg_ref, cw_ref, cb_ref,
                   o_ref, h_scr, a_scr, *, n_lat_tiles):
    i = pl.program_id(0)
    j = pl.program_id(1)

    @pl.when(j == 0)
    def _():
        r = _mod_row(i, TM)
        nw = nw_ref[...]
        sh = sh_ref[pl.ds(r, 1), :]
        sc = sc_ref[pl.ds(r, 1), :]
        h_scr[0:HALO] = _norm_mod(xp_ref[...], nw, sh, sc).astype(BF16)
        h_scr[HALO:HALO + TM] = _norm_mod(x_ref[...], nw, sh, sc).astype(BF16)
        h_scr[HALO + TM:] = _norm_mod(xn_ref[...], nw, sh, sc).astype(BF16)

    a_scr[...] = jnp.dot(h_scr[...], wa_ref[...], preferred_element_type=F32)
    gv = jnp.dot(h_scr[HALO:HALO + TM], wg_ref[...], preferred_element_type=F32)
    seq_mask = jnp.where(i < n_lat_tiles, SEQ - 1, CTX_LEN - 1)
    pos = (i * TM + lax.broadcasted_iota(jnp.int32, (TM, TN), 0)) & seq_mask
    prev = jnp.where(pos != 0, a_scr[HALO - 1:HALO - 1 + TM], 0.0)
    nxt = jnp.where(pos != seq_mask, a_scr[HALO + 1:HALO + 1 + TM], 0.0)
    a = cb_ref[...] + cw_ref[0:1] * prev + cw_ref[1:2] * a_scr[HALO:HALO + TM] + cw_ref[2:3] * nxt
    o_ref[...] = (a * jax.nn.sigmoid(a) * gv).astype(BF16)


def _ffn_up(x_all, mod, norm_w, w_up, conv_w, conv_b, n_rows):
    hb = TM // HALO
    last = x_all.shape[0] // HALO - 1
    return pl.pallas_call(
        functools.partial(_ffn_up_kernel, n_lat_tiles=N_LAT // TM),
        grid=(n_rows // TM, _FF_TILES),
        in_specs=[
            pl.BlockSpec((TM, D_MODEL), lambda i, j: (i, 0)),
            pl.BlockSpec((HALO, D_MODEL), lambda i, j: (jnp.maximum(i * hb - 1, 0), 0)),
            pl.BlockSpec((HALO, D_MODEL), lambda i, j: (jnp.minimum((i + 1) * hb, last), 0)),
            pl.BlockSpec((MOD_ROWS, D_MODEL), lambda i, j: (0, 3)),
            pl.BlockSpec((MOD_ROWS, D_MODEL), lambda i, j: (0, 4)),
            pl.BlockSpec((1, D_MODEL), lambda i, j: (0, 0)),
            pl.BlockSpec((D_MODEL, TN), lambda i, j: (0, j)),
            pl.BlockSpec((D_MODEL, TN), lambda i, j: (0, _FF_TILES + j)),
            pl.BlockSpec((CONV_W, TN), lambda i, j: (0, j)),
            pl.BlockSpec((1, TN), lambda i, j: (0, j)),
        ],
        out_specs=pl.BlockSpec((TM, TN), lambda i, j: (i, j)),
        out_shape=jax.ShapeDtypeStruct((n_rows, D_FF), BF16),
        scratch_shapes=[pltpu.VMEM((TM + 2 * HALO, D_MODEL), BF16),
                        pltpu.VMEM((TM + 2 * HALO, TN), F32)],
        compiler_params=_cparams(("arbitrary", "arbitrary")),
        name="ffn_up_conv",
    )(x_all, x_all, x_all, mod, mod, norm_w.reshape(1, D_MODEL), w_up, w_up, conv_w,
      conv_b.reshape(1, D_FF))


def _ffn_down_kernel(a_ref, w_ref, x_ref, g2_ref, o_ref):
    r = _mod_row(pl.program_id(0), TM)
    o = jnp.dot(a_ref[...], w_ref[...], preferred_element_type=F32)
    o_ref[...] = x_ref[...] + g2_ref[pl.ds(r, 1), :] * o


def _ffn_down(act, w_down, x_all, mod, n_rows):
    n_j = D_MODEL // TN
    return pl.pallas_call(
        _ffn_down_kernel,
        grid=(n_rows // TM, n_j),
        in_specs=[
            pl.BlockSpec((TM, D_FF), lambda i, j: (i, 0)),
            pl.BlockSpec((D_FF, TN), lambda i, j: (0, j)),
            pl.BlockSpec((TM, TN), lambda i, j: (i, j)),
            pl.BlockSpec((MOD_ROWS, TN), lambda i, j: (0, 5 * n_j + j)),
        ],
        out_specs=pl.BlockSpec((TM, TN), lambda i, j: (i, j)),
        out_shape=jax.ShapeDtypeStruct((n_rows, D_MODEL), F32),
        compiler_params=_cparams(("arbitrary", "arbitrary")),
        name="ffn_down",
    )(act, w_down, x_all, mod)


def _final_norm_kernel(x_ref, w_ref, o_ref):
    x = x_ref[...]
    ms = jnp.mean(x * x, axis=-1, keepdims=True)
    o_ref[...] = x * lax.rsqrt(ms + EPS) * w_ref[...]


def _final_norm(x, w):
    return pl.pallas_call(
        _final_norm_kernel,
        grid=(N_LAT // TM,),
        in_specs=[pl.BlockSpec((TM, D_MODEL), lambda i: (i, 0)),
                  pl.BlockSpec((1, D_MODEL), lambda i: (0, 0))],
        out_specs=pl.BlockSpec((TM, D_MODEL), lambda i: (i, 0)),
        out_shape=jax.ShapeDtypeStruct((N_LAT, D_MODEL), F32),
        compiler_params=_cparams(("arbitrary",)),
        name="final_norm",
    )(x, w.reshape(1, D_MODEL))


def kernel(x, c, ctx, c_ctx, ada_w, ada_b, norm1_w, norm2_w, w_in, na_rpb, gmlp_norm_w, gmlp_ws,
           gmlp_bs, w_f_out, w_na_out, w_c_out, w_o, ffn_up, ffn_conv_w, ffn_conv_b, ffn_down,
           final_norm_w):
    assert x.shape == (BATCH, SEQ, D_MODEL) and ctx.shape == (BATCH, CTX_LEN, D_MODEL)
    cvec = jnp.concatenate(
        [c, c_ctx[None, :], jnp.zeros((MOD_ROWS - BATCH - 1, D_MODEL), F32)], axis=0)
    mod_all = _modulation(cvec, ada_w, ada_b)
    w1, m2, cs, t1, cs_ctx = _dft_tables()
    x_all = jnp.concatenate([x.reshape(N_LAT, D_MODEL), ctx.reshape(N_CTX, D_MODEL)], axis=0)

    for l in range(DEPTH):
        last = l == DEPTH - 1
        n_rows = N_LAT if last else N_TOK
        mod = mod_all[l]
        f_all, qkv, z, gate = _in_projection(x_all, mod, norm1_w[l], w_in[l].astype(BF16))
        att = _attention(qkv, _bias_table(na_rpb[l]))
        y = _fourier_latent(f_all, w1, m2, cs)
        if not last:
            att = _context_attention(att, qkv)
            y = _fourier_context(y, f_all, t1, cs_ctx)
        sg = _spatial_gating(z, gmlp_norm_w[l], gmlp_ws[l], gmlp_bs[l])
        x_all = _merge(y, att, sg, gate, x_all, mod, w_f_out[l].astype(BF16),
                       w_na_out[l].astype(BF16), w_c_out[l].astype(BF16), w_o[l].astype(BF16), n_rows)
        act = _ffn_up(x_all, mod, norm2_w[l], ffn_up[l].astype(BF16), ffn_conv_w[l], ffn_conv_b[l],
                      n_rows)
        x_all = _ffn_down(act, ffn_down[l].astype(BF16), x_all, mod, n_rows)
    return _final_norm(x_all, final_norm_w).reshape(BATCH, SEQ, D_MODEL)
```

```python
import functools
import math

import numpy as np
import jax
import jax.numpy as jnp
from jax import lax
from jax.experimental import pallas as pl
from jax.experimental.pallas import tpu as pltpu

F32 = jnp.float32
BF16 = jnp.bfloat16

D_MODEL = 2048
BATCH = 4
SEQ = 4096
DEPTH = 2
GRID_W = 64
ROWS = SEQ // GRID_W
CTX_LEN = 256
HEAD_DIM = 128
F_GROUPS = 4
F_W = F_GROUPS * HEAD_DIM
NA_HEADS = 8
NA_W = NA_HEADS * HEAD_DIM
NA_KH = 8
NA_KW = 16
C_GROUPS = 4
C_W = C_GROUPS * HEAD_DIM
CHUNK = 128
D_FF = 5632
CONV_W = 3
EPS = 1e-6
NEG_INF = -1e30
IN_W = F_W + 3 * NA_W + 2 * C_W + 3 * D_MODEL

N_LAT = BATCH * SEQ
N_CTX = BATCH * CTX_LEN
N_TOK = N_LAT + N_CTX
MOD_ROWS = 8

TM = 1024
TN = 512
RC = 256
TM_RES = 256
HALO = 16
QROWS = 4
KROWS = 12
PITCH = 72
VMEM_LIMIT = 56 * 1024 * 1024


def _cparams(sem):
    return pltpu.CompilerParams(dimension_semantics=sem, vmem_limit_bytes=VMEM_LIMIT)


def _mod_row(i, tm):
    return jnp.minimum((i * tm) // SEQ, BATCH)


def _norm_mod(x, nw, sh, sc):
    ms = jnp.mean(x * x, axis=-1, keepdims=True)
    return (x * lax.rsqrt(ms + EPS) * nw) * (1.0 + sc) + sh


def _rms(x, nw):
    ms = jnp.mean(x * x, axis=-1, keepdims=True)
    return x * lax.rsqrt(ms + EPS) * nw


def _resident(shape):
    return pl.BlockSpec(shape, lambda *_: (0,) * len(shape), pipeline_mode=pl.Buffered(1))


def _mod_kernel(c_ref, w_ref, b_ref, o_ref):
    c = c_ref[...]
    s = (c * jax.nn.sigmoid(c)).astype(BF16)
    o_ref[0] = jnp.dot(s, w_ref[0].astype(BF16), preferred_element_type=F32) + b_ref[0]


def _modulation(cvec, ada_w, ada_b):
    tn = 1024
    return pl.pallas_call(
        _mod_kernel,
        grid=(DEPTH, 6 * D_MODEL // tn),
        in_specs=[
            pl.BlockSpec((MOD_ROWS, D_MODEL), lambda l, j: (0, 0)),
            pl.BlockSpec((1, D_MODEL, tn), lambda l, j: (l, 0, j)),
            pl.BlockSpec((1, 1, tn), lambda l, j: (l, 0, j)),
        ],
        out_specs=pl.BlockSpec((1, MOD_ROWS, tn), lambda l, j: (l, 0, j)),
        out_shape=jax.ShapeDtypeStruct((DEPTH, MOD_ROWS, 6 * D_MODEL), F32),
        compiler_params=_cparams(("arbitrary", "arbitrary")),
        name="modulation",
    )(cvec, ada_w, ada_b.reshape(DEPTH, 1, 6 * D_MODEL))


def _prenorm_kernel(xa_ref, xb_ref, sh_ref, sc_ref, nw_ref, o_ref, *, n_a_tiles):
    i = pl.program_id(0)
    r = _mod_row(i, TM)
    x = jnp.where(i < n_a_tiles, xa_ref[...], xb_ref[...])
    h = _norm_mod(x, nw_ref[...], sh_ref[pl.ds(r, 1), :], sc_ref[pl.ds(r, 1), :])
    o_ref[...] = h.astype(BF16)


def _prenorm(x2d, ctx2d, mod, norm_w):
    na = N_LAT // TM
    return pl.pallas_call(
        functools.partial(_prenorm_kernel, n_a_tiles=na),
        grid=(N_TOK // TM,),
        in_specs=[
            pl.BlockSpec((TM, D_MODEL), lambda i: (jnp.minimum(i, na - 1), 0)),
            pl.BlockSpec((TM, D_MODEL), lambda i: (jnp.maximum(i - na, 0), 0)),
            pl.BlockSpec((MOD_ROWS, D_MODEL), lambda i: (0, 0)),
            pl.BlockSpec((MOD_ROWS, D_MODEL), lambda i: (0, 1)),
            pl.BlockSpec((1, D_MODEL), lambda i: (0, 0)),
        ],
        out_specs=pl.BlockSpec((TM, D_MODEL), lambda i: (i, 0)),
        out_shape=jax.ShapeDtypeStruct((N_TOK, D_MODEL), BF16),
        compiler_params=_cparams(("arbitrary",)),
        name="prenorm",
    )(x2d, ctx2d, mod, mod, norm_w.reshape(1, D_MODEL))


def _ident(a):
    return a


def _gelu(a):
    return 0.5 * a * (1.0 + lax.erf(a * math.sqrt(0.5)))


def _proj_kernel(h_ref, w_ref, o_ref, w_scr, *, epilogue):
    @pl.when(pl.program_id(1) == 0)
    def _():
        w_scr[...] = w_ref[...].astype(BF16)

    for c in range(TM // RC):
        rows = slice(c * RC, (c + 1) * RC)
        acc = jnp.dot(h_ref[rows, :], w_scr[...], preferred_element_type=F32)
        o_ref[rows, :] = epilogue(acc).astype(BF16)


def _projection(h, w_all, layer, col0, n_cols, tn, epilogue, name):
    return pl.pallas_call(
        functools.partial(_proj_kernel, epilogue=epilogue),
        grid=(n_cols // tn, N_TOK // TM),
        in_specs=[
            pl.BlockSpec((TM, D_MODEL), lambda j, i: (i, 0)),
            pl.BlockSpec((pl.Squeezed(), pl.Element(D_MODEL), pl.Element(tn)),
                         lambda j, i: (layer, 0, pl.multiple_of(col0 + j * tn, 128))),
        ],
        out_specs=pl.BlockSpec((TM, tn), lambda j, i: (i, j)),
        out_shape=jax.ShapeDtypeStruct((N_TOK, n_cols), BF16),
        scratch_shapes=[pltpu.VMEM((D_MODEL, tn), BF16)],
        compiler_params=_cparams(("arbitrary", "arbitrary")),
        name=name,
    )(h, w_all)


_N_QBLK = ROWS // QROWS
_BIAS_VARIANTS = (0, 1, _N_QBLK - 1)


def _key_row0(nb):
    return min(max(QROWS * nb - NA_KH // 2, 0), ROWS - KROWS)


def _bias_kernel(rpb_ref, o_ref, pair_scr):
    h = pl.program_id(0)
    qc = lax.broadcasted_iota(jnp.int32, (GRID_W, 2 * GRID_W), 0)
    lane = lax.broadcasted_iota(jnp.int32, (GRID_W, 2 * GRID_W), 1)
    kc = lane & (GRID_W - 1)
    low = lane < GRID_W
    dc = jnp.clip(kc - qc + NA_KW - 1, 0, 2 * NA_KW - 2)
    c_start = jnp.clip(qc - NA_KW // 2, 0, GRID_W - NA_KW)
    col_in = (kc >= c_start) & (kc < c_start + NA_KW)
    n_dr = 2 * NA_KH - 1
    for dr0 in range(-1, n_dr):
        acc = jnp.zeros((GRID_W, 2 * GRID_W), F32)
        for d in range(2 * NA_KW - 1):
            lo = rpb_ref[h, dr0, d] if 0 <= dr0 < n_dr else 0.0
            hi = rpb_ref[h, dr0 + 1, d] if 0 <= dr0 + 1 < n_dr else 0.0
            acc = jnp.where(dc == d, jnp.where(low, lo, hi), acc)
        pair_scr[dr0 + 1] = acc
    for vi, nb in enumerate(_BIAS_VARIANTS):
        r0 = _key_row0(nb)
        for i in range(QROWS):
            r = QROWS * nb + i
            r_start = min(max(r - NA_KH // 2, 0), ROWS - NA_KH)
            for tp in range(KROWS // 2):
                kr = r0 + 2 * tp
                ok_lo = r_start <= kr < r_start + NA_KH
                ok_hi = r_start <= kr + 1 < r_start + NA_KH
                dr0 = kr - r + NA_KH - 1
                if not (ok_lo or ok_hi):
                    tile = jnp.full((GRID_W, 2 * GRID_W), NEG_INF, F32)
                else:
                    valid = col_in
                    if not ok_lo:
                        valid = valid & jnp.logical_not(low)
                    if not ok_hi:
                        valid = valid & low
                    tile = jnp.where(valid, pair_scr[dr0 + 1], NEG_INF)
                o_ref[vi, 0, i * GRID_W:(i + 1) * GRID_W, tp * 2 * GRID_W:(tp + 1) * 2 * GRID_W] = tile


def _bias_table(rpb):
    nv = len(_BIAS_VARIANTS)
    return pl.pallas_call(
        _bias_kernel,
        grid=(NA_HEADS,),
        in_specs=[pl.BlockSpec(memory_space=pltpu.SMEM)],
        out_specs=pl.BlockSpec((nv, 1, QROWS * GRID_W, KROWS * GRID_W), lambda h: (0, h, 0, 0)),
        out_shape=jax.ShapeDtypeStruct((nv, NA_HEADS, QROWS * GRID_W, KROWS * GRID_W), F32),
        scratch_shapes=[pltpu.VMEM((2 * NA_KH, GRID_W, 2 * GRID_W), F32)],
        compiler_params=_cparams(("arbitrary",)),
        name="attn_bias_table",
    )(rpb)


_QK_DIMS = (((1,), (1,)), ((), ()))
_SCALE = HEAD_DIM ** -0.5
_QB = QROWS * GRID_W


def _attn_kernel(q_ref, k_ref, v_ref, kc_ref, vc_ref, bias_ref, o_ref):
    kc = kc_ref[...]
    vc = vc_ref[...]

    def block(nb, carry):
        r0 = jnp.clip(QROWS * nb - NA_KH // 2, 0, ROWS - KROWS)
        start = pl.multiple_of(r0 * GRID_W, GRID_W)
        qrow = pl.multiple_of(nb * _QB, _QB)
        variant = jnp.where(nb == 0, 0, jnp.where(nb == _N_QBLK - 1, 2, 1))
        q = (q_ref[pl.ds(qrow, _QB), :].astype(F32) * _SCALE).astype(BF16)
        kw = k_ref[pl.ds(start, KROWS * GRID_W), :]
        vw = v_ref[pl.ds(start, KROWS * GRID_W), :]
        s_w = lax.dot_general(q, kw, _QK_DIMS, preferred_element_type=F32) + bias_ref[variant, 0]
        s_c = lax.dot_general(q, kc, _QK_DIMS, preferred_element_type=F32)
        m = jnp.maximum(jnp.max(s_w, axis=-1, keepdims=True), jnp.max(s_c, axis=-1, keepdims=True))
        p_w = jnp.exp(s_w - m)
        p_c = jnp.exp(s_c - m)
        denom = jnp.sum(p_w, axis=-1, keepdims=True) + jnp.sum(p_c, axis=-1, keepdims=True)
        o = (jnp.dot(p_w.astype(BF16), vw, preferred_element_type=F32)
             + jnp.dot(p_c.astype(BF16), vc, preferred_element_type=F32))
        o_ref[pl.ds(qrow, _QB), :] = (o / denom).astype(BF16)
        return carry

    lax.fori_loop(0, _N_QBLK, block, 0, unroll=2)


def _attention(qkv, bias):
    ctx0 = N_LAT // CTX_LEN
    nv = len(_BIAS_VARIANTS)
    return pl.pallas_call(
        _attn_kernel,
        grid=(BATCH, NA_HEADS),
        in_specs=[
            pl.BlockSpec((SEQ, HEAD_DIM), lambda b, h: (b, h)),
            pl.BlockSpec((SEQ, HEAD_DIM), lambda b, h: (b, NA_HEADS + h)),
            pl.BlockSpec((SEQ, HEAD_DIM), lambda b, h: (b, 2 * NA_HEADS + h)),
            pl.BlockSpec((CTX_LEN, HEAD_DIM), lambda b, h: (ctx0 + b, NA_HEADS + h)),
            pl.BlockSpec((CTX_LEN, HEAD_DIM), lambda b, h: (ctx0 + b, 2 * NA_HEADS + h)),
            pl.BlockSpec((nv, 1, _QB, KROWS * GRID_W), lambda b, h: (0, h, 0, 0)),
        ],
        out_specs=pl.BlockSpec((SEQ, HEAD_DIM), lambda b, h: (b, h)),
        out_shape=jax.ShapeDtypeStruct((N_TOK, NA_W), BF16),
        compiler_params=_cparams(("arbitrary", "arbitrary")),
        name="neighborhood_attention",
    )(qkv, qkv, qkv, qkv, qkv, bias)


def _ctx_attn_kernel(att_in_ref, q_ref, k_ref, v_ref, o_ref):
    del att_in_ref
    q = (q_ref[...].astype(F32) * _SCALE).astype(BF16)
    s = lax.dot_general(q, k_ref[...], _QK_DIMS, preferred_element_type=F32)
    p = jnp.exp(s - jnp.max(s, axis=-1, keepdims=True))
    o = jnp.dot(p.astype(BF16), v_ref[...], preferred_element_type=F32)
    o_ref[...] = (o / jnp.sum(p, axis=-1, keepdims=True)).astype(BF16)


def _context_attention(att, qkv):
    ctx0 = N_LAT // CTX_LEN
    return pl.pallas_call(
        _ctx_attn_kernel,
        grid=(BATCH, NA_HEADS),
        in_specs=[
            pl.BlockSpec(memory_space=pl.ANY),
            pl.BlockSpec((CTX_LEN, HEAD_DIM), lambda b, h: (ctx0 + b, h)),
            pl.BlockSpec((CTX_LEN, HEAD_DIM), lambda b, h: (ctx0 + b, NA_HEADS + h)),
            pl.BlockSpec((CTX_LEN, HEAD_DIM), lambda b, h: (ctx0 + b, 2 * NA_HEADS + h)),
        ],
        out_specs=pl.BlockSpec((CTX_LEN, HEAD_DIM), lambda b, h: (ctx0 + b, h)),
        out_shape=jax.ShapeDtypeStruct((N_TOK, NA_W), BF16),
        input_output_aliases={0: 0},
        compiler_params=_cparams(("arbitrary", "arbitrary")),
        name="context_attention",
    )(att, qkv, qkv, qkv)


def _dft_tables():
    n1 = GRID_W
    idx = np.arange(n1)
    ang1 = 2 * np.pi * np.outer(idx, idx) / n1
    w1 = np.concatenate([np.cos(ang1), -np.sin(ang1)], axis=0)
    q = idx[:, None, None]
    p = idx[None, :, None]
    b = idx[None, None, :]
    ang2 = 2 * np.pi * b * (n1 * p + q) / SEQ
    wc, ws = np.cos(ang2), np.sin(ang2)
    m2 = np.concatenate([np.concatenate([wc, ws], axis=2),
                         np.concatenate([-ws, wc], axis=2)], axis=1)
    d = np.arange(HEAD_DIM)
    ang3 = 2 * np.pi * np.outer(d, d) / HEAD_DIM
    cs = np.concatenate([np.cos(ang3), np.sin(ang3)], axis=0) / math.sqrt(SEQ * HEAD_DIM)
    n = np.arange(CTX_LEN)
    angc = 2 * np.pi * np.outer(n, n) / CTX_LEN
    t1 = np.concatenate([np.cos(angc), np.sin(angc)], axis=0)
    cs_ctx = np.concatenate([np.cos(ang3), -np.sin(ang3)], axis=0) / math.sqrt(CTX_LEN * HEAD_DIM)
    as_bf16 = lambda a: jnp.asarray(a, dtype=F32).astype(BF16)
    return as_bf16(w1), as_bf16(m2), as_bf16(cs), as_bf16(t1), as_bf16(cs_ctx)


def _fourier_kernel(x_ref, w1_ref, m2_ref, cs_ref, o_ref, re_scr, im_scr, y_scr):
    n1 = GRID_W
    w1 = w1_ref[...]
    for b in range(n1):
        a = jnp.dot(w1, x_ref[:, b * F_W:(b + 1) * F_W], preferred_element_type=F32)
        for g in range(F_GROUPS):
            sl = slice(g * HEAD_DIM, (g + 1) * HEAD_DIM)
            re_scr[g, pl.ds(b, n1, stride=PITCH), :] = a[:n1, sl]
            im_scr[g, pl.ds(b, n1, stride=PITCH), :] = a[n1:, sl]

    cs = cs_ref[...]

    def per_q(q, carry):
        row = pl.multiple_of(q * PITCH, 8)
        a = jnp.concatenate(
            [jnp.concatenate([re_scr[g, pl.ds(row, n1), :] for g in range(F_GROUPS)], axis=1),
             jnp.concatenate([im_scr[g, pl.ds(row, n1), :] for g in range(F_GROUPS)], axis=1)], axis=0)
        z = jnp.dot(m2_ref[q], a.astype(BF16), preferred_element_type=F32)
        lhs = jnp.concatenate(
            [jnp.concatenate([z[:n1, g * HEAD_DIM:(g + 1) * HEAD_DIM],
                              z[n1:, g * HEAD_DIM:(g + 1) * HEAD_DIM]], axis=1)
             for g in range(F_GROUPS)], axis=0).astype(BF16)
        y = jnp.dot(lhs, cs, preferred_element_type=F32)
        for g in range(F_GROUPS):
            y_scr[g, pl.ds(q, n1, stride=PITCH), :] = y[g * n1:(g + 1) * n1]
        return carry

    lax.fori_loop(0, n1, per_q, 0, unroll=2)
    for p in range(n1):
        for g in range(F_GROUPS):
            o_ref[p * n1:(p + 1) * n1, g * HEAD_DIM:(g + 1) * HEAD_DIM] = (
                y_scr[g, p * PITCH:p * PITCH + n1, :].astype(BF16))


def _fourier_latent(f_all, w1, m2, cs):
    n1 = GRID_W
    fx = f_all.reshape(N_TOK // n1, n1 * F_W)
    return pl.pallas_call(
        _fourier_kernel,
        grid=(BATCH,),
        in_specs=[
            pl.BlockSpec((n1, n1 * F_W), lambda b: (b, 0)),
            pl.BlockSpec((2 * n1, n1), lambda b: (0, 0)),
            pl.BlockSpec((n1, 2 * n1, 2 * n1), lambda b: (0, 0, 0)),
            pl.BlockSpec((2 * HEAD_DIM, HEAD_DIM), lambda b: (0, 0)),
        ],
        out_specs=pl.BlockSpec((SEQ, F_W), lambda b: (b, 0)),
        out_shape=jax.ShapeDtypeStruct((N_TOK, F_W), BF16),
        scratch_shapes=[pltpu.VMEM((F_GROUPS, n1 * PITCH, HEAD_DIM), F32)] * 3,
        compiler_params=_cparams(("arbitrary",)),
        name="fourier_latent",
    )(fx, w1, m2, cs)


def _fourier_ctx_kernel(y_in_ref, x_ref, t1_ref, cs_ref, o_ref):
    del y_in_ref
    t = jnp.dot(t1_ref[...], x_ref[...], preferred_element_type=F32)
    cs = cs_ref[...]
    for g in range(F_GROUPS):
        sl = slice(g * HEAD_DIM, (g + 1) * HEAD_DIM)
        lhs = jnp.concatenate([t[:CTX_LEN, sl], t[CTX_LEN:, sl]], axis=1).astype(BF16)
        o_ref[:, sl] = jnp.dot(lhs, cs, preferred_element_type=F32).astype(BF16)


def _fourier_context(y, f_all, t1, cs_ctx):
    ctx0 = N_LAT // CTX_LEN
    return pl.pallas_call(
        _fourier_ctx_kernel,
        grid=(BATCH,),
        in_specs=[
            pl.BlockSpec(memory_space=pl.ANY),
            pl.BlockSpec((CTX_LEN, F_W), lambda b: (ctx0 + b, 0)),
            pl.BlockSpec((2 * CTX_LEN, CTX_LEN), lambda b: (0, 0)),
            pl.BlockSpec((2 * HEAD_DIM, HEAD_DIM), lambda b: (0, 0)),
        ],
        out_specs=pl.BlockSpec((CTX_LEN, F_W), lambda b: (ctx0 + b, 0)),
        out_shape=jax.ShapeDtypeStruct((N_TOK, F_W), BF16),
        input_output_aliases={0: 0},
        compiler_params=_cparams(("arbitrary",)),
        name="fourier_context",
    )(y, f_all, t1, cs_ctx)


def _gating_kernel(z_ref, nw_ref, ws_ref, bs_ref, o_ref):
    nw = nw_ref[...]
    for g in range(C_GROUPS):
        sl = slice(g * HEAD_DIM, (g + 1) * HEAD_DIM)
        v = z_ref[:, C_W + g * HEAD_DIM:C_W + (g + 1) * HEAD_DIM].astype(F32)
        mu = jnp.mean(v, axis=-1, keepdims=True)
        vc = v - mu
        var = jnp.mean(vc * vc, axis=-1, keepdims=True)
        vn = (vc * lax.rsqrt(var + EPS) * nw[:, sl]).astype(BF16)
        w = ws_ref[g]
        bias = bs_ref[g]
        for c in range(TM // CHUNK):
            rows = slice(c * CHUNK, (c + 1) * CHUNK)
            sp = jnp.dot(w, vn[rows], preferred_element_type=F32) + bias
            o_ref[rows, sl] = (z_ref[rows, sl].astype(F32) * sp).astype(BF16)


def _spatial_gating(z, norm_w, ws, bs):
    bs_exp = jnp.broadcast_to(bs[:, :, None], (C_GROUPS, CHUNK, HEAD_DIM))
    return pl.pallas_call(
        _gating_kernel,
        grid=(N_TOK // TM,),
        in_specs=[
            pl.BlockSpec((TM, 2 * C_W), lambda i: (i, 0)),
            pl.BlockSpec((1, C_W), lambda i: (0, 0)),
            pl.BlockSpec((C_GROUPS, CHUNK, CHUNK), lambda i: (0, 0, 0)),
            pl.BlockSpec((C_GROUPS, CHUNK, HEAD_DIM), lambda i: (0, 0, 0)),
        ],
        out_specs=pl.BlockSpec((TM, C_W), lambda i: (i, 0)),
        out_shape=jax.ShapeDtypeStruct((N_TOK, C_W), BF16),
        compiler_params=_cparams(("arbitrary",)),
        name="spatial_gating",
    )(z, norm_w.reshape(1, C_W), ws.astype(BF16), bs_exp)


def _merge_kernel(y_ref, att_ref, sg_ref, gate_ref, xa_ref, xb_ref, g1_ref, sh_ref, sc_ref, nw_ref,
                  wf_ref, wna_ref, wc_ref, wo_ref, o_ref, h_ref, m_scr, *, n_a_tiles):
    i = pl.program_id(0)
    r = _mod_row(i, TM_RES)
    y = y_ref[...]
    att = att_ref[...]
    sg = sg_ref[...]
    for c in range(D_MODEL // TN):
        sl = slice(c * TN, (c + 1) * TN)
        a_f = jnp.dot(y, wf_ref[:, sl], preferred_element_type=F32)
        a_na = jnp.dot(att, wna_ref[:, sl], preferred_element_type=F32)
        a_c = jnp.dot(sg, wc_ref[:, sl], preferred_element_type=F32)
        g_f = gate_ref[:, c * TN:(c + 1) * TN].astype(F32)
        g_na = gate_ref[:, D_MODEL + c * TN:D_MODEL + (c + 1) * TN].astype(F32)
        g_c = gate_ref[:, 2 * D_MODEL + c * TN:2 * D_MODEL + (c + 1) * TN].astype(F32)
        m_scr[:, sl] = (g_f * a_f + g_na * a_na + g_c * a_c).astype(BF16)
    m = m_scr[...]
    from_a = i < n_a_tiles
    for c in range(D_MODEL // TN):
        sl = slice(c * TN, (c + 1) * TN)
        o = jnp.dot(m, wo_ref[:, sl], preferred_element_type=F32)
        x = jnp.where(from_a, xa_ref[:, sl], xb_ref[:, sl])
        o_ref[:, sl] = x + g1_ref[pl.ds(r, 1), sl] * o
    h = _norm_mod(o_ref[...], nw_ref[...], sh_ref[pl.ds(r, 1), :], sc_ref[pl.ds(r, 1), :])
    h_ref[...] = h.astype(BF16)


def _merge(y, att, sg, gate, xa, xb, mod, norm2_w, wf, wna, wc, wo, n_rows):
    tm = TM_RES
    na = xa.shape[0] // tm
    nbt = xb.shape[0] // tm
    return pl.pallas_call(
        functools.partial(_merge_kernel, n_a_tiles=na),
        grid=(n_rows // tm,),
        in_specs=[
            pl.BlockSpec((tm, F_W), lambda i: (i, 0)),
            pl.BlockSpec((tm, NA_W), lambda i: (i, 0)),
            pl.BlockSpec((tm, C_W), lambda i: (i, 0)),
            pl.BlockSpec((tm, 3 * D_MODEL), lambda i: (i, 0)),
            pl.BlockSpec((tm, D_MODEL), lambda i: (jnp.minimum(i, na - 1), 0)),
            pl.BlockSpec((tm, D_MODEL), lambda i: (jnp.clip(i - na, 0, nbt - 1), 0)),
            pl.BlockSpec((MOD_ROWS, D_MODEL), lambda i: (0, 2)),
            pl.BlockSpec((MOD_ROWS, D_MODEL), lambda i: (0, 3)),
            pl.BlockSpec((MOD_ROWS, D_MODEL), lambda i: (0, 4)),
            pl.BlockSpec((1, D_MODEL), lambda i: (0, 0)),
            _resident((F_W, D_MODEL)),
            _resident((NA_W, D_MODEL)),
            _resident((C_W, D_MODEL)),
            _resident((D_MODEL, D_MODEL)),
        ],
        out_specs=[pl.BlockSpec((tm, D_MODEL), lambda i: (i, 0)),
                   pl.BlockSpec((tm, D_MODEL), lambda i: (i, 0))],
        out_shape=[jax.ShapeDtypeStruct((n_rows, D_MODEL), F32),
                   jax.ShapeDtypeStruct((n_rows, D_MODEL), BF16)],
        scratch_shapes=[pltpu.VMEM((tm, D_MODEL), BF16)],
        compiler_params=_cparams(("arbitrary",)),
        name="gated_merge",
    )(y, att, sg, gate, xa, xb, mod, mod, mod, norm2_w.reshape(1, D_MODEL), wf, wna, wc, wo)


_FF_TILES = D_FF // TN


def _ffn_up_kernel(h_ref, hp_ref, hn_ref, wa_ref, wg_ref, cw_ref, cb_ref, o_ref, wa_scr, wg_scr, a_scr,
                   *, n_lat_tiles):
    i = pl.program_id(1)

    @pl.when(i == 0)
    def _():
        wa_scr[...] = wa_ref[...].astype(BF16)
        wg_scr[...] = wg_ref[...].astype(BF16)

    n_chunks = TM // RC
    for c in range(n_chunks):
        parts = [h_ref[c * RC:(c + 1) * RC, :]]
        lo, hi = HALO + c * RC, HALO + (c + 1) * RC
        if c == 0:
            parts.insert(0, hp_ref[...])
            lo -= HALO
        if c == n_chunks - 1:
            parts.append(hn_ref[...])
            hi += HALO
        lhs = parts[0] if len(parts) == 1 else jnp.concatenate(parts, axis=0)
        a_scr[lo:hi, :] = jnp.dot(lhs, wa_scr[...], preferred_element_type=F32)

    seq_mask = jnp.where(i < n_lat_tiles, SEQ - 1, CTX_LEN - 1)
    w0, w1, w2, cb = cw_ref[0:1], cw_ref[1:2], cw_ref[2:3], cb_ref[...]
    for c in range(n_chunks):
        r0 = HALO + c * RC
        gv = jnp.dot(h_ref[c * RC:(c + 1) * RC, :], wg_scr[...], preferred_element_type=F32)
        pos = (i * TM + c * RC + lax.broadcasted_iota(jnp.int32, (RC, TN), 0)) & seq_mask
        prev = jnp.where(pos != 0, a_scr[r0 - 1:r0 - 1 + RC, :], 0.0)
        nxt = jnp.where(pos != seq_mask, a_scr[r0 + 1:r0 + 1 + RC, :], 0.0)
        a = cb + w0 * prev + w1 * a_scr[r0:r0 + RC, :] + w2 * nxt
        o_ref[c * RC:(c + 1) * RC, :] = (a * jax.nn.sigmoid(a) * gv).astype(BF16)


def _ffn_up(h2, w_up_all, layer, conv_w, conv_b, n_rows):
    hb = TM // HALO
    last = h2.shape[0] // HALO - 1

    def wspec(off):
        return pl.BlockSpec((pl.Squeezed(), D_MODEL, TN), lambda j, i: (layer, 0, off + j))

    return pl.pallas_call(
        functools.partial(_ffn_up_kernel, n_lat_tiles=N_LAT // TM),
        grid=(_FF_TILES, n_rows // TM),
        in_specs=[
            pl.BlockSpec((TM, D_MODEL), lambda j, i: (i, 0)),
            pl.BlockSpec((HALO, D_MODEL), lambda j, i: (jnp.maximum(i * hb - 1, 0), 0)),
            pl.BlockSpec((HALO, D_MODEL), lambda j, i: (jnp.minimum((i + 1) * hb, last), 0)),
            wspec(0),
            wspec(_FF_TILES),
            pl.BlockSpec((CONV_W, TN), lambda j, i: (0, j)),
            pl.BlockSpec((1, TN), lambda j, i: (0, j)),
        ],
        out_specs=pl.BlockSpec((TM, TN), lambda j, i: (i, j)),
        out_shape=jax.ShapeDtypeStruct((n_rows, D_FF), BF16),
        scratch_shapes=[pltpu.VMEM((D_MODEL, TN), BF16), pltpu.VMEM((D_MODEL, TN), BF16),
                        pltpu.VMEM((TM + 2 * HALO, TN), F32)],
        compiler_params=_cparams(("arbitrary", "arbitrary")),
        name="ffn_up_conv",
    )(h2, h2, h2, w_up_all, w_up_all, conv_w, conv_b.reshape(1, D_FF))


def _ffn_down_kernel(a_ref, w_ref, x_ref, g2_ref, sh_ref, sc_ref, nw_ref, *refs, final):
    r = _mod_row(pl.program_id(0), TM_RES)
    xs_ref = refs[1] if final else refs[0]
    act = a_ref[...]
    for c in range(D_MODEL // TN):
        sl = slice(c * TN, (c + 1) * TN)
        o = jnp.dot(act, w_ref[:, sl], preferred_element_type=F32)
        xs_ref[:, sl] = x_ref[:, sl] + g2_ref[pl.ds(r, 1), sl] * o
    if final:
        refs[0][...] = _rms(xs_ref[...], nw_ref[...])
    else:
        h = _norm_mod(xs_ref[...], nw_ref[...], sh_ref[pl.ds(r, 1), :], sc_ref[pl.ds(r, 1), :])
        refs[1][...] = h.astype(BF16)


def _ffn_down(act, w_down, x_all, mod, mod_next, norm_w, n_rows, final):
    tm = TM_RES
    row = lambda i: (i, 0)
    if final:
        out_specs = pl.BlockSpec((tm, D_MODEL), row)
        out_shape = jax.ShapeDtypeStruct((n_rows, D_MODEL), F32)
        scratch = [pltpu.VMEM((tm, D_MODEL), F32)]
    else:
        out_specs = [pl.BlockSpec((tm, D_MODEL), row), pl.BlockSpec((tm, D_MODEL), row)]
        out_shape = [jax.ShapeDtypeStruct((n_rows, D_MODEL), F32),
                     jax.ShapeDtypeStruct((n_rows, D_MODEL), BF16)]
        scratch = []
    return pl.pallas_call(
        functools.partial(_ffn_down_kernel, final=final),
        grid=(n_rows // tm,),
        in_specs=[
            pl.BlockSpec((tm, D_FF), row),
            _resident((D_FF, D_MODEL)),
            pl.BlockSpec((tm, D_MODEL), row),
            pl.BlockSpec((MOD_ROWS, D_MODEL), lambda i: (0, 5)),
            pl.BlockSpec((MOD_ROWS, D_MODEL), lambda i: (0, 0)),
            pl.BlockSpec((MOD_ROWS, D_MODEL), lambda i: (0, 1)),
            pl.BlockSpec((1, D_MODEL), lambda i: (0, 0)),
        ],
        out_specs=out_specs,
        out_shape=out_shape,
        scratch_shapes=scratch,
        compiler_params=_cparams(("arbitrary",)),
        name="ffn_down",
    )(act, w_down, x_all, mod, mod_next, mod_next, norm_w.reshape(1, D_MODEL))


def kernel(x, c, ctx, c_ctx, ada_w, ada_b, norm1_w, norm2_w, w_in, na_rpb, gmlp_norm_w, gmlp_ws,
           gmlp_bs, w_f_out, w_na_out, w_c_out, w_o, ffn_up, ffn_conv_w, ffn_conv_b, ffn_down,
           final_norm_w):
    assert x.shape == (BATCH, SEQ, D_MODEL) and ctx.shape == (BATCH, CTX_LEN, D_MODEL)
    cvec = jnp.concatenate(
        [c, c_ctx[None, :], jnp.zeros((MOD_ROWS - BATCH - 1, D_MODEL), F32)], axis=0)
    mod_all = _modulation(cvec, ada_w, ada_b)
    w1, m2, cs, t1, cs_ctx = _dft_tables()
    xa = x.reshape(N_LAT, D_MODEL)
    xb = ctx.reshape(N_CTX, D_MODEL)
    h1 = _prenorm(xa, xb, mod_all[0], norm1_w[0])

    for l in range(DEPTH):
        last = l == DEPTH - 1
        n_rows = N_LAT if last else N_TOK
        mod = mod_all[l]
        f_all = _projection(h1, w_in, l, 0, F_W, F_W, _ident, "in_proj_f")
        qkv = _projection(h1, w_in, l, F_W, 3 * NA_W, 1024, _ident, "in_proj_qkv")
        z = _projection(h1, w_in, l, F_W + 3 * NA_W, 2 * C_W, 1024, _gelu, "in_proj_z")
        gate = _projection(h1, w_in, l, F_W + 3 * NA_W + 2 * C_W, 3 * D_MODEL, 1024, jax.nn.sigmoid,
                           "in_proj_gate")
        att = _attention(qkv, _bias_table(na_rpb[l]))
        y = _fourier_latent(f_all, w1, m2, cs)
        if not last:
            att = _context_attention(att, qkv)
            y = _fourier_context(y, f_all, t1, cs_ctx)
        sg = _spatial_gating(z, gmlp_norm_w[l], gmlp_ws[l], gmlp_bs[l])
        x_mid, h2 = _merge(y, att, sg, gate, xa, xb, mod, norm2_w[l], w_f_out[l].astype(BF16),
                           w_na_out[l].astype(BF16), w_c_out[l].astype(BF16), w_o[l].astype(BF16),
                           n_rows)
        act = _ffn_up(h2, ffn_up, l, ffn_conv_w[l], ffn_conv_b[l], n_rows)
        if last:
            out = _ffn_down(act, ffn_down[l].astype(BF16), x_mid, mod, mod, final_norm_w, n_rows, True)
        else:
            xa, h1 = _ffn_down(act, ffn_down[l].astype(BF16), x_mid, mod, mod_all[l + 1],
                               norm1_w[l + 1], n_rows, False)
            xb = xa
    return out.reshape(BATCH, SEQ, D_MODEL)
```

```python
import functools
import math

import numpy as np
import jax
import jax.numpy as jnp
from jax import lax
from jax.experimental import pallas as pl
from jax.experimental.pallas import tpu as pltpu

F32 = jnp.float32
BF16 = jnp.bfloat16

D_MODEL = 2048
BATCH = 4
SEQ = 4096
DEPTH = 2
GRID_W = 64
ROWS = SEQ // GRID_W
CTX_LEN = 256
HEAD_DIM = 128
F_GROUPS = 4
F_W = F_GROUPS * HEAD_DIM
NA_HEADS = 8
NA_W = NA_HEADS * HEAD_DIM
NA_KH = 8
NA_KW = 16
C_GROUPS = 4
C_W = C_GROUPS * HEAD_DIM
CHUNK = 128
D_FF = 5632
CONV_W = 3
EPS = 1e-6
NEG_INF = -1e30
IN_W = F_W + 3 * NA_W + 2 * C_W + 3 * D_MODEL

N_LAT = BATCH * SEQ
N_CTX = BATCH * CTX_LEN
N_TOK = N_LAT + N_CTX
MOD_ROWS = 8

TM = 1024
TN = 512
N_CHUNKS = 8
TM_RES = 256
HALO = 16
QROWS = 4
KROWS = 12
PITCH = 72
VMEM_LIMIT = 56 * 1024 * 1024


def _cparams(sem):
    return pltpu.CompilerParams(dimension_semantics=sem, vmem_limit_bytes=VMEM_LIMIT)


def _mod_row(i, tm):
    return jnp.minimum((i * tm) // SEQ, BATCH)


def _norm_mod(x, nw, sh, sc):
    ms = jnp.mean(x * x, axis=-1, keepdims=True)
    return (x * lax.rsqrt(ms + EPS) * nw) * (1.0 + sc) + sh


def _rms(x, nw):
    ms = jnp.mean(x * x, axis=-1, keepdims=True)
    return x * lax.rsqrt(ms + EPS) * nw


def _resident(shape):
    return pl.BlockSpec(shape, lambda *_: (0,) * len(shape), pipeline_mode=pl.Buffered(1))


def _mod_kernel(c_ref, w_ref, b_ref, o_ref):
    c = c_ref[...]
    s = (c * jax.nn.sigmoid(c)).astype(BF16)
    o_ref[0] = jnp.dot(s, w_ref[0].astype(BF16), preferred_element_type=F32) + b_ref[0]


def _modulation(cvec, ada_w, ada_b):
    tn = 1024
    return pl.pallas_call(
        _mod_kernel,
        grid=(DEPTH, 6 * D_MODEL // tn),
        in_specs=[
            pl.BlockSpec((MOD_ROWS, D_MODEL), lambda l, j: (0, 0)),
            pl.BlockSpec((1, D_MODEL, tn), lambda l, j: (l, 0, j)),
            pl.BlockSpec((1, 1, tn), lambda l, j: (l, 0, j)),
        ],
        out_specs=pl.BlockSpec((1, MOD_ROWS, tn), lambda l, j: (l, 0, j)),
        out_shape=jax.ShapeDtypeStruct((DEPTH, MOD_ROWS, 6 * D_MODEL), F32),
        compiler_params=_cparams(("arbitrary", "arbitrary")),
        name="modulation",
    )(cvec, ada_w, ada_b.reshape(DEPTH, 1, 6 * D_MODEL))


def _prenorm_kernel(xa_ref, xb_ref, sh_ref, sc_ref, nw_ref, o_ref, *, n_a_tiles):
    i = pl.program_id(0)
    r = _mod_row(i, TM)
    x = jnp.where(i < n_a_tiles, xa_ref[...], xb_ref[...])
    h = _norm_mod(x, nw_ref[...], sh_ref[pl.ds(r, 1), :], sc_ref[pl.ds(r, 1), :])
    o_ref[...] = h.astype(BF16)


def _prenorm(x2d, ctx2d, mod, norm_w):
    na = N_LAT // TM
    return pl.pallas_call(
        functools.partial(_prenorm_kernel, n_a_tiles=na),
        grid=(N_TOK // TM,),
        in_specs=[
            pl.BlockSpec((TM, D_MODEL), lambda i: (jnp.minimum(i, na - 1), 0)),
            pl.BlockSpec((TM, D_MODEL), lambda i: (jnp.maximum(i - na, 0), 0)),
            pl.BlockSpec((MOD_ROWS, D_MODEL), lambda i: (0, 0)),
            pl.BlockSpec((MOD_ROWS, D_MODEL), lambda i: (0, 1)),
            pl.BlockSpec((1, D_MODEL), lambda i: (0, 0)),
        ],
        out_specs=pl.BlockSpec((TM, D_MODEL), lambda i: (i, 0)),
        out_shape=jax.ShapeDtypeStruct((N_TOK, D_MODEL), BF16),
        compiler_params=_cparams(("arbitrary",)),
        name="prenorm",
    )(x2d, ctx2d, mod, mod, norm_w.reshape(1, D_MODEL))


def _ident(a):
    return a


def _gelu(a):
    return 0.5 * a * (1.0 + lax.erf(a * math.sqrt(0.5)))


def _proj_tile(n_rows):
    tp = n_rows // 8
    assert tp * 8 == n_rows and tp % (N_CHUNKS * HALO) == 0
    return tp


def _proj_kernel(h_ref, w_ref, o_ref, w_scr, *, epilogue):
    @pl.when(pl.program_id(1) == 0)
    def _():
        w_scr[...] = w_ref[...].astype(BF16)

    rc = h_ref.shape[0] // N_CHUNKS
    for c in range(N_CHUNKS):
        rows = slice(c * rc, (c + 1) * rc)
        acc = jnp.dot(h_ref[rows, :], w_scr[...], preferred_element_type=F32)
        o_ref[rows, :] = epilogue(acc).astype(BF16)


def _projection(h, w_all, layer, col0, n_cols, tn, epilogue, name, n_rows):
    tp = _proj_tile(n_rows)
    return pl.pallas_call(
        functools.partial(_proj_kernel, epilogue=epilogue),
        grid=(n_cols // tn, n_rows // tp),
        in_specs=[
            pl.BlockSpec((tp, D_MODEL), lambda j, i: (i, 0)),
            pl.BlockSpec((pl.Squeezed(), pl.Element(D_MODEL), pl.Element(tn)),
                         lambda j, i: (layer, 0, pl.multiple_of(col0 + j * tn, 128))),
        ],
        out_specs=pl.BlockSpec((tp, tn), lambda j, i: (i, j)),
        out_shape=jax.ShapeDtypeStruct((n_rows, n_cols), BF16),
        scratch_shapes=[pltpu.VMEM((D_MODEL, tn), BF16)],
        compiler_params=_cparams(("arbitrary", "arbitrary")),
        name=name,
    )(h, w_all)


_N_QBLK = ROWS // QROWS
_BIAS_VARIANTS = (0, 1, _N_QBLK - 1)


def _key_row0(nb):
    return min(max(QROWS * nb - NA_KH // 2, 0), ROWS - KROWS)


def _bias_kernel(rpb_ref, o_ref, pair_scr):
    h = pl.program_id(0)
    qc = lax.broadcasted_iota(jnp.int32, (GRID_W, 2 * GRID_W), 0)
    lane = lax.broadcasted_iota(jnp.int32, (GRID_W, 2 * GRID_W), 1)
    kc = lane & (GRID_W - 1)
    low = lane < GRID_W
    dc = jnp.clip(kc - qc + NA_KW - 1, 0, 2 * NA_KW - 2)
    c_start = jnp.clip(qc - NA_KW // 2, 0, GRID_W - NA_KW)
    col_in = (kc >= c_start) & (kc < c_start + NA_KW)
    n_dr = 2 * NA_KH - 1
    for dr0 in range(-1, n_dr):
        acc = jnp.zeros((GRID_W, 2 * GRID_W), F32)
        for d in range(2 * NA_KW - 1):
            lo = rpb_ref[h, dr0, d] if 0 <= dr0 < n_dr else 0.0
            hi = rpb_ref[h, dr0 + 1, d] if 0 <= dr0 + 1 < n_dr else 0.0
            acc = jnp.where(dc == d, jnp.where(low, lo, hi), acc)
        pair_scr[dr0 + 1] = acc
    for vi, nb in enumerate(_BIAS_VARIANTS):
        r0 = _key_row0(nb)
        for i in range(QROWS):
            r = QROWS * nb + i
            r_start = min(max(r - NA_KH // 2, 0), ROWS - NA_KH)
            for tp in range(KROWS // 2):
                kr = r0 + 2 * tp
                ok_lo = r_start <= kr < r_start + NA_KH
                ok_hi = r_start <= kr + 1 < r_start + NA_KH
                dr0 = kr - r + NA_KH - 1
                if not (ok_lo or ok_hi):
                    tile = jnp.full((GRID_W, 2 * GRID_W), NEG_INF, F32)
                else:
                    valid = col_in
                    if not ok_lo:
                        valid = valid & jnp.logical_not(low)
                    if not ok_hi:
                        valid = valid & low
                    tile = jnp.where(valid, pair_scr[dr0 + 1], NEG_INF)
                o_ref[vi, 0, i * GRID_W:(i + 1) * GRID_W, tp * 2 * GRID_W:(tp + 1) * 2 * GRID_W] = tile


def _bias_table(rpb):
    nv = len(_BIAS_VARIANTS)
    return pl.pallas_call(
        _bias_kernel,
        grid=(NA_HEADS,),
        in_specs=[pl.BlockSpec(memory_space=pltpu.SMEM)],
        out_specs=pl.BlockSpec((nv, 1, QROWS * GRID_W, KROWS * GRID_W), lambda h: (0, h, 0, 0)),
        out_shape=jax.ShapeDtypeStruct((nv, NA_HEADS, QROWS * GRID_W, KROWS * GRID_W), F32),
        scratch_shapes=[pltpu.VMEM((2 * NA_KH, GRID_W, 2 * GRID_W), F32)],
        compiler_params=_cparams(("arbitrary",)),
        name="attn_bias_table",
    )(rpb)


_QK_DIMS = (((1,), (1,)), ((), ()))
_SCALE = HEAD_DIM ** -0.5
_QB = QROWS * GRID_W


def _attn_kernel(q_ref, k_ref, v_ref, kc_ref, vc_ref, bias_ref, o_ref):
    kc = kc_ref[...]
    vc = vc_ref[...]

    def block(nb, carry):
        r0 = jnp.clip(QROWS * nb - NA_KH // 2, 0, ROWS - KROWS)
        start = pl.multiple_of(r0 * GRID_W, GRID_W)
        qrow = pl.multiple_of(nb * _QB, _QB)
        variant = jnp.where(nb == 0, 0, jnp.where(nb == _N_QBLK - 1, 2, 1))
        q = (q_ref[pl.ds(qrow, _QB), :].astype(F32) * _SCALE).astype(BF16)
        kw = k_ref[pl.ds(start, KROWS * GRID_W), :]
        vw = v_ref[pl.ds(start, KROWS * GRID_W), :]
        s_w = lax.dot_general(q, kw, _QK_DIMS, preferred_element_type=F32) + bias_ref[variant, 0]
        s_c = lax.dot_general(q, kc, _QK_DIMS, preferred_element_type=F32)
        m = jnp.maximum(jnp.max(s_w, axis=-1, keepdims=True), jnp.max(s_c, axis=-1, keepdims=True))
        p_w = jnp.exp(s_w - m)
        p_c = jnp.exp(s_c - m)
        denom = jnp.sum(p_w, axis=-1, keepdims=True) + jnp.sum(p_c, axis=-1, keepdims=True)
        o = (jnp.dot(p_w.astype(BF16), vw, preferred_element_type=F32)
             + jnp.dot(p_c.astype(BF16), vc, preferred_element_type=F32))
        o_ref[pl.ds(qrow, _QB), :] = (o / denom).astype(BF16)
        return carry

    lax.fori_loop(0, _N_QBLK, block, 0, unroll=2)


def _attention(qkv, bias):
    ctx0 = N_LAT // CTX_LEN
    nv = len(_BIAS_VARIANTS)
    return pl.pallas_call(
        _attn_kernel,
        grid=(BATCH, NA_HEADS),
        in_specs=[
            pl.BlockSpec((SEQ, HEAD_DIM), lambda b, h: (b, h)),
            pl.BlockSpec((SEQ, HEAD_DIM), lambda b, h: (b, NA_HEADS + h)),
            pl.BlockSpec((SEQ, HEAD_DIM), lambda b, h: (b, 2 * NA_HEADS + h)),
            pl.BlockSpec((CTX_LEN, HEAD_DIM), lambda b, h: (ctx0 + b, NA_HEADS + h)),
            pl.BlockSpec((CTX_LEN, HEAD_DIM), lambda b, h: (ctx0 + b, 2 * NA_HEADS + h)),
            pl.BlockSpec((nv, 1, _QB, KROWS * GRID_W), lambda b, h: (0, h, 0, 0)),
        ],
        out_specs=pl.BlockSpec((SEQ, HEAD_DIM), lambda b, h: (b, h)),
        out_shape=jax.ShapeDtypeStruct((N_TOK, NA_W), BF16),
        compiler_params=_cparams(("arbitrary", "arbitrary")),
        name="neighborhood_attention",
    )(qkv, qkv, qkv, qkv, qkv, bias)


def _ctx_attn_kernel(att_in_ref, q_ref, k_ref, v_ref, o_ref):
    del att_in_ref
    q = (q_ref[...].astype(F32) * _SCALE).astype(BF16)
    s = lax.dot_general(q, k_ref[...], _QK_DIMS, preferred_element_type=F32)
    p = jnp.exp(s - jnp.max(s, axis=-1, keepdims=True))
    o = jnp.dot(p.astype(BF16), v_ref[...], preferred_element_type=F32)
    o_ref[...] = (o / jnp.sum(p, axis=-1, keepdims=True)).astype(BF16)


def _context_attention(att, qkv):
    ctx0 = N_LAT // CTX_LEN
    return pl.pallas_call(
        _ctx_attn_kernel,
        grid=(BATCH, NA_HEADS),
        in_specs=[
            pl.BlockSpec(memory_space=pl.ANY),
            pl.BlockSpec((CTX_LEN, HEAD_DIM), lambda b, h: (ctx0 + b, h)),
            pl.BlockSpec((CTX_LEN, HEAD_DIM), lambda b, h: (ctx0 + b, NA_HEADS + h)),
            pl.BlockSpec((CTX_LEN, HEAD_DIM), lambda b, h: (ctx0 + b, 2 * NA_HEADS + h)),
        ],
        out_specs=pl.BlockSpec((CTX_LEN, HEAD_DIM), lambda b, h: (ctx0 + b, h)),
        out_shape=jax.ShapeDtypeStruct((N_TOK, NA_W), BF16),
        input_output_aliases={0: 0},
        compiler_params=_cparams(("arbitrary", "arbitrary")),
        name="context_attention",
    )(att, qkv, qkv, qkv)


def _dft_tables():
    n1 = GRID_W
    idx = np.arange(n1)
    ang1 = 2 * np.pi * np.outer(idx, idx) / n1
    w1 = np.concatenate([np.cos(ang1), -np.sin(ang1)], axis=0)
    q = idx[:, None, None]
    p = idx[None, :, None]
    b = idx[None, None, :]
    ang2 = 2 * np.pi * b * (n1 * p + q) / SEQ
    wc, ws = np.cos(ang2), np.sin(ang2)
    m2 = np.concatenate([np.concatenate([wc, ws], axis=2),
                         np.concatenate([-ws, wc], axis=2)], axis=1)
    d = np.arange(HEAD_DIM)
    ang3 = 2 * np.pi * np.outer(d, d) / HEAD_DIM
    cs = np.concatenate([np.cos(ang3), np.sin(ang3)], axis=0) / math.sqrt(SEQ * HEAD_DIM)
    n = np.arange(CTX_LEN)
    angc = 2 * np.pi * np.outer(n, n) / CTX_LEN
    t1 = np.concatenate([np.cos(angc), np.sin(angc)], axis=0)
    cs_ctx = np.concatenate([np.cos(ang3), -np.sin(ang3)], axis=0) / math.sqrt(CTX_LEN * HEAD_DIM)
    as_bf16 = lambda a: jnp.asarray(a, dtype=F32).astype(BF16)
    return as_bf16(w1), as_bf16(m2), as_bf16(cs), as_bf16(t1), as_bf16(cs_ctx)


def _fourier_kernel(x_ref, w1_ref, m2_ref, cs_ref, o_ref, re_scr, im_scr, y_scr):
    n1 = GRID_W
    w1 = w1_ref[...]
    for b in range(n1):
        a = jnp.dot(w1, x_ref[:, b * F_W:(b + 1) * F_W], preferred_element_type=F32)
        for g in range(F_GROUPS):
            sl = slice(g * HEAD_DIM, (g + 1) * HEAD_DIM)
            re_scr[g, pl.ds(b, n1, stride=PITCH), :] = a[:n1, sl]
            im_scr[g, pl.ds(b, n1, stride=PITCH), :] = a[n1:, sl]

    cs = cs_ref[...]

    def per_q(q, carry):
        row = pl.multiple_of(q * PITCH, 8)
        a = jnp.concatenate(
            [jnp.concatenate([re_scr[g, pl.ds(row, n1), :] for g in range(F_GROUPS)], axis=1),
             jnp.concatenate([im_scr[g, pl.ds(row, n1), :] for g in range(F_GROUPS)], axis=1)], axis=0)
        z = jnp.dot(m2_ref[q], a.astype(BF16), preferred_element_type=F32)
        lhs = jnp.concatenate(
            [jnp.concatenate([z[:n1, g * HEAD_DIM:(g + 1) * HEAD_DIM],
                              z[n1:, g * HEAD_DIM:(g + 1) * HEAD_DIM]], axis=1)
             for g in range(F_GROUPS)], axis=0).astype(BF16)
        y = jnp.dot(lhs, cs, preferred_element_type=F32)
        for g in range(F_GROUPS):
            y_scr[g, pl.ds(q, n1, stride=PITCH), :] = y[g * n1:(g + 1) * n1]
        return carry

    lax.fori_loop(0, n1, per_q, 0, unroll=2)
    for p in range(n1):
        for g in range(F_GROUPS):
            o_ref[p * n1:(p + 1) * n1, g * HEAD_DIM:(g + 1) * HEAD_DIM] = (
                y_scr[g, p * PITCH:p * PITCH + n1, :].astype(BF16))


def _fourier_latent(f_all, w1, m2, cs):
    n1 = GRID_W
    fx = f_all.reshape(f_all.shape[0] // n1, n1 * F_W)
    return pl.pallas_call(
        _fourier_kernel,
        grid=(BATCH,),
        in_specs=[
            pl.BlockSpec((n1, n1 * F_W), lambda b: (b, 0)),
            pl.BlockSpec((2 * n1, n1), lambda b: (0, 0)),
            pl.BlockSpec((n1, 2 * n1, 2 * n1), lambda b: (0, 0, 0)),
            pl.BlockSpec((2 * HEAD_DIM, HEAD_DIM), lambda b: (0, 0)),
        ],
        out_specs=pl.BlockSpec((SEQ, F_W), lambda b: (b, 0)),
        out_shape=jax.ShapeDtypeStruct((N_TOK, F_W), BF16),
        scratch_shapes=[pltpu.VMEM((F_GROUPS, n1 * PITCH, HEAD_DIM), F32)] * 3,
        compiler_params=_cparams(("arbitrary",)),
        name="fourier_latent",
    )(fx, w1, m2, cs)


def _fourier_ctx_kernel(y_in_ref, x_ref, t1_ref, cs_ref, o_ref):
    del y_in_ref
    t = jnp.dot(t1_ref[...], x_ref[...], preferred_element_type=F32)
    cs = cs_ref[...]
    for g in range(F_GROUPS):
        sl = slice(g * HEAD_DIM, (g + 1) * HEAD_DIM)
        lhs = jnp.concatenate([t[:CTX_LEN, sl], t[CTX_LEN:, sl]], axis=1).astype(BF16)
        o_ref[:, sl] = jnp.dot(lhs, cs, preferred_element_type=F32).astype(BF16)


def _fourier_context(y, f_all, t1, cs_ctx):
    ctx0 = N_LAT // CTX_LEN
    return pl.pallas_call(
        _fourier_ctx_kernel,
        grid=(BATCH,),
        in_specs=[
            pl.BlockSpec(memory_space=pl.ANY),
            pl.BlockSpec((CTX_LEN, F_W), lambda b: (ctx0 + b, 0)),
            pl.BlockSpec((2 * CTX_LEN, CTX_LEN), lambda b: (0, 0)),
            pl.BlockSpec((2 * HEAD_DIM, HEAD_DIM), lambda b: (0, 0)),
        ],
        out_specs=pl.BlockSpec((CTX_LEN, F_W), lambda b: (ctx0 + b, 0)),
        out_shape=jax.ShapeDtypeStruct((N_TOK, F_W), BF16),
        input_output_aliases={0: 0},
        compiler_params=_cparams(("arbitrary",)),
        name="fourier_context",
    )(y, f_all, t1, cs_ctx)


def _gating_kernel(z_ref, nw_ref, ws_ref, bs_ref, o_ref):
    nw = nw_ref[...]
    for g in range(C_GROUPS):
        sl = slice(g * HEAD_DIM, (g + 1) * HEAD_DIM)
        v = z_ref[:, C_W + g * HEAD_DIM:C_W + (g + 1) * HEAD_DIM].astype(F32)
        mu = jnp.mean(v, axis=-1, keepdims=True)
        vc = v - mu
        var = jnp.mean(vc * vc, axis=-1, keepdims=True)
        vn = (vc * lax.rsqrt(var + EPS) * nw[:, sl]).astype(BF16)
        w = ws_ref[g]
        bias = bs_ref[g]
        for c in range(TM // CHUNK):
            rows = slice(c * CHUNK, (c + 1) * CHUNK)
            sp = jnp.dot(w, vn[rows], preferred_element_type=F32) + bias
            o_ref[rows, sl] = (z_ref[rows, sl].astype(F32) * sp).astype(BF16)


def _spatial_gating(z, norm_w, ws, bs):
    bs_exp = jnp.broadcast_to(bs[:, :, None], (C_GROUPS, CHUNK, HEAD_DIM))
    return pl.pallas_call(
        _gating_kernel,
        grid=(z.shape[0] // TM,),
        in_specs=[
            pl.BlockSpec((TM, 2 * C_W), lambda i: (i, 0)),
            pl.BlockSpec((1, C_W), lambda i: (0, 0)),
            pl.BlockSpec((C_GROUPS, CHUNK, CHUNK), lambda i: (0, 0, 0)),
            pl.BlockSpec((C_GROUPS, CHUNK, HEAD_DIM), lambda i: (0, 0, 0)),
        ],
        out_specs=pl.BlockSpec((TM, C_W), lambda i: (i, 0)),
        out_shape=jax.ShapeDtypeStruct((z.shape[0], C_W), BF16),
        compiler_params=_cparams(("arbitrary",)),
        name="spatial_gating",
    )(z, norm_w.reshape(1, C_W), ws.astype(BF16), bs_exp)


def _merge_kernel(y_ref, att_ref, sg_ref, gate_ref, xa_ref, xb_ref, g1_ref, sh_ref, sc_ref, nw_ref,
                  wf_ref, wna_ref, wc_ref, wo_ref, o_ref, h_ref, m_scr, *, n_a_tiles):
    i = pl.program_id(0)
    r = _mod_row(i, TM_RES)
    y = y_ref[...]
    att = att_ref[...]
    sg = sg_ref[...]
    for c in range(D_MODEL // TN):
        sl = slice(c * TN, (c + 1) * TN)
        a_f = jnp.dot(y, wf_ref[:, sl], preferred_element_type=F32)
        a_na = jnp.dot(att, wna_ref[:, sl], preferred_element_type=F32)
        a_c = jnp.dot(sg, wc_ref[:, sl], preferred_element_type=F32)
        g_f = gate_ref[:, c * TN:(c + 1) * TN].astype(F32)
        g_na = gate_ref[:, D_MODEL + c * TN:D_MODEL + (c + 1) * TN].astype(F32)
        g_c = gate_ref[:, 2 * D_MODEL + c * TN:2 * D_MODEL + (c + 1) * TN].astype(F32)
        m_scr[:, sl] = (g_f * a_f + g_na * a_na + g_c * a_c).astype(BF16)
    m = m_scr[...]
    from_a = i < n_a_tiles
    for c in range(D_MODEL // TN):
        sl = slice(c * TN, (c + 1) * TN)
        o = jnp.dot(m, wo_ref[:, sl], preferred_element_type=F32)
        x = jnp.where(from_a, xa_ref[:, sl], xb_ref[:, sl])
        o_ref[:, sl] = x + g1_ref[pl.ds(r, 1), sl] * o
    h = _norm_mod(o_ref[...], nw_ref[...], sh_ref[pl.ds(r, 1), :], sc_ref[pl.ds(r, 1), :])
    h_ref[...] = h.astype(BF16)


def _merge(y, att, sg, gate, xa, xb, mod, norm2_w, wf, wna, wc, wo, n_rows):
    tm = TM_RES
    na = xa.shape[0] // tm
    nbt = xb.shape[0] // tm
    return pl.pallas_call(
        functools.partial(_merge_kernel, n_a_tiles=na),
        grid=(n_rows // tm,),
        in_specs=[
            pl.BlockSpec((tm, F_W), lambda i: (i, 0)),
            pl.BlockSpec((tm, NA_W), lambda i: (i, 0)),
            pl.BlockSpec((tm, C_W), lambda i: (i, 0)),
            pl.BlockSpec((tm, 3 * D_MODEL), lambda i: (i, 0)),
            pl.BlockSpec((tm, D_MODEL), lambda i: (jnp.minimum(i, na - 1), 0)),
            pl.BlockSpec((tm, D_MODEL), lambda i: (jnp.clip(i - na, 0, nbt - 1), 0)),
            pl.BlockSpec((MOD_ROWS, D_MODEL), lambda i: (0, 2)),
            pl.BlockSpec((MOD_ROWS, D_MODEL), lambda i: (0, 3)),
            pl.BlockSpec((MOD_ROWS, D_MODEL), lambda i: (0, 4)),
            pl.BlockSpec((1, D_MODEL), lambda i: (0, 0)),
            _resident((F_W, D_MODEL)),
            _resident((NA_W, D_MODEL)),
            _resident((C_W, D_MODEL)),
            _resident((D_MODEL, D_MODEL)),
        ],
        out_specs=[pl.BlockSpec((tm, D_MODEL), lambda i: (i, 0)),
                   pl.BlockSpec((tm, D_MODEL), lambda i: (i, 0))],
        out_shape=[jax.ShapeDtypeStruct((n_rows, D_MODEL), F32),
                   jax.ShapeDtypeStruct((n_rows, D_MODEL), BF16)],
        scratch_shapes=[pltpu.VMEM((tm, D_MODEL), BF16)],
        compiler_params=_cparams(("arbitrary",)),
        name="gated_merge",
    )(y, att, sg, gate, xa, xb, mod, mod, mod, norm2_w.reshape(1, D_MODEL), wf, wna, wc, wo)


_FF_TILES = D_FF // TN


def _ffn_up_kernel(h_ref, hp_ref, hn_ref, wa_ref, wg_ref, cw_ref, cb_ref, o_ref, wa_scr, wg_scr, a_scr,
                   ):
    i = pl.program_id(1)
    tp = h_ref.shape[0]
    rc = tp // N_CHUNKS

    @pl.when(i == 0)
    def _():
        wa_scr[...] = wa_ref[...].astype(BF16)
        wg_scr[...] = wg_ref[...].astype(BF16)

    n_chunks = N_CHUNKS
    for c in range(n_chunks):
        parts = [h_ref[c * rc:(c + 1) * rc, :]]
        lo, hi = HALO + c * rc, HALO + (c + 1) * rc
        if c == 0:
            parts.insert(0, hp_ref[...])
            lo -= HALO
        if c == n_chunks - 1:
            parts.append(hn_ref[...])
            hi += HALO
        lhs = parts[0] if len(parts) == 1 else jnp.concatenate(parts, axis=0)
        a_scr[lo:hi, :] = jnp.dot(lhs, wa_scr[...], preferred_element_type=F32)

    w0, w1, w2, cb = cw_ref[0:1], cw_ref[1:2], cw_ref[2:3], cb_ref[...]
    for c in range(n_chunks):
        r0 = HALO + c * rc
        gv = jnp.dot(h_ref[c * rc:(c + 1) * rc, :], wg_scr[...], preferred_element_type=F32)
        row = i * tp + c * rc + lax.broadcasted_iota(jnp.int32, (rc, TN), 0)
        seq_mask = jnp.where(row < N_LAT, SEQ - 1, CTX_LEN - 1)
        pos = row & seq_mask
        prev = jnp.where(pos != 0, a_scr[r0 - 1:r0 - 1 + rc, :], 0.0)
        nxt = jnp.where(pos != seq_mask, a_scr[r0 + 1:r0 + 1 + rc, :], 0.0)
        a = cb + w0 * prev + w1 * a_scr[r0:r0 + rc, :] + w2 * nxt
        o_ref[c * rc:(c + 1) * rc, :] = (a * jax.nn.sigmoid(a) * gv).astype(BF16)


def _ffn_up(h2, w_up_all, layer, conv_w, conv_b, n_rows):
    tp = _proj_tile(n_rows)
    hb = tp // HALO
    last = h2.shape[0] // HALO - 1

    def wspec(off):
        return pl.BlockSpec((pl.Squeezed(), D_MODEL, TN), lambda j, i: (layer, 0, off + j))

    return pl.pallas_call(
        _ffn_up_kernel,
        grid=(_FF_TILES, n_rows // tp),
        in_specs=[
            pl.BlockSpec((tp, D_MODEL), lambda j, i: (i, 0)),
            pl.BlockSpec((HALO, D_MODEL), lambda j, i: (jnp.maximum(i * hb - 1, 0), 0)),
            pl.BlockSpec((HALO, D_MODEL), lambda j, i: (jnp.minimum((i + 1) * hb, last), 0)),
            wspec(0),
            wspec(_FF_TILES),
            pl.BlockSpec((CONV_W, TN), lambda j, i: (0, j)),
            pl.BlockSpec((1, TN), lambda j, i: (0, j)),
        ],
        out_specs=pl.BlockSpec((tp, TN), lambda j, i: (i, j)),
        out_shape=jax.ShapeDtypeStruct((n_rows, D_FF), BF16),
        scratch_shapes=[pltpu.VMEM((D_MODEL, TN), BF16), pltpu.VMEM((D_MODEL, TN), BF16),
                        pltpu.VMEM((tp + 2 * HALO, TN), F32)],
        compiler_params=_cparams(("arbitrary", "arbitrary")),
        name="ffn_up_conv",
    )(h2, h2, h2, w_up_all, w_up_all, conv_w, conv_b.reshape(1, D_FF))


def _ffn_down_kernel(a_ref, w_ref, x_ref, g2_ref, sh_ref, sc_ref, nw_ref, *refs, final):
    r = _mod_row(pl.program_id(0), TM_RES)
    xs_ref = refs[1] if final else refs[0]
    act = a_ref[...]
    for c in range(D_MODEL // TN):
        sl = slice(c * TN, (c + 1) * TN)
        o = jnp.dot(act, w_ref[:, sl], preferred_element_type=F32)
        xs_ref[:, sl] = x_ref[:, sl] + g2_ref[pl.ds(r, 1), sl] * o
    if final:
        refs[0][...] = _rms(xs_ref[...], nw_ref[...])
    else:
        h = _norm_mod(xs_ref[...], nw_ref[...], sh_ref[pl.ds(r, 1), :], sc_ref[pl.ds(r, 1), :])
        refs[1][...] = h.astype(BF16)


def _ffn_down(act, w_down, x_all, mod, mod_next, norm_w, n_rows, final):
    tm = TM_RES
    row = lambda i: (i, 0)
    if final:
        out_specs = pl.BlockSpec((tm, D_MODEL), row)
        out_shape = jax.ShapeDtypeStruct((n_rows, D_MODEL), F32)
        scratch = [pltpu.VMEM((tm, D_MODEL), F32)]
    else:
        out_specs = [pl.BlockSpec((tm, D_MODEL), row), pl.BlockSpec((tm, D_MODEL), row)]
        out_shape = [jax.ShapeDtypeStruct((n_rows, D_MODEL), F32),
                     jax.ShapeDtypeStruct((n_rows, D_MODEL), BF16)]
        scratch = []
    return pl.pallas_call(
        functools.partial(_ffn_down_kernel, final=final),
        grid=(n_rows // tm,),
        in_specs=[
            pl.BlockSpec((tm, D_FF), row),
            _resident((D_FF, D_MODEL)),
            pl.BlockSpec((tm, D_MODEL), row),
            pl.BlockSpec((MOD_ROWS, D_MODEL), lambda i: (0, 5)),
            pl.BlockSpec((MOD_ROWS, D_MODEL), lambda i: (0, 0)),
            pl.BlockSpec((MOD_ROWS, D_MODEL), lambda i: (0, 1)),
            pl.BlockSpec((1, D_MODEL), lambda i: (0, 0)),
        ],
        out_specs=out_specs,
        out_shape=out_shape,
        scratch_shapes=scratch,
        compiler_params=_cparams(("arbitrary",)),
        name="ffn_down",
    )(act, w_down, x_all, mod, mod_next, mod_next, norm_w.reshape(1, D_MODEL))


def kernel(x, c, ctx, c_ctx, ada_w, ada_b, norm1_w, norm2_w, w_in, na_rpb, gmlp_norm_w, gmlp_ws,
           gmlp_bs, w_f_out, w_na_out, w_c_out, w_o, ffn_up, ffn_conv_w, ffn_conv_b, ffn_down,
           final_norm_w):
    assert x.shape == (BATCH, SEQ, D_MODEL) and ctx.shape == (BATCH, CTX_LEN, D_MODEL)
    cvec = jnp.concatenate(
        [c, c_ctx[None, :], jnp.zeros((MOD_ROWS - BATCH - 1, D_MODEL), F32)], axis=0)
    mod_all = _modulation(cvec, ada_w, ada_b)
    w1, m2, cs, t1, cs_ctx = _dft_tables()
    xa = x.reshape(N_LAT, D_MODEL)
    xb = ctx.reshape(N_CTX, D_MODEL)
    h1 = _prenorm(xa, xb, mod_all[0], norm1_w[0])

    for l in range(DEPTH):
        last = l == DEPTH - 1
        n_rows = N_LAT if last else N_TOK
        mod = mod_all[l]
        f_all = _projection(h1, w_in, l, 0, F_W, F_W, _ident, "in_proj_f", n_rows)
        qkv = _projection(h1, w_in, l, F_W, 3 * NA_W, 1024, _ident, "in_proj_qkv", N_TOK)
        z = _projection(h1, w_in, l, F_W + 3 * NA_W, 2 * C_W, 1024, _gelu, "in_proj_z", n_rows)
        gate = _projection(h1, w_in, l, F_W + 3 * NA_W + 2 * C_W, 3 * D_MODEL, 1024, jax.nn.sigmoid,
                           "in_proj_gate", n_rows)
        att = _attention(qkv, _bias_table(na_rpb[l]))
        y = _fourier_latent(f_all, w1, m2, cs)
        if not last:
            att = _context_attention(att, qkv)
            y = _fourier_context(y, f_all, t1, cs_ctx)
        sg = _spatial_gating(z, gmlp_norm_w[l], gmlp_ws[l], gmlp_bs[l])
        x_mid, h2 = _merge(y, att, sg, gate, xa, xb, mod, norm2_w[l], w_f_out[l].astype(BF16),
                           w_na_out[l].astype(BF16), w_c_out[l].astype(BF16), w_o[l].astype(BF16),
                           n_rows)
        act = _ffn_up(h2, ffn_up, l, ffn_conv_w[l], ffn_conv_b[l], n_rows)
        if last:
            out = _ffn_down(act, ffn_down[l].astype(BF16), x_mid, mod, mod, final_norm_w, n_rows, True)
        else:
            xa, h1 = _ffn_down(act, ffn_down[l].astype(BF16), x_mid, mod, mod_all[l + 1],
                               norm1_w[l + 1], n_rows, False)
            xb = xa
    return out.reshape(BATCH, SEQ, D_MODEL)
```

```python
import functools
import math

import numpy as np
import jax
import jax.numpy as jnp
from jax import lax
from jax.experimental import pallas as pl
from jax.experimental.pallas import tpu as pltpu

F32 = jnp.float32
BF16 = jnp.bfloat16

D_MODEL = 2048
BATCH = 4
SEQ = 4096
DEPTH = 2
GRID_W = 64
ROWS = SEQ // GRID_W
CTX_LEN = 256
HEAD_DIM = 128
F_GROUPS = 4
F_W = F_GROUPS * HEAD_DIM
NA_HEADS = 8
NA_W = NA_HEADS * HEAD_DIM
NA_KH = 8
NA_KW = 16
C_GROUPS = 4
C_W = C_GROUPS * HEAD_DIM
CHUNK = 128
D_FF = 5632
CONV_W = 3
EPS = 1e-6
NEG_INF = -1e30
IN_W = F_W + 3 * NA_W + 2 * C_W + 3 * D_MODEL

N_LAT = BATCH * SEQ
N_CTX = BATCH * CTX_LEN
N_TOK = N_LAT + N_CTX
MOD_ROWS = 8

TM = 1024
TN = 512
N_CHUNKS = 8
TM_RES = 256
TM_DOWN = 512
HALO = 16
QROWS = 4
KROWS = 12
ATTN_UNROLL = 4
DFT_UNROLL = 8
PITCH = 72
VMEM_LIMIT = 56 * 1024 * 1024


def _cparams(sem):
    return pltpu.CompilerParams(dimension_semantics=sem, vmem_limit_bytes=VMEM_LIMIT)


def _mod_row(i, tm):
    return jnp.minimum((i * tm) // SEQ, BATCH)


def _norm_mod(x, nw, sh, sc):
    ms = jnp.mean(x * x, axis=-1, keepdims=True)
    return (x * lax.rsqrt(ms + EPS) * nw) * (1.0 + sc) + sh


def _rms(x, nw):
    ms = jnp.mean(x * x, axis=-1, keepdims=True)
    return x * lax.rsqrt(ms + EPS) * nw


def _resident(shape):
    return pl.BlockSpec(shape, lambda *_: (0,) * len(shape), pipeline_mode=pl.Buffered(1))


def _mod_kernel(c_ref, w_ref, b_ref, o_ref):
    c = c_ref[...]
    s = (c * jax.nn.sigmoid(c)).astype(BF16)
    o_ref[0] = jnp.dot(s, w_ref[0].astype(BF16), preferred_element_type=F32) + b_ref[0]


def _modulation(cvec, ada_w, ada_b):
    tn = 1024
    return pl.pallas_call(
        _mod_kernel,
        grid=(DEPTH, 6 * D_MODEL // tn),
        in_specs=[
            pl.BlockSpec((MOD_ROWS, D_MODEL), lambda l, j: (0, 0)),
            pl.BlockSpec((1, D_MODEL, tn), lambda l, j: (l, 0, j)),
            pl.BlockSpec((1, 1, tn), lambda l, j: (l, 0, j)),
        ],
        out_specs=pl.BlockSpec((1, MOD_ROWS, tn), lambda l, j: (l, 0, j)),
        out_shape=jax.ShapeDtypeStruct((DEPTH, MOD_ROWS, 6 * D_MODEL), F32),
        compiler_params=_cparams(("arbitrary", "arbitrary")),
        name="modulation",
    )(cvec, ada_w, ada_b.reshape(DEPTH, 1, 6 * D_MODEL))


def _prenorm_kernel(xa_ref, xb_ref, sh_ref, sc_ref, nw_ref, o_ref, *, n_a_tiles):
    i = pl.program_id(0)
    r = _mod_row(i, TM)
    x = jnp.where(i < n_a_tiles, xa_ref[...], xb_ref[...])
    h = _norm_mod(x, nw_ref[...], sh_ref[pl.ds(r, 1), :], sc_ref[pl.ds(r, 1), :])
    o_ref[...] = h.astype(BF16)


def _prenorm(x2d, ctx2d, mod, norm_w):
    na = N_LAT // TM
    return pl.pallas_call(
        functools.partial(_prenorm_kernel, n_a_tiles=na),
        grid=(N_TOK // TM,),
        in_specs=[
            pl.BlockSpec((TM, D_MODEL), lambda i: (jnp.minimum(i, na - 1), 0)),
            pl.BlockSpec((TM, D_MODEL), lambda i: (jnp.maximum(i - na, 0), 0)),
            pl.BlockSpec((MOD_ROWS, D_MODEL), lambda i: (0, 0)),
            pl.BlockSpec((MOD_ROWS, D_MODEL), lambda i: (0, 1)),
            pl.BlockSpec((1, D_MODEL), lambda i: (0, 0)),
        ],
        out_specs=pl.BlockSpec((TM, D_MODEL), lambda i: (i, 0)),
        out_shape=jax.ShapeDtypeStruct((N_TOK, D_MODEL), BF16),
        compiler_params=_cparams(("arbitrary",)),
        name="prenorm",
    )(x2d, ctx2d, mod, mod, norm_w.reshape(1, D_MODEL))


def _ident(a):
    return a


def _gelu(a):
    return 0.5 * a * (1.0 + lax.erf(a * math.sqrt(0.5)))


def _proj_tile(n_rows):
    tp = n_rows // 8
    assert tp * 8 == n_rows and tp % (N_CHUNKS * HALO) == 0
    return tp


def _proj_kernel(h_ref, w_ref, o_ref, w_scr, *, epilogue):
    @pl.when(pl.program_id(1) == 0)
    def _():
        w_scr[...] = w_ref[...].astype(BF16)

    rc = h_ref.shape[0] // N_CHUNKS
    for c in range(N_CHUNKS):
        rows = slice(c * rc, (c + 1) * rc)
        acc = jnp.dot(h_ref[rows, :], w_scr[...], preferred_element_type=F32)
        o_ref[rows, :] = epilogue(acc).astype(BF16)


def _projection(h, w_all, layer, col0, n_cols, tn, epilogue, name, n_rows):
    tp = _proj_tile(n_rows)
    return pl.pallas_call(
        functools.partial(_proj_kernel, epilogue=epilogue),
        grid=(n_cols // tn, n_rows // tp),
        in_specs=[
            pl.BlockSpec((tp, D_MODEL), lambda j, i: (i, 0)),
            pl.BlockSpec((pl.Squeezed(), pl.Element(D_MODEL), pl.Element(tn)),
                         lambda j, i: (layer, 0, pl.multiple_of(col0 + j * tn, 128))),
        ],
        out_specs=pl.BlockSpec((tp, tn), lambda j, i: (i, j)),
        out_shape=jax.ShapeDtypeStruct((n_rows, n_cols), BF16),
        scratch_shapes=[pltpu.VMEM((D_MODEL, tn), BF16)],
        compiler_params=_cparams(("arbitrary", "arbitrary")),
        name=name,
    )(h, w_all)


def _proj_f_kernel(h_ref, w_ref, *rest, with_tok):
    if with_tok:
        og_ref, ot_ref, w_scr, t_scr = rest
    else:
        og_ref, w_scr, t_scr = rest

    @pl.when(pl.program_id(0) == 0)
    def _():
        w_scr[...] = w_ref[...].astype(BF16)

    n1 = GRID_W
    acc = jnp.dot(h_ref[...], w_scr[...], preferred_element_type=F32)
    if with_tok:
        ot_ref[...] = acc.astype(BF16)
    for a in range(TM // n1):
        for g in range(F_GROUPS):
            t_scr[g, a * PITCH:a * PITCH + n1, :] = acc[a * n1:(a + 1) * n1, g * HEAD_DIM:(g + 1) * HEAD_DIM]
    for b in range(n1):
        for g in range(F_GROUPS):
            col = b * F_W + g * HEAD_DIM
            og_ref[:, col:col + HEAD_DIM] = t_scr[g, pl.ds(b, TM // n1, stride=PITCH), :].astype(BF16)


def _projection_f(h, w_all, layer, n_rows, with_tok):
    n1 = GRID_W
    out_specs = [pl.BlockSpec((TM // n1, n1 * F_W), lambda i: (i, 0))]
    out_shape = [jax.ShapeDtypeStruct((n_rows // n1, n1 * F_W), BF16)]
    if with_tok:
        out_specs.append(pl.BlockSpec((TM, F_W), lambda i: (i, 0)))
        out_shape.append(jax.ShapeDtypeStruct((n_rows, F_W), BF16))
    return pl.pallas_call(
        functools.partial(_proj_f_kernel, with_tok=with_tok),
        grid=(n_rows // TM,),
        in_specs=[
            pl.BlockSpec((TM, D_MODEL), lambda i: (i, 0)),
            pl.BlockSpec((pl.Squeezed(), D_MODEL, F_W), lambda i: (layer, 0, 0)),
        ],
        out_specs=out_specs,
        out_shape=out_shape,
        scratch_shapes=[pltpu.VMEM((D_MODEL, F_W), BF16),
                        pltpu.VMEM((F_GROUPS, (TM // n1) * PITCH, HEAD_DIM), F32)],
        compiler_params=_cparams(("arbitrary",)),
        name="in_proj_f",
    )(h, w_all)


_N_QBLK = ROWS // QROWS
_BIAS_VARIANTS = (0, 1, _N_QBLK - 1)


def _key_row0(nb):
    return min(max(QROWS * nb - NA_KH // 2, 0), ROWS - KROWS)


def _bias_kernel(rpb_ref, o_ref, pair_scr):
    h = pl.program_id(0)
    qc = lax.broadcasted_iota(jnp.int32, (GRID_W, 2 * GRID_W), 0)
    lane = lax.broadcasted_iota(jnp.int32, (GRID_W, 2 * GRID_W), 1)
    kc = lane & (GRID_W - 1)
    low = lane < GRID_W
    dc = jnp.clip(kc - qc + NA_KW - 1, 0, 2 * NA_KW - 2)
    c_start = jnp.clip(qc - NA_KW // 2, 0, GRID_W - NA_KW)
    col_in = (kc >= c_start) & (kc < c_start + NA_KW)
    n_dr = 2 * NA_KH - 1
    for dr0 in range(-1, n_dr):
        acc = jnp.zeros((GRID_W, 2 * GRID_W), F32)
        for d in range(2 * NA_KW - 1):
            lo = rpb_ref[h, dr0, d] if 0 <= dr0 < n_dr else 0.0
            hi = rpb_ref[h, dr0 + 1, d] if 0 <= dr0 + 1 < n_dr else 0.0
            acc = jnp.where(dc == d, jnp.where(low, lo, hi), acc)
        pair_scr[dr0 + 1] = acc
    for vi, nb in enumerate(_BIAS_VARIANTS):
        r0 = _key_row0(nb)
        for i in range(QROWS):
            r = QROWS * nb + i
            r_start = min(max(r - NA_KH // 2, 0), ROWS - NA_KH)
            for tp in range(KROWS // 2):
                kr = r0 + 2 * tp
                ok_lo = r_start <= kr < r_start + NA_KH
                ok_hi = r_start <= kr + 1 < r_start + NA_KH
                dr0 = kr - r + NA_KH - 1
                if not (ok_lo or ok_hi):
                    tile = jnp.full((GRID_W, 2 * GRID_W), NEG_INF, F32)
                else:
                    valid = col_in
                    if not ok_lo:
                        valid = valid & jnp.logical_not(low)
                    if not ok_hi:
                        valid = valid & low
                    tile = jnp.where(valid, pair_scr[dr0 + 1], NEG_INF)
                o_ref[vi, 0, i * GRID_W:(i + 1) * GRID_W, tp * 2 * GRID_W:(tp + 1) * 2 * GRID_W] = tile


def _bias_table(rpb):
    nv = len(_BIAS_VARIANTS)
    return pl.pallas_call(
        _bias_kernel,
        grid=(NA_HEADS,),
        in_specs=[pl.BlockSpec(memory_space=pltpu.SMEM)],
        out_specs=pl.BlockSpec((nv, 1, QROWS * GRID_W, KROWS * GRID_W), lambda h: (0, h, 0, 0)),
        out_shape=jax.ShapeDtypeStruct((nv, NA_HEADS, QROWS * GRID_W, KROWS * GRID_W), F32),
        scratch_shapes=[pltpu.VMEM((2 * NA_KH, GRID_W, 2 * GRID_W), F32)],
        compiler_params=_cparams(("arbitrary",)),
        name="attn_bias_table",
    )(rpb)


_QK_DIMS = (((1,), (1,)), ((), ()))
_SCALE = HEAD_DIM ** -0.5
_QB = QROWS * GRID_W


def _attn_kernel(q_ref, k_ref, v_ref, kc_ref, vc_ref, bias_ref, *rest, with_ctx):
    kc = kc_ref[...]
    vc = vc_ref[...]

    def block(nb, carry):
        r0 = jnp.clip(QROWS * nb - NA_KH // 2, 0, ROWS - KROWS)
        start = pl.multiple_of(r0 * GRID_W, GRID_W)
        qrow = pl.multiple_of(nb * _QB, _QB)
        variant = jnp.where(nb == 0, 0, jnp.where(nb == _N_QBLK - 1, 2, 1))
        q = (q_ref[pl.ds(qrow, _QB), :].astype(F32) * _SCALE).astype(BF16)
        kw = k_ref[pl.ds(start, KROWS * GRID_W), :]
        vw = v_ref[pl.ds(start, KROWS * GRID_W), :]
        s_w = lax.dot_general(q, kw, _QK_DIMS, preferred_element_type=F32) + bias_ref[variant, 0]
        s_c = lax.dot_general(q, kc, _QK_DIMS, preferred_element_type=F32)
        m = jnp.maximum(jnp.max(s_w, axis=-1, keepdims=True), jnp.max(s_c, axis=-1, keepdims=True))
        p_w = jnp.exp(s_w - m)
        p_c = jnp.exp(s_c - m)
        denom = jnp.sum(p_w, axis=-1, keepdims=True) + jnp.sum(p_c, axis=-1, keepdims=True)
        o = (jnp.dot(p_w.astype(BF16), vw, preferred_element_type=F32)
             + jnp.dot(p_c.astype(BF16), vc, preferred_element_type=F32))
        o_ref[pl.ds(qrow, _QB), :] = (o / denom).astype(BF16)
        return carry

    if with_ctx:
        qc_ref, o_ref, oc_ref = rest
        q = (qc_ref[...].astype(F32) * _SCALE).astype(BF16)
        s = lax.dot_general(q, kc, _QK_DIMS, preferred_element_type=F32)
        p = jnp.exp(s - jnp.max(s, axis=-1, keepdims=True))
        o = jnp.dot(p.astype(BF16), vc, preferred_element_type=F32)
        oc_ref[...] = (o / jnp.sum(p, axis=-1, keepdims=True)).astype(BF16)
    else:
        (o_ref,) = rest
    lax.fori_loop(0, _N_QBLK, block, 0, unroll=ATTN_UNROLL)


def _attention(qkv, bias, with_ctx):
    ctx0 = N_LAT // CTX_LEN
    nv = len(_BIAS_VARIANTS)
    in_specs = [
        pl.BlockSpec((SEQ, HEAD_DIM), lambda b, h: (b, h)),
        pl.BlockSpec((SEQ, HEAD_DIM), lambda b, h: (b, NA_HEADS + h)),
        pl.BlockSpec((SEQ, HEAD_DIM), lambda b, h: (b, 2 * NA_HEADS + h)),
        pl.BlockSpec((CTX_LEN, HEAD_DIM), lambda b, h: (ctx0 + b, NA_HEADS + h)),
        pl.BlockSpec((CTX_LEN, HEAD_DIM), lambda b, h: (ctx0 + b, 2 * NA_HEADS + h)),
        pl.BlockSpec((nv, 1, _QB, KROWS * GRID_W), lambda b, h: (0, h, 0, 0)),
    ]
    out_specs = [pl.BlockSpec((SEQ, HEAD_DIM), lambda b, h: (b, h))]
    out_shape = [jax.ShapeDtypeStruct((N_LAT, NA_W), BF16)]
    operands = [qkv, qkv, qkv, qkv, qkv, bias]
    if with_ctx:
        in_specs.append(pl.BlockSpec((CTX_LEN, HEAD_DIM), lambda b, h: (ctx0 + b, h)))
        out_specs.append(pl.BlockSpec((CTX_LEN, HEAD_DIM), lambda b, h: (b, h)))
        out_shape.append(jax.ShapeDtypeStruct((N_CTX, NA_W), BF16))
        operands.append(qkv)
    return pl.pallas_call(
        functools.partial(_attn_kernel, with_ctx=with_ctx),
        grid=(BATCH, NA_HEADS),
        in_specs=in_specs,
        out_specs=out_specs,
        out_shape=out_shape,
        compiler_params=_cparams(("arbitrary", "arbitrary")),
        name="neighborhood_attention",
    )(*operands)


def _dft_tables():
    n1 = GRID_W
    idx = np.arange(n1)
    ang1 = 2 * np.pi * np.outer(idx, idx) / n1
    w1 = np.concatenate([np.cos(ang1), -np.sin(ang1)], axis=0)
    q = idx[:, None, None]
    p = idx[None, :, None]
    b = idx[None, None, :]
    ang2 = 2 * np.pi * b * (n1 * p + q) / SEQ
    wc, ws = np.cos(ang2), np.sin(ang2)
    m2 = np.concatenate([np.concatenate([wc, ws], axis=2),
                         np.concatenate([-ws, wc], axis=2)], axis=1)
    d = np.arange(HEAD_DIM)
    ang3 = 2 * np.pi * np.outer(d, d) / HEAD_DIM
    cs = np.concatenate([np.cos(ang3), np.sin(ang3)], axis=0) / math.sqrt(SEQ * HEAD_DIM)
    n = np.arange(CTX_LEN)
    angc = 2 * np.pi * np.outer(n, n) / CTX_LEN
    t1 = np.concatenate([np.cos(angc), np.sin(angc)], axis=0)
    cs_ctx = np.concatenate([np.cos(ang3), -np.sin(ang3)], axis=0) / math.sqrt(CTX_LEN * HEAD_DIM)
    as_bf16 = lambda a: jnp.asarray(a, dtype=F32).astype(BF16)
    return as_bf16(w1), as_bf16(m2), as_bf16(cs), as_bf16(t1), as_bf16(cs_ctx)


def _fourier_kernel(x_ref, w1_ref, m2_ref, cs_ref, *rest, with_ctx):
    if with_ctx:
        xc_ref, t1_ref, csc_ref, o_ref, oc_ref, re_scr, im_scr, y_scr = rest
        t = jnp.dot(t1_ref[...], xc_ref[...], preferred_element_type=F32)
        csc = csc_ref[...]
        for g in range(F_GROUPS):
            sl = slice(g * HEAD_DIM, (g + 1) * HEAD_DIM)
            lhs = jnp.concatenate([t[:CTX_LEN, sl], t[CTX_LEN:, sl]], axis=1).astype(BF16)
            oc_ref[:, sl] = jnp.dot(lhs, csc, preferred_element_type=F32).astype(BF16)
    else:
        o_ref, re_scr, im_scr, y_scr = rest
    n1 = GRID_W
    w1 = w1_ref[...]
    for b in range(n1):
        a = jnp.dot(w1, x_ref[:, b * F_W:(b + 1) * F_W], preferred_element_type=F32)
        for g in range(F_GROUPS):
            sl = slice(g * HEAD_DIM, (g + 1) * HEAD_DIM)
            re_scr[g, pl.ds(b, n1, stride=PITCH), :] = a[:n1, sl]
            im_scr[g, pl.ds(b, n1, stride=PITCH), :] = a[n1:, sl]

    cs = cs_ref[...]

    def per_q(q, carry):
        row = pl.multiple_of(q * PITCH, 8)
        a = jnp.concatenate(
            [jnp.concatenate([re_scr[g, pl.ds(row, n1), :] for g in range(F_GROUPS)], axis=1),
             jnp.concatenate([im_scr[g, pl.ds(row, n1), :] for g in range(F_GROUPS)], axis=1)], axis=0)
        z = jnp.dot(m2_ref[q], a.astype(BF16), preferred_element_type=F32)
        lhs = jnp.concatenate(
            [jnp.concatenate([z[:n1, g * HEAD_DIM:(g + 1) * HEAD_DIM],
                              z[n1:, g * HEAD_DIM:(g + 1) * HEAD_DIM]], axis=1)
             for g in range(F_GROUPS)], axis=0).astype(BF16)
        y = jnp.dot(lhs, cs, preferred_element_type=F32)
        for g in range(F_GROUPS):
            y_scr[g, pl.ds(q, n1, stride=PITCH), :] = y[g * n1:(g + 1) * n1]
        return carry

    lax.fori_loop(0, n1, per_q, 0, unroll=DFT_UNROLL)
    for p in range(n1):
        for g in range(F_GROUPS):
            o_ref[p * n1:(p + 1) * n1, g * HEAD_DIM:(g + 1) * HEAD_DIM] = (
                y_scr[g, p * PITCH:p * PITCH + n1, :].astype(BF16))


def _fourier(fx, f_ctx, tables, with_ctx):
    w1, m2, cs, t1, cs_ctx = tables
    n1 = GRID_W
    const = lambda shape: pl.BlockSpec(shape, lambda b: (0,) * len(shape))
    in_specs = [pl.BlockSpec((n1, n1 * F_W), lambda b: (b, 0)), const((2 * n1, n1)),
                const((n1, 2 * n1, 2 * n1)), const((2 * HEAD_DIM, HEAD_DIM))]
    out_specs = [pl.BlockSpec((SEQ, F_W), lambda b: (b, 0))]
    out_shape = [jax.ShapeDtypeStruct((N_LAT, F_W), BF16)]
    operands = [fx, w1, m2, cs]
    if with_ctx:
        ctx0 = N_LAT // CTX_LEN
        in_specs += [pl.BlockSpec((CTX_LEN, F_W), lambda b: (ctx0 + b, 0)), const((2 * CTX_LEN, CTX_LEN)),
                     const((2 * HEAD_DIM, HEAD_DIM))]
        out_specs.append(pl.BlockSpec((CTX_LEN, F_W), lambda b: (b, 0)))
        out_shape.append(jax.ShapeDtypeStruct((N_CTX, F_W), BF16))
        operands += [f_ctx, t1, cs_ctx]
    return pl.pallas_call(
        functools.partial(_fourier_kernel, with_ctx=with_ctx),
        grid=(BATCH,),
        in_specs=in_specs,
        out_specs=out_specs,
        out_shape=out_shape,
        scratch_shapes=[pltpu.VMEM((F_GROUPS, n1 * PITCH, HEAD_DIM), F32)] * 3,
        compiler_params=_cparams(("arbitrary",)),
        name="fourier_mix",
    )(*operands)


def _gating_kernel(z_ref, nw_ref, ws_ref, bs_ref, o_ref):
    nw = nw_ref[...]
    for g in range(C_GROUPS):
        sl = slice(g * HEAD_DIM, (g + 1) * HEAD_DIM)
        v = z_ref[:, C_W + g * HEAD_DIM:C_W + (g + 1) * HEAD_DIM].astype(F32)
        mu = jnp.mean(v, axis=-1, keepdims=True)
        vc = v - mu
        var = jnp.mean(vc * vc, axis=-1, keepdims=True)
        vn = (vc * lax.rsqrt(var + EPS) * nw[:, sl]).astype(BF16)
        w = ws_ref[g]
        bias = bs_ref[g]
        for c in range(TM // CHUNK):
            rows = slice(c * CHUNK, (c + 1) * CHUNK)
            sp = jnp.dot(w, vn[rows], preferred_element_type=F32) + bias
            o_ref[rows, sl] = (z_ref[rows, sl].astype(F32) * sp).astype(BF16)


def _spatial_gating(z, norm_w, ws, bs):
    bs_exp = jnp.broadcast_to(bs[:, :, None], (C_GROUPS, CHUNK, HEAD_DIM))
    return pl.pallas_call(
        _gating_kernel,
        grid=(z.shape[0] // TM,),
        in_specs=[
            pl.BlockSpec((TM, 2 * C_W), lambda i: (i, 0)),
            pl.BlockSpec((1, C_W), lambda i: (0, 0)),
            pl.BlockSpec((C_GROUPS, CHUNK, CHUNK), lambda i: (0, 0, 0)),
            pl.BlockSpec((C_GROUPS, CHUNK, HEAD_DIM), lambda i: (0, 0, 0)),
        ],
        out_specs=pl.BlockSpec((TM, C_W), lambda i: (i, 0)),
        out_shape=jax.ShapeDtypeStruct((z.shape[0], C_W), BF16),
        compiler_params=_cparams(("arbitrary",)),
        name="spatial_gating",
    )(z, norm_w.reshape(1, C_W), ws.astype(BF16), bs_exp)


def _merge_kernel(ya_ref, yb_ref, atta_ref, attb_ref, sg_ref, gate_ref, xa_ref, xb_ref, g1_ref, sh_ref,
                  sc_ref, nw_ref, wf_ref, wna_ref, wc_ref, wo_ref, o_ref, h_ref, m_scr, *, n_a_tiles):
    i = pl.program_id(0)
    r = _mod_row(i, TM_RES)
    from_a = i < n_a_tiles
    y = jnp.where(from_a, ya_ref[...], yb_ref[...])
    att = jnp.where(from_a, atta_ref[...], attb_ref[...])
    sg = sg_ref[...]
    for c in range(D_MODEL // TN):
        sl = slice(c * TN, (c + 1) * TN)
        a_f = jnp.dot(y, wf_ref[:, sl], preferred_element_type=F32)
        a_na = jnp.dot(att, wna_ref[:, sl], preferred_element_type=F32)
        a_c = jnp.dot(sg, wc_ref[:, sl], preferred_element_type=F32)
        g_f = gate_ref[:, c * TN:(c + 1) * TN].astype(F32)
        g_na = gate_ref[:, D_MODEL + c * TN:D_MODEL + (c + 1) * TN].astype(F32)
        g_c = gate_ref[:, 2 * D_MODEL + c * TN:2 * D_MODEL + (c + 1) * TN].astype(F32)
        m_scr[:, sl] = (g_f * a_f + g_na * a_na + g_c * a_c).astype(BF16)
    m = m_scr[...]
    ss = jnp.zeros((TM_RES, 1), F32)
    for c in range(D_MODEL // TN):
        sl = slice(c * TN, (c + 1) * TN)
        o = jnp.dot(m, wo_ref[:, sl], preferred_element_type=F32)
        x = jnp.where(from_a, xa_ref[:, sl], xb_ref[:, sl])
        xn = x + g1_ref[pl.ds(r, 1), sl] * o
        o_ref[:, sl] = xn
        ss = ss + jnp.sum(xn * xn, axis=-1, keepdims=True)
    wmod = nw_ref[...] * (1.0 + sc_ref[pl.ds(r, 1), :])
    h = o_ref[...] * lax.rsqrt(ss * (1.0 / D_MODEL) + EPS) * wmod + sh_ref[pl.ds(r, 1), :]
    h_ref[...] = h.astype(BF16)


def _merge(ya, yb, atta, attb, sg, gate, xa, xb, mod, norm2_w, wf, wna, wc, wo, n_rows):
    tm = TM_RES
    na = N_LAT // tm
    nbt = max((n_rows - N_LAT) // tm, 1)
    a_spec = lambda w: pl.BlockSpec((tm, w), lambda i: (jnp.minimum(i, na - 1), 0))
    b_spec = lambda w: pl.BlockSpec((tm, w), lambda i: (jnp.clip(i - na, 0, nbt - 1), 0))
    return pl.pallas_call(
        functools.partial(_merge_kernel, n_a_tiles=na),
        grid=(n_rows // tm,),
        in_specs=[
            a_spec(F_W), b_spec(F_W), a_spec(NA_W), b_spec(NA_W),
            pl.BlockSpec((tm, C_W), lambda i: (i, 0)),
            pl.BlockSpec((tm, 3 * D_MODEL), lambda i: (i, 0)),
            a_spec(D_MODEL), b_spec(D_MODEL),
            pl.BlockSpec((MOD_ROWS, D_MODEL), lambda i: (0, 2)),
            pl.BlockSpec((MOD_ROWS, D_MODEL), lambda i: (0, 3)),
            pl.BlockSpec((MOD_ROWS, D_MODEL), lambda i: (0, 4)),
            pl.BlockSpec((1, D_MODEL), lambda i: (0, 0)),
            _resident((F_W, D_MODEL)),
            _resident((NA_W, D_MODEL)),
            _resident((C_W, D_MODEL)),
            _resident((D_MODEL, D_MODEL)),
        ],
        out_specs=[pl.BlockSpec((tm, D_MODEL), lambda i: (i, 0)),
                   pl.BlockSpec((tm, D_MODEL), lambda i: (i, 0))],
        out_shape=[jax.ShapeDtypeStruct((n_rows, D_MODEL), F32),
                   jax.ShapeDtypeStruct((n_rows, D_MODEL), BF16)],
        scratch_shapes=[pltpu.VMEM((tm, D_MODEL), BF16)],
        compiler_params=_cparams(("arbitrary",)),
        name="gated_merge",
    )(ya, yb, atta, attb, sg, gate, xa, xb, mod, mod, mod, norm2_w.reshape(1, D_MODEL), wf, wna, wc, wo)


_FF_TILES = D_FF // TN


def _ffn_up_kernel(h_ref, hp_ref, hn_ref, wa_ref, wg_ref, cw_ref, cb_ref, o_ref, wa_scr, wg_scr, a_scr):
    i = pl.program_id(1)
    tp = h_ref.shape[0]
    rc = tp // N_CHUNKS

    @pl.when(i == 0)
    def _():
        wa_scr[...] = wa_ref[...].astype(BF16)
        wg_scr[...] = wg_ref[...].astype(BF16)

    n_chunks = N_CHUNKS
    for c in range(n_chunks):
        parts = [h_ref[c * rc:(c + 1) * rc, :]]
        lo, hi = HALO + c * rc, HALO + (c + 1) * rc
        if c == 0:
            parts.insert(0, hp_ref[...])
            lo -= HALO
        if c == n_chunks - 1:
            parts.append(hn_ref[...])
            hi += HALO
        lhs = parts[0] if len(parts) == 1 else jnp.concatenate(parts, axis=0)
        a_scr[lo:hi, :] = jnp.dot(lhs, wa_scr[...], preferred_element_type=F32)

    w0, w1, w2, cb = cw_ref[0:1], cw_ref[1:2], cw_ref[2:3], cb_ref[...]
    for c in range(n_chunks):
        r0 = HALO + c * rc
        gv = jnp.dot(h_ref[c * rc:(c + 1) * rc, :], wg_scr[...], preferred_element_type=F32)
        row = i * tp + c * rc + lax.broadcasted_iota(jnp.int32, (rc, TN), 0)
        seq_mask = jnp.where(row < N_LAT, SEQ - 1, CTX_LEN - 1)
        pos = row & seq_mask
        prev = jnp.where(pos != 0, a_scr[r0 - 1:r0 - 1 + rc, :], 0.0)
        nxt = jnp.where(pos != seq_mask, a_scr[r0 + 1:r0 + 1 + rc, :], 0.0)
        a = cb + w0 * prev + w1 * a_scr[r0:r0 + rc, :] + w2 * nxt
        o_ref[c * rc:(c + 1) * rc, :] = (a * jax.nn.sigmoid(a) * gv).astype(BF16)


def _ffn_up(h2, w_up_all, layer, conv_w, conv_b, n_rows):
    tp = _proj_tile(n_rows)
    hb = tp // HALO
    last = h2.shape[0] // HALO - 1

    def wspec(off):
        return pl.BlockSpec((pl.Squeezed(), D_MODEL, TN), lambda j, i: (layer, 0, off + j))

    return pl.pallas_call(
        _ffn_up_kernel,
        grid=(_FF_TILES, n_rows // tp),
        in_specs=[
            pl.BlockSpec((tp, D_MODEL), lambda j, i: (i, 0)),
            pl.BlockSpec((HALO, D_MODEL), lambda j, i: (jnp.maximum(i * hb - 1, 0), 0)),
            pl.BlockSpec((HALO, D_MODEL), lambda j, i: (jnp.minimum((i + 1) * hb, last), 0)),
            wspec(0),
            wspec(_FF_TILES),
            pl.BlockSpec((CONV_W, TN), lambda j, i: (0, j)),
            pl.BlockSpec((1, TN), lambda j, i: (0, j)),
        ],
        out_specs=pl.BlockSpec((tp, TN), lambda j, i: (i, j)),
        out_shape=jax.ShapeDtypeStruct((n_rows, D_FF), BF16),
        scratch_shapes=[pltpu.VMEM((D_MODEL, TN), BF16), pltpu.VMEM((D_MODEL, TN), BF16),
                        pltpu.VMEM((tp + 2 * HALO, TN), F32)],
        compiler_params=_cparams(("arbitrary", "arbitrary")),
        name="ffn_up_conv",
    )(h2, h2, h2, w_up_all, w_up_all, conv_w, conv_b.reshape(1, D_FF))


def _ffn_down_kernel(a_ref, w_ref, x_ref, g2_ref, sh_ref, sc_ref, nw_ref, *refs, final):
    tm = a_ref.shape[0]
    r = _mod_row(pl.program_id(0), tm)
    xs_ref = refs[1] if final else refs[0]
    act = a_ref[...]
    ss = jnp.zeros((tm, 1), F32)
    for c in range(D_MODEL // TN):
        sl = slice(c * TN, (c + 1) * TN)
        o = jnp.dot(act, w_ref[:, sl], preferred_element_type=F32)
        xn = x_ref[:, sl] + g2_ref[pl.ds(r, 1), sl] * o
        xs_ref[:, sl] = xn
        ss = ss + jnp.sum(xn * xn, axis=-1, keepdims=True)
    inv = lax.rsqrt(ss * (1.0 / D_MODEL) + EPS)
    if final:
        refs[0][...] = xs_ref[...] * inv * nw_ref[...]
    else:
        wmod = nw_ref[...] * (1.0 + sc_ref[pl.ds(r, 1), :])
        refs[1][...] = (xs_ref[...] * inv * wmod + sh_ref[pl.ds(r, 1), :]).astype(BF16)


def _ffn_down(act, w_down, x_all, mod, mod_next, norm_w, n_rows, final):
    tm = TM_DOWN
    assert n_rows % tm == 0 and N_LAT % tm == 0
    row = lambda i: (i, 0)
    if final:
        out_specs = pl.BlockSpec((tm, D_MODEL), row)
        out_shape = jax.ShapeDtypeStruct((n_rows, D_MODEL), F32)
        scratch = [pltpu.VMEM((tm, D_MODEL), F32)]
    else:
        out_specs = [pl.BlockSpec((tm, D_MODEL), row), pl.BlockSpec((tm, D_MODEL), row)]
        out_shape = [jax.ShapeDtypeStruct((n_rows, D_MODEL), F32),
                     jax.ShapeDtypeStruct((n_rows, D_MODEL), BF16)]
        scratch = []
    return pl.pallas_call(
        functools.partial(_ffn_down_kernel, final=final),
        grid=(n_rows // tm,),
        in_specs=[
            pl.BlockSpec((tm, D_FF), row),
            _resident((D_FF, D_MODEL)),
            pl.BlockSpec((tm, D_MODEL), row),
            pl.BlockSpec((MOD_ROWS, D_MODEL), lambda i: (0, 5)),
            pl.BlockSpec((MOD_ROWS, D_MODEL), lambda i: (0, 0)),
            pl.BlockSpec((MOD_ROWS, D_MODEL), lambda i: (0, 1)),
            pl.BlockSpec((1, D_MODEL), lambda i: (0, 0)),
        ],
        out_specs=out_specs,
        out_shape=out_shape,
        scratch_shapes=scratch,
        compiler_params=pltpu.CompilerParams(dimension_semantics=("arbitrary",),
                                             vmem_limit_bytes=60 * 1024 * 1024),
        name="ffn_down",
    )(act, w_down, x_all, mod, mod_next, mod_next, norm_w.reshape(1, D_MODEL))


def kernel(x, c, ctx, c_ctx, ada_w, ada_b, norm1_w, norm2_w, w_in, na_rpb, gmlp_norm_w, gmlp_ws,
           gmlp_bs, w_f_out, w_na_out, w_c_out, w_o, ffn_up, ffn_conv_w, ffn_conv_b, ffn_down,
           final_norm_w):
    assert x.shape == (BATCH, SEQ, D_MODEL) and ctx.shape == (BATCH, CTX_LEN, D_MODEL)
    cvec = jnp.concatenate(
        [c, c_ctx[None, :], jnp.zeros((MOD_ROWS - BATCH - 1, D_MODEL), F32)], axis=0)
    mod_all = _modulation(cvec, ada_w, ada_b)
    tables = _dft_tables()
    xa = x.reshape(N_LAT, D_MODEL)
    xb = ctx.reshape(N_CTX, D_MODEL)
    h1 = _prenorm(xa, xb, mod_all[0], norm1_w[0])

    for l in range(DEPTH):
        last = l == DEPTH - 1
        n_rows = N_LAT if last else N_TOK
        mod = mod_all[l]
        fs = _projection_f(h1, w_in, l, n_rows, not last)
        qkv = _projection(h1, w_in, l, F_W, 3 * NA_W, 1024, _ident, "in_proj_qkv", N_TOK)
        z = _projection(h1, w_in, l, F_W + 3 * NA_W, 2 * C_W, 1024, _gelu, "in_proj_z", n_rows)
        gate = _projection(h1, w_in, l, F_W + 3 * NA_W + 2 * C_W, 3 * D_MODEL, 1024, jax.nn.sigmoid,
                           "in_proj_gate", n_rows)
        atts = _attention(qkv, _bias_table(na_rpb[l]), not last)
        ys = _fourier(fs[0], fs[-1], tables, not last)
        sg = _spatial_gating(z, gmlp_norm_w[l], gmlp_ws[l], gmlp_bs[l])
        x_mid, h2 = _merge(ys[0], ys[-1], atts[0], atts[-1], sg, gate, xa, xb, mod, norm2_w[l],
                           w_f_out[l].astype(BF16),
                           w_na_out[l].astype(BF16), w_c_out[l].astype(BF16), w_o[l].astype(BF16),
                           n_rows)
        act = _ffn_up(h2, ffn_up, l, ffn_conv_w[l], ffn_conv_b[l], n_rows)
        if last:
            out = _ffn_down(act, ffn_down[l].astype(BF16), x_mid, mod, mod, final_norm_w, n_rows, True)
        else:
            xa, h1 = _ffn_down(act, ffn_down[l].astype(BF16), x_mid, mod, mod_all[l + 1],
                               norm1_w[l + 1], n_rows, False)
            xb = xa
    return out.reshape(BATCH, SEQ, D_MODEL)
```

```python
import functools
import math

import numpy as np
import jax
import jax.numpy as jnp
from jax import lax
from jax.experimental import pallas as pl
from jax.experimental.pallas import tpu as pltpu

F32 = jnp.float32
BF16 = jnp.bfloat16

D_MODEL = 2048
BATCH = 4
SEQ = 4096
DEPTH = 2
GRID_W = 64
ROWS = SEQ // GRID_W
CTX_LEN = 256
HEAD_DIM = 128
F_GROUPS = 4
F_W = F_GROUPS * HEAD_DIM
NA_HEADS = 8
NA_W = NA_HEADS * HEAD_DIM
NA_KH = 8
NA_KW = 16
C_GROUPS = 4
C_W = C_GROUPS * HEAD_DIM
CHUNK = 128
D_FF = 5632
CONV_W = 3
EPS = 1e-6
NEG_INF = -1e30
IN_W = F_W + 3 * NA_W + 2 * C_W + 3 * D_MODEL

N_LAT = BATCH * SEQ
N_CTX = BATCH * CTX_LEN
N_TOK = N_LAT + N_CTX
MOD_ROWS = 8

TM = 1024
TN = 512
N_CHUNKS = 8
TM_RES = 256
TM_DOWN = 512
HALO = 16
QROWS = 4
KROWS = 12
ATTN_UNROLL = 4
DFT_UNROLL = 8
PITCH = 72
VMEM_LIMIT = 56 * 1024 * 1024


def _cparams(sem):
    return pltpu.CompilerParams(dimension_semantics=sem, vmem_limit_bytes=VMEM_LIMIT)


def _mod_row(i, tm):
    return jnp.minimum((i * tm) // SEQ, BATCH)


def _norm_mod(x, nw, sh, sc):
    ms = jnp.mean(x * x, axis=-1, keepdims=True)
    return (x * lax.rsqrt(ms + EPS) * nw) * (1.0 + sc) + sh


def _rms(x, nw):
    ms = jnp.mean(x * x, axis=-1, keepdims=True)
    return x * lax.rsqrt(ms + EPS) * nw


def _resident(shape):
    return pl.BlockSpec(shape, lambda *_: (0,) * len(shape), pipeline_mode=pl.Buffered(1))


def _mod_kernel(c_ref, w_ref, b_ref, o_ref):
    c = c_ref[...]
    s = (c * jax.nn.sigmoid(c)).astype(BF16)
    o_ref[0] = jnp.dot(s, w_ref[0].astype(BF16), preferred_element_type=F32) + b_ref[0]


def _modulation(cvec, ada_w, ada_b):
    tn = 1024
    return pl.pallas_call(
        _mod_kernel,
        grid=(DEPTH, 6 * D_MODEL // tn),
        in_specs=[
            pl.BlockSpec((MOD_ROWS, D_MODEL), lambda l, j: (0, 0)),
            pl.BlockSpec((1, D_MODEL, tn), lambda l, j: (l, 0, j)),
            pl.BlockSpec((1, 1, tn), lambda l, j: (l, 0, j)),
        ],
        out_specs=pl.BlockSpec((1, MOD_ROWS, tn), lambda l, j: (l, 0, j)),
        out_shape=jax.ShapeDtypeStruct((DEPTH, MOD_ROWS, 6 * D_MODEL), F32),
        compiler_params=_cparams(("arbitrary", "arbitrary")),
        name="modulation",
    )(cvec, ada_w, ada_b.reshape(DEPTH, 1, 6 * D_MODEL))


def _prenorm_kernel(xa_ref, xb_ref, sh_ref, sc_ref, nw_ref, o_ref, *, n_a_tiles):
    i = pl.program_id(0)
    r = _mod_row(i, TM)
    x = jnp.where(i < n_a_tiles, xa_ref[...], xb_ref[...])
    h = _norm_mod(x, nw_ref[...], sh_ref[pl.ds(r, 1), :], sc_ref[pl.ds(r, 1), :])
    o_ref[...] = h.astype(BF16)


def _prenorm(x2d, ctx2d, mod, norm_w):
    na = N_LAT // TM
    return pl.pallas_call(
        functools.partial(_prenorm_kernel, n_a_tiles=na),
        grid=(N_TOK // TM,),
        in_specs=[
            pl.BlockSpec((TM, D_MODEL), lambda i: (jnp.minimum(i, na - 1), 0)),
            pl.BlockSpec((TM, D_MODEL), lambda i: (jnp.maximum(i - na, 0), 0)),
            pl.BlockSpec((MOD_ROWS, D_MODEL), lambda i: (0, 0)),
            pl.BlockSpec((MOD_ROWS, D_MODEL), lambda i: (0, 1)),
            pl.BlockSpec((1, D_MODEL), lambda i: (0, 0)),
        ],
        out_specs=pl.BlockSpec((TM, D_MODEL), lambda i: (i, 0)),
        out_shape=jax.ShapeDtypeStruct((N_TOK, D_MODEL), BF16),
        compiler_params=_cparams(("arbitrary",)),
        name="prenorm",
    )(x2d, ctx2d, mod, mod, norm_w.reshape(1, D_MODEL))


def _ident(a):
    return a


def _gelu(a):
    return 0.5 * a * (1.0 + lax.erf(a * math.sqrt(0.5)))


def _proj_tile(n_rows):
    tp = n_rows // 8
    assert tp * 8 == n_rows and tp % (N_CHUNKS * HALO) == 0
    return tp


def _proj_kernel(h_ref, w_ref, o_ref, w_scr, *, epilogue):
    @pl.when(pl.program_id(1) == 0)
    def _():
        w_scr[...] = w_ref[...].astype(BF16)

    rc = h_ref.shape[0] // N_CHUNKS
    for c in range(N_CHUNKS):
        rows = slice(c * rc, (c + 1) * rc)
        acc = jnp.dot(h_ref[rows, :], w_scr[...], preferred_element_type=F32)
        o_ref[rows, :] = epilogue(acc).astype(BF16)


def _projection(h, w_all, layer, col0, n_cols, tn, epilogue, name, n_rows):
    tp = _proj_tile(n_rows)
    return pl.pallas_call(
        functools.partial(_proj_kernel, epilogue=epilogue),
        grid=(n_cols // tn, n_rows // tp),
        in_specs=[
            pl.BlockSpec((tp, D_MODEL), lambda j, i: (i, 0)),
            pl.BlockSpec((pl.Squeezed(), pl.Element(D_MODEL), pl.Element(tn)),
                         lambda j, i: (layer, 0, pl.multiple_of(col0 + j * tn, 128))),
        ],
        out_specs=pl.BlockSpec((tp, tn), lambda j, i: (i, j)),
        out_shape=jax.ShapeDtypeStruct((n_rows, n_cols), BF16),
        scratch_shapes=[pltpu.VMEM((D_MODEL, tn), BF16)],
        compiler_params=_cparams(("arbitrary", "arbitrary")),
        name=name,
    )(h, w_all)


def _proj_f_kernel(h_ref, w_ref, *rest, with_tok):
    if with_tok:
        og_ref, ot_ref, w_scr, t_scr = rest
    else:
        og_ref, w_scr, t_scr = rest

    @pl.when(pl.program_id(0) == 0)
    def _():
        w_scr[...] = w_ref[...].astype(BF16)

    n1 = GRID_W
    acc = jnp.dot(h_ref[...], w_scr[...], preferred_element_type=F32)
    if with_tok:
        ot_ref[...] = acc.astype(BF16)
    for a in range(TM // n1):
        for g in range(F_GROUPS):
            t_scr[g, a * PITCH:a * PITCH + n1, :] = acc[a * n1:(a + 1) * n1, g * HEAD_DIM:(g + 1) * HEAD_DIM]
    for b in range(n1):
        for g in range(F_GROUPS):
            col = b * F_W + g * HEAD_DIM
            og_ref[:, col:col + HEAD_DIM] = t_scr[g, pl.ds(b, TM // n1, stride=PITCH), :].astype(BF16)


def _projection_f(h, w_all, layer, n_rows, with_tok):
    n1 = GRID_W
    out_specs = [pl.BlockSpec((TM // n1, n1 * F_W), lambda i: (i, 0))]
    out_shape = [jax.ShapeDtypeStruct((n_rows // n1, n1 * F_W), BF16)]
    if with_tok:
        out_specs.append(pl.BlockSpec((TM, F_W), lambda i: (i, 0)))
        out_shape.append(jax.ShapeDtypeStruct((n_rows, F_W), BF16))
    return pl.pallas_call(
        functools.partial(_proj_f_kernel, with_tok=with_tok),
        grid=(n_rows // TM,),
        in_specs=[
            pl.BlockSpec((TM, D_MODEL), lambda i: (i, 0)),
            pl.BlockSpec((pl.Squeezed(), D_MODEL, F_W), lambda i: (layer, 0, 0)),
        ],
        out_specs=out_specs,
        out_shape=out_shape,
        scratch_shapes=[pltpu.VMEM((D_MODEL, F_W), BF16),
                        pltpu.VMEM((F_GROUPS, (TM // n1) * PITCH, HEAD_DIM), F32)],
        compiler_params=_cparams(("arbitrary",)),
        name="in_proj_f",
    )(h, w_all)


_N_QBLK = ROWS // QROWS
_BIAS_VARIANTS = (0, 1, _N_QBLK - 1)


def _key_row0(nb):
    return min(max(QROWS * nb - NA_KH // 2, 0), ROWS - KROWS)


def _bias_kernel(rpb_ref, o_ref, pair_scr):
    h = pl.program_id(0)
    qc = lax.broadcasted_iota(jnp.int32, (GRID_W, 2 * GRID_W), 0)
    lane = lax.broadcasted_iota(jnp.int32, (GRID_W, 2 * GRID_W), 1)
    kc = lane & (GRID_W - 1)
    low = lane < GRID_W
    dc = jnp.clip(kc - qc + NA_KW - 1, 0, 2 * NA_KW - 2)
    c_start = jnp.clip(qc - NA_KW // 2, 0, GRID_W - NA_KW)
    col_in = (kc >= c_start) & (kc < c_start + NA_KW)
    n_dr = 2 * NA_KH - 1
    for dr0 in range(-1, n_dr):
        acc = jnp.zeros((GRID_W, 2 * GRID_W), F32)
        for d in range(2 * NA_KW - 1):
            lo = rpb_ref[h, dr0, d] if 0 <= dr0 < n_dr else 0.0
            hi = rpb_ref[h, dr0 + 1, d] if 0 <= dr0 + 1 < n_dr else 0.0
            acc = jnp.where(dc == d, jnp.where(low, lo, hi), acc)
        pair_scr[dr0 + 1] = acc
    for vi, nb in enumerate(_BIAS_VARIANTS):
        r0 = _key_row0(nb)
        for i in range(QROWS):
            r = QROWS * nb + i
            r_start = min(max(r - NA_KH // 2, 0), ROWS - NA_KH)
            for tp in range(KROWS // 2):
                kr = r0 + 2 * tp
                ok_lo = r_start <= kr < r_start + NA_KH
                ok_hi = r_start <= kr + 1 < r_start + NA_KH
                dr0 = kr - r + NA_KH - 1
                if not (ok_lo or ok_hi):
                    tile = jnp.full((GRID_W, 2 * GRID_W), NEG_INF, F32)
                else:
                    valid = col_in
                    if not ok_lo:
                        valid = valid & jnp.logical_not(low)
                    if not ok_hi:
                        valid = valid & low
                    tile = jnp.where(valid, pair_scr[dr0 + 1], NEG_INF)
                o_ref[vi, 0, i * GRID_W:(i + 1) * GRID_W, tp * 2 * GRID_W:(tp + 1) * 2 * GRID_W] = tile


def _bias_table(rpb):
    nv = len(_BIAS_VARIANTS)
    return pl.pallas_call(
        _bias_kernel,
        grid=(NA_HEADS,),
        in_specs=[pl.BlockSpec(memory_space=pltpu.SMEM)],
        out_specs=pl.BlockSpec((nv, 1, QROWS * GRID_W, KROWS * GRID_W), lambda h: (0, h, 0, 0)),
        out_shape=jax.ShapeDtypeStruct((nv, NA_HEADS, QROWS * GRID_W, KROWS * GRID_W), F32),
        scratch_shapes=[pltpu.VMEM((2 * NA_KH, GRID_W, 2 * GRID_W), F32)],
        compiler_params=_cparams(("arbitrary",)),
        name="attn_bias_table",
    )(rpb)


_QK_DIMS = (((1,), (1,)), ((), ()))
_SCALE = HEAD_DIM ** -0.5
_QB = QROWS * GRID_W


def _attn_kernel(q_ref, k_ref, v_ref, kc_ref, vc_ref, bias_ref, *rest, with_ctx):
    kc = kc_ref[...]
    vc = vc_ref[...]
    if with_ctx:
        qc_ref, o_ref, oc_ref, sw_scr, sc_scr = rest
    else:
        o_ref, sw_scr, sc_scr = rest

    def key_start(nb):
        r0 = jnp.clip(QROWS * nb - NA_KH // 2, 0, ROWS - KROWS)
        return pl.multiple_of(r0 * GRID_W, GRID_W)

    def scores(nb, slot):
        qrow = pl.multiple_of(nb * _QB, _QB)
        variant = jnp.where(nb == 0, 0, jnp.where(nb == _N_QBLK - 1, 2, 1))
        q = (q_ref[pl.ds(qrow, _QB), :].astype(F32) * _SCALE).astype(BF16)
        kw = k_ref[pl.ds(key_start(nb), KROWS * GRID_W), :]
        sw_scr[slot] = lax.dot_general(q, kw, _QK_DIMS, preferred_element_type=F32) + bias_ref[variant, 0]
        sc_scr[slot] = lax.dot_general(q, kc, _QK_DIMS, preferred_element_type=F32)

    def finish(nb, slot):
        qrow = pl.multiple_of(nb * _QB, _QB)
        vw = v_ref[pl.ds(key_start(nb), KROWS * GRID_W), :]
        s_w = sw_scr[slot]
        s_c = sc_scr[slot]
        m = jnp.maximum(jnp.max(s_w, axis=-1, keepdims=True), jnp.max(s_c, axis=-1, keepdims=True))
        p_w = jnp.exp(s_w - m)
        p_c = jnp.exp(s_c - m)
        denom = jnp.sum(p_w, axis=-1, keepdims=True) + jnp.sum(p_c, axis=-1, keepdims=True)
        o = (jnp.dot(p_w.astype(BF16), vw, preferred_element_type=F32)
             + jnp.dot(p_c.astype(BF16), vc, preferred_element_type=F32))
        o_ref[pl.ds(qrow, _QB), :] = (o / denom).astype(BF16)

    if with_ctx:
        q = (qc_ref[...].astype(F32) * _SCALE).astype(BF16)
        s = lax.dot_general(q, kc, _QK_DIMS, preferred_element_type=F32)
        p = jnp.exp(s - jnp.max(s, axis=-1, keepdims=True))
        o = jnp.dot(p.astype(BF16), vc, preferred_element_type=F32)
        oc_ref[...] = (o / jnp.sum(p, axis=-1, keepdims=True)).astype(BF16)

    scores(0, 0)

    def pair(t, carry):
        nb = 2 * t
        scores(nb + 1, 1)
        finish(nb, 0)
        scores(jnp.minimum(nb + 2, _N_QBLK - 1), 0)
        finish(nb + 1, 1)
        return carry

    lax.fori_loop(0, _N_QBLK // 2, pair, 0, unroll=ATTN_UNROLL)


def _attention(qkv, bias, with_ctx):
    ctx0 = N_LAT // CTX_LEN
    nv = len(_BIAS_VARIANTS)
    in_specs = [
        pl.BlockSpec((SEQ, HEAD_DIM), lambda b, h: (b, h)),
        pl.BlockSpec((SEQ, HEAD_DIM), lambda b, h: (b, NA_HEADS + h)),
        pl.BlockSpec((SEQ, HEAD_DIM), lambda b, h: (b, 2 * NA_HEADS + h)),
        pl.BlockSpec((CTX_LEN, HEAD_DIM), lambda b, h: (ctx0 + b, NA_HEADS + h)),
        pl.BlockSpec((CTX_LEN, HEAD_DIM), lambda b, h: (ctx0 + b, 2 * NA_HEADS + h)),
        pl.BlockSpec((nv, 1, _QB, KROWS * GRID_W), lambda b, h: (0, h, 0, 0)),
    ]
    out_specs = [pl.BlockSpec((SEQ, HEAD_DIM), lambda b, h: (b, h))]
    out_shape = [jax.ShapeDtypeStruct((N_LAT, NA_W), BF16)]
    operands = [qkv, qkv, qkv, qkv, qkv, bias]
    if with_ctx:
        in_specs.append(pl.BlockSpec((CTX_LEN, HEAD_DIM), lambda b, h: (ctx0 + b, h)))
        out_specs.append(pl.BlockSpec((CTX_LEN, HEAD_DIM), lambda b, h: (b, h)))
        out_shape.append(jax.ShapeDtypeStruct((N_CTX, NA_W), BF16))
        operands.append(qkv)
    return pl.pallas_call(
        functools.partial(_attn_kernel, with_ctx=with_ctx),
        grid=(BATCH, NA_HEADS),
        in_specs=in_specs,
        out_specs=out_specs,
        out_shape=out_shape,
        scratch_shapes=[pltpu.VMEM((2, _QB, KROWS * GRID_W), F32), pltpu.VMEM((2, _QB, CTX_LEN), F32)],
        compiler_params=_cparams(("arbitrary", "arbitrary")),
        name="neighborhood_attention",
    )(*operands)


def _dft_tables():
    n1 = GRID_W
    idx = np.arange(n1)
    ang1 = 2 * np.pi * np.outer(idx, idx) / n1
    w1 = np.concatenate([np.cos(ang1), -np.sin(ang1)], axis=0)
    q = idx[:, None, None]
    p = idx[None, :, None]
    b = idx[None, None, :]
    ang2 = 2 * np.pi * b * (n1 * p + q) / SEQ
    wc, ws = np.cos(ang2), np.sin(ang2)
    m2 = np.concatenate([np.concatenate([wc, ws], axis=2),
                         np.concatenate([-ws, wc], axis=2)], axis=1)
    d = np.arange(HEAD_DIM)
    ang3 = 2 * np.pi * np.outer(d, d) / HEAD_DIM
    cs = np.concatenate([np.cos(ang3), np.sin(ang3)], axis=0) / math.sqrt(SEQ * HEAD_DIM)
    n = np.arange(CTX_LEN)
    angc = 2 * np.pi * np.outer(n, n) / CTX_LEN
    t1 = np.concatenate([np.cos(angc), np.sin(angc)], axis=0)
    cs_ctx = np.concatenate([np.cos(ang3), -np.sin(ang3)], axis=0) / math.sqrt(CTX_LEN * HEAD_DIM)
    as_bf16 = lambda a: jnp.asarray(a, dtype=F32).astype(BF16)
    return as_bf16(w1), as_bf16(m2), as_bf16(cs), as_bf16(t1), as_bf16(cs_ctx)


def _fourier_kernel(x_ref, w1_ref, m2_ref, cs_ref, *rest, with_ctx):
    if with_ctx:
        xc_ref, t1_ref, csc_ref, o_ref, oc_ref, re_scr, im_scr, y_scr = rest
        t = jnp.dot(t1_ref[...], xc_ref[...], preferred_element_type=F32)
        csc = csc_ref[...]
        for g in range(F_GROUPS):
            sl = slice(g * HEAD_DIM, (g + 1) * HEAD_DIM)
            lhs = jnp.concatenate([t[:CTX_LEN, sl], t[CTX_LEN:, sl]], axis=1).astype(BF16)
            oc_ref[:, sl] = jnp.dot(lhs, csc, preferred_element_type=F32).astype(BF16)
    else:
        o_ref, re_scr, im_scr, y_scr = rest
    n1 = GRID_W
    w1 = w1_ref[...]
    for b in range(n1):
        a = jnp.dot(w1, x_ref[:, b * F_W:(b + 1) * F_W], preferred_element_type=F32)
        for g in range(F_GROUPS):
            sl = slice(g * HEAD_DIM, (g + 1) * HEAD_DIM)
            re_scr[g, pl.ds(b, n1, stride=PITCH), :] = a[:n1, sl]
            im_scr[g, pl.ds(b, n1, stride=PITCH), :] = a[n1:, sl]

    cs = cs_ref[...]

    def per_q(q, carry):
        row = pl.multiple_of(q * PITCH, 8)
        a = jnp.concatenate(
            [jnp.concatenate([re_scr[g, pl.ds(row, n1), :] for g in range(F_GROUPS)], axis=1),
             jnp.concatenate([im_scr[g, pl.ds(row, n1), :] for g in range(F_GROUPS)], axis=1)], axis=0)
        z = jnp.dot(m2_ref[q], a.astype(BF16), preferred_element_type=F32)
        lhs = jnp.concatenate(
            [jnp.concatenate([z[:n1, g * HEAD_DIM:(g + 1) * HEAD_DIM],
                              z[n1:, g * HEAD_DIM:(g + 1) * HEAD_DIM]], axis=1)
             for g in range(F_GROUPS)], axis=0).astype(BF16)
        y = jnp.dot(lhs, cs, preferred_element_type=F32)
        for g in range(F_GROUPS):
            y_scr[g, pl.ds(q, n1, stride=PITCH), :] = y[g * n1:(g + 1) * n1]
        return carry

    lax.fori_loop(0, n1, per_q, 0, unroll=DFT_UNROLL)
    for p in range(n1):
        for g in range(F_GROUPS):
            o_ref[p * n1:(p + 1) * n1, g * HEAD_DIM:(g + 1) * HEAD_DIM] = (
                y_scr[g, p * PITCH:p * PITCH + n1, :].astype(BF16))


def _fourier(fx, f_ctx, tables, with_ctx):
    w1, m2, cs, t1, cs_ctx = tables
    n1 = GRID_W
    const = lambda shape: pl.BlockSpec(shape, lambda b: (0,) * len(shape))
    in_specs = [pl.BlockSpec((n1, n1 * F_W), lambda b: (b, 0)), const((2 * n1, n1)),
                const((n1, 2 * n1, 2 * n1)), const((2 * HEAD_DIM, HEAD_DIM))]
    out_specs = [pl.BlockSpec((SEQ, F_W), lambda b: (b, 0))]
    out_shape = [jax.ShapeDtypeStruct((N_LAT, F_W), BF16)]
    operands = [fx, w1, m2, cs]
    if with_ctx:
        ctx0 = N_LAT // CTX_LEN
        in_specs += [pl.BlockSpec((CTX_LEN, F_W), lambda b: (ctx0 + b, 0)), const((2 * CTX_LEN, CTX_LEN)),
                     const((2 * HEAD_DIM, HEAD_DIM))]
        out_specs.append(pl.BlockSpec((CTX_LEN, F_W), lambda b: (b, 0)))
        out_shape.append(jax.ShapeDtypeStruct((N_CTX, F_W), BF16))
        operands += [f_ctx, t1, cs_ctx]
    return pl.pallas_call(
        functools.partial(_fourier_kernel, with_ctx=with_ctx),
        grid=(BATCH,),
        in_specs=in_specs,
        out_specs=out_specs,
        out_shape=out_shape,
        scratch_shapes=[pltpu.VMEM((F_GROUPS, n1 * PITCH, HEAD_DIM), F32)] * 3,
        compiler_params=_cparams(("arbitrary",)),
        name="fourier_mix",
    )(*operands)


def _gating_kernel(z_ref, nw_ref, ws_ref, bs_ref, o_ref):
    nw = nw_ref[...]
    for g in range(C_GROUPS):
        sl = slice(g * HEAD_DIM, (g + 1) * HEAD_DIM)
        v = z_ref[:, C_W + g * HEAD_DIM:C_W + (g + 1) * HEAD_DIM].astype(F32)
        mu = jnp.mean(v, axis=-1, keepdims=True)
        vc = v - mu
        var = jnp.mean(vc * vc, axis=-1, keepdims=True)
        vn = (vc * lax.rsqrt(var + EPS) * nw[:, sl]).astype(BF16)
        w = ws_ref[g]
        bias = bs_ref[g]
        for c in range(TM // CHUNK):
            rows = slice(c * CHUNK, (c + 1) * CHUNK)
            sp = jnp.dot(w, vn[rows], preferred_element_type=F32) + bias
            o_ref[rows, sl] = (z_ref[rows, sl].astype(F32) * sp).astype(BF16)


def _spatial_gating(z, norm_w, ws, bs):
    bs_exp = jnp.broadcast_to(bs[:, :, None], (C_GROUPS, CHUNK, HEAD_DIM))
    return pl.pallas_call(
        _gating_kernel,
        grid=(z.shape[0] // TM,),
        in_specs=[
            pl.BlockSpec((TM, 2 * C_W), lambda i: (i, 0)),
            pl.BlockSpec((1, C_W), lambda i: (0, 0)),
            pl.BlockSpec((C_GROUPS, CHUNK, CHUNK), lambda i: (0, 0, 0)),
            pl.BlockSpec((C_GROUPS, CHUNK, HEAD_DIM), lambda i: (0, 0, 0)),
        ],
        out_specs=pl.BlockSpec((TM, C_W), lambda i: (i, 0)),
        out_shape=jax.ShapeDtypeStruct((z.shape[0], C_W), BF16),
        compiler_params=_cparams(("arbitrary",)),
        name="spatial_gating",
    )(z, norm_w.reshape(1, C_W), ws.astype(BF16), bs_exp)


def _merge_kernel(ya_ref, yb_ref, atta_ref, attb_ref, sg_ref, gate_ref, xa_ref, xb_ref, g1_ref, sh_ref,
                  sc_ref, nw_ref, wf_ref, wna_ref, wc_ref, wo_ref, o_ref, h_ref, m_scr, *, n_a_tiles):
    i = pl.program_id(0)
    r = _mod_row(i, TM_RES)
    from_a = i < n_a_tiles
    y = jnp.where(from_a, ya_ref[...], yb_ref[...])
    att = jnp.where(from_a, atta_ref[...], attb_ref[...])
    sg = sg_ref[...]
    for c in range(D_MODEL // TN):
        sl = slice(c * TN, (c + 1) * TN)
        a_f = jnp.dot(y, wf_ref[:, sl], preferred_element_type=F32)
        a_na = jnp.dot(att, wna_ref[:, sl], preferred_element_type=F32)
        a_c = jnp.dot(sg, wc_ref[:, sl], preferred_element_type=F32)
        g_f = gate_ref[:, c * TN:(c + 1) * TN].astype(F32)
        g_na = gate_ref[:, D_MODEL + c * TN:D_MODEL + (c + 1) * TN].astype(F32)
        g_c = gate_ref[:, 2 * D_MODEL + c * TN:2 * D_MODEL + (c + 1) * TN].astype(F32)
        m_scr[:, sl] = (g_f * a_f + g_na * a_na + g_c * a_c).astype(BF16)
    m = m_scr[...]
    ss = jnp.zeros((TM_RES, 1), F32)
    for c in range(D_MODEL // TN):
        sl = slice(c * TN, (c + 1) * TN)
        o = jnp.dot(m, wo_ref[:, sl], preferred_element_type=F32)
        x = jnp.where(from_a, xa_ref[:, sl], xb_ref[:, sl])
        xn = x + g1_ref[pl.ds(r, 1), sl] * o
        o_ref[:, sl] = xn
        ss = ss + jnp.sum(xn * xn, axis=-1, keepdims=True)
    wmod = nw_ref[...] * (1.0 + sc_ref[pl.ds(r, 1), :])
    h = o_ref[...] * lax.rsqrt(ss * (1.0 / D_MODEL) + EPS) * wmod + sh_ref[pl.ds(r, 1), :]
    h_ref[...] = h.astype(BF16)


def _merge(ya, yb, atta, attb, sg, gate, xa, xb, mod, norm2_w, wf, wna, wc, wo, n_rows):
    tm = TM_RES
    na = N_LAT // tm
    nbt = max((n_rows - N_LAT) // tm, 1)
    a_spec = lambda w: pl.BlockSpec((tm, w), lambda i: (jnp.minimum(i, na - 1), 0))
    b_spec = lambda w: pl.BlockSpec((tm, w), lambda i: (jnp.clip(i - na, 0, nbt - 1), 0))
    return pl.pallas_call(
        functools.partial(_merge_kernel, n_a_tiles=na),
        grid=(n_rows // tm,),
        in_specs=[
            a_spec(F_W), b_spec(F_W), a_spec(NA_W), b_spec(NA_W),
            pl.BlockSpec((tm, C_W), lambda i: (i, 0)),
            pl.BlockSpec((tm, 3 * D_MODEL), lambda i: (i, 0)),
            a_spec(D_MODEL), b_spec(D_MODEL),
            pl.BlockSpec((MOD_ROWS, D_MODEL), lambda i: (0, 2)),
            pl.BlockSpec((MOD_ROWS, D_MODEL), lambda i: (0, 3)),
            pl.BlockSpec((MOD_ROWS, D_MODEL), lambda i: (0, 4)),
            pl.BlockSpec((1, D_MODEL), lambda i: (0, 0)),
            _resident((F_W, D_MODEL)),
            _resident((NA_W, D_MODEL)),
            _resident((C_W, D_MODEL)),
            _resident((D_MODEL, D_MODEL)),
        ],
        out_specs=[pl.BlockSpec((tm, D_MODEL), lambda i: (i, 0)),
                   pl.BlockSpec((tm, D_MODEL), lambda i: (i, 0))],
        out_shape=[jax.ShapeDtypeStruct((n_rows, D_MODEL), F32),
                   jax.ShapeDtypeStruct((n_rows, D_MODEL), BF16)],
        scratch_shapes=[pltpu.VMEM((tm, D_MODEL), BF16)],
        compiler_params=_cparams(("arbitrary",)),
        name="gated_merge",
    )(ya, yb, atta, attb, sg, gate, xa, xb, mod, mod, mod, norm2_w.reshape(1, D_MODEL), wf, wna, wc, wo)


_FF_TILES = D_FF // TN


def _ffn_up_kernel(h_ref, hp_ref, hn_ref, wa_ref, wg_ref, cw_ref, cb_ref, o_ref, wa_scr, wg_scr, a_scr):
    i = pl.program_id(1)
    tp = h_ref.shape[0]
    rc = tp // N_CHUNKS

    @pl.when(i == 0)
    def _():
        wa_scr[...] = wa_ref[...].astype(BF16)
        wg_scr[...] = wg_ref[...].astype(BF16)

    n_chunks = N_CHUNKS
    for c in range(n_chunks):
        parts = [h_ref[c * rc:(c + 1) * rc, :]]
        lo, hi = HALO + c * rc, HALO + (c + 1) * rc
        if c == 0:
            parts.insert(0, hp_ref[...])
            lo -= HALO
        if c == n_chunks - 1:
            parts.append(hn_ref[...])
            hi += HALO
        lhs = parts[0] if len(parts) == 1 else jnp.concatenate(parts, axis=0)
        a_scr[lo:hi, :] = jnp.dot(lhs, wa_scr[...], preferred_element_type=F32)

    w0, w1, w2, cb = cw_ref[0:1], cw_ref[1:2], cw_ref[2:3], cb_ref[...]
    for c in range(n_chunks):
        r0 = HALO + c * rc
        gv = jnp.dot(h_ref[c * rc:(c + 1) * rc, :], wg_scr[...], preferred_element_type=F32)
        row = i * tp + c * rc + lax.broadcasted_iota(jnp.int32, (rc, TN), 0)
        seq_mask = jnp.where(row < N_LAT, SEQ - 1, CTX_LEN - 1)
        pos = row & seq_mask
        prev = jnp.where(pos != 0, a_scr[r0 - 1:r0 - 1 + rc, :], 0.0)
        nxt = jnp.where(pos != seq_mask, a_scr[r0 + 1:r0 + 1 + rc, :], 0.0)
        a = cb + w0 * prev + w1 * a_scr[r0:r0 + rc, :] + w2 * nxt
        o_ref[c * rc:(c + 1) * rc, :] = (a * jax.nn.sigmoid(a) * gv).astype(BF16)


def _ffn_up(h2, w_up_all, layer, conv_w, conv_b, n_rows):
    tp = _proj_tile(n_rows)
    hb = tp // HALO
    last = h2.shape[0] // HALO - 1

    def wspec(off):
        return pl.BlockSpec((pl.Squeezed(), D_MODEL, TN), lambda j, i: (layer, 0, off + j))

    return pl.pallas_call(
        _ffn_up_kernel,
        grid=(_FF_TILES, n_rows // tp),
        in_specs=[
            pl.BlockSpec((tp, D_MODEL), lambda j, i: (i, 0)),
            pl.BlockSpec((HALO, D_MODEL), lambda j, i: (jnp.maximum(i * hb - 1, 0), 0)),
            pl.BlockSpec((HALO, D_MODEL), lambda j, i: (jnp.minimum((i + 1) * hb, last), 0)),
            wspec(0),
            wspec(_FF_TILES),
            pl.BlockSpec((CONV_W, TN), lambda j, i: (0, j)),
            pl.BlockSpec((1, TN), lambda j, i: (0, j)),
        ],
        out_specs=pl.BlockSpec((tp, TN), lambda j, i: (i, j)),
        out_shape=jax.ShapeDtypeStruct((n_rows, D_FF), BF16),
        scratch_shapes=[pltpu.VMEM((D_MODEL, TN), BF16), pltpu.VMEM((D_MODEL, TN), BF16),
                        pltpu.VMEM((tp + 2 * HALO, TN), F32)],
        compiler_params=_cparams(("arbitrary", "arbitrary")),
        name="ffn_up_conv",
    )(h2, h2, h2, w_up_all, w_up_all, conv_w, conv_b.reshape(1, D_FF))


def _ffn_down_kernel(a_ref, w_ref, x_ref, g2_ref, sh_ref, sc_ref, nw_ref, *refs, final):
    tm = a_ref.shape[0]
    r = _mod_row(pl.program_id(0), tm)
    xs_ref = refs[1] if final else refs[0]
    act = a_ref[...]
    ss = jnp.zeros((tm, 1), F32)
    for c in range(D_MODEL // TN):
        sl = slice(c * TN, (c + 1) * TN)
        o = jnp.dot(act, w_ref[:, sl], preferred_element_type=F32)
        xn = x_ref[:, sl] + g2_ref[pl.ds(r, 1), sl] * o
        xs_ref[:, sl] = xn
        ss = ss + jnp.sum(xn * xn, axis=-1, keepdims=True)
    inv = lax.rsqrt(ss * (1.0 / D_MODEL) + EPS)
    if final:
        refs[0][...] = xs_ref[...] * inv * nw_ref[...]
    else:
        wmod = nw_ref[...] * (1.0 + sc_ref[pl.ds(r, 1), :])
        refs[1][...] = (xs_ref[...] * inv * wmod + sh_ref[pl.ds(r, 1), :]).astype(BF16)


def _ffn_down(act, w_down, x_all, mod, mod_next, norm_w, n_rows, final):
    tm = TM_DOWN
    assert n_rows % tm == 0 and N_LAT % tm == 0
    row = lambda i: (i, 0)
    if final:
        out_specs = pl.BlockSpec((tm, D_MODEL), row)
        out_shape = jax.ShapeDtypeStruct((n_rows, D_MODEL), F32)
        scratch = [pltpu.VMEM((tm, D_MODEL), F32)]
    else:
        out_specs = [pl.BlockSpec((tm, D_MODEL), row), pl.BlockSpec((tm, D_MODEL), row)]
        out_shape = [jax.ShapeDtypeStruct((n_rows, D_MODEL), F32),
                     jax.ShapeDtypeStruct((n_rows, D_MODEL), BF16)]
        scratch = []
    return pl.pallas_call(
        functools.partial(_ffn_down_kernel, final=final),
        grid=(n_rows // tm,),
        in_specs=[
            pl.BlockSpec((tm, D_FF), row),
            _resident((D_FF, D_MODEL)),
            pl.BlockSpec((tm, D_MODEL), row),
            pl.BlockSpec((MOD_ROWS, D_MODEL), lambda i: (0, 5)),
            pl.BlockSpec((MOD_ROWS, D_MODEL), lambda i: (0, 0)),
            pl.BlockSpec((MOD_ROWS, D_MODEL), lambda i: (0, 1)),
            pl.BlockSpec((1, D_MODEL), lambda i: (0, 0)),
        ],
        out_specs=out_specs,
        out_shape=out_shape,
        scratch_shapes=scratch,
        compiler_params=pltpu.CompilerParams(dimension_semantics=("arbitrary",),
                                             vmem_limit_bytes=60 * 1024 * 1024),
        name="ffn_down",
    )(act, w_down, x_all, mod, mod_next, mod_next, norm_w.reshape(1, D_MODEL))


def kernel(x, c, ctx, c_ctx, ada_w, ada_b, norm1_w, norm2_w, w_in, na_rpb, gmlp_norm_w, gmlp_ws,
           gmlp_bs, w_f_out, w_na_out, w_c_out, w_o, ffn_up, ffn_conv_w, ffn_conv_b, ffn_down,
           final_norm_w):
    assert x.shape == (BATCH, SEQ, D_MODEL) and ctx.shape == (BATCH, CTX_LEN, D_MODEL)
    cvec = jnp.concatenate(
        [c, c_ctx[None, :], jnp.zeros((MOD_ROWS - BATCH - 1, D_MODEL), F32)], axis=0)
    mod_all = _modulation(cvec, ada_w, ada_b)
    tables = _dft_tables()
    xa = x.reshape(N_LAT, D_MODEL)
    xb = ctx.reshape(N_CTX, D_MODEL)
    h1 = _prenorm(xa, xb, mod_all[0], norm1_w[0])

    for l in range(DEPTH):
        last = l == DEPTH - 1
        n_rows = N_LAT if last else N_TOK
        mod = mod_all[l]
        fs = _projection_f(h1, w_in, l, n_rows, not last)
        qkv = _projection(h1, w_in, l, F_W, 3 * NA_W, 1024, _ident, "in_proj_qkv", N_TOK)
        z = _projection(h1, w_in, l, F_W + 3 * NA_W, 2 * C_W, 1024, _gelu, "in_proj_z", n_rows)
        gate = _projection(h1, w_in, l, F_W + 3 * NA_W + 2 * C_W, 3 * D_MODEL, 1024, jax.nn.sigmoid,
                           "in_proj_gate", n_rows)
        atts = _attention(qkv, _bias_table(na_rpb[l]), not last)
        ys = _fourier(fs[0], fs[-1], tables, not last)
        sg = _spatial_gating(z, gmlp_norm_w[l], gmlp_ws[l], gmlp_bs[l])
        x_mid, h2 = _merge(ys[0], ys[-1], atts[0], atts[-1], sg, gate, xa, xb, mod, norm2_w[l],
                           w_f_out[l].astype(BF16),
                           w_na_out[l].astype(BF16), w_c_out[l].astype(BF16), w_o[l].astype(BF16),
                           n_rows)
        act = _ffn_up(h2, ffn_up, l, ffn_conv_w[l], ffn_conv_b[l], n_rows)
        if last:
            out = _ffn_down(act, ffn_down[l].astype(BF16), x_mid, mod, mod, final_norm_w, n_rows, True)
        else:
            xa, h1 = _ffn_down(act, ffn_down[l].astype(BF16), x_mid, mod, mod_all[l + 1],
                               norm1_w[l + 1], n_rows, False)
            xb = xa
    return out.reshape(BATCH, SEQ, D_MODEL)
```

```python
import functools
import math

import numpy as np
import jax
import jax.numpy as jnp
from jax import lax
from jax.experimental import pallas as pl
from jax.experimental.pallas import tpu as pltpu

F32 = jnp.float32
BF16 = jnp.bfloat16

D_MODEL = 2048
BATCH = 4
SEQ = 4096
DEPTH = 2
GRID_W = 64
ROWS = SEQ // GRID_W
CTX_LEN = 256
HEAD_DIM = 128
F_GROUPS = 4
F_W = F_GROUPS * HEAD_DIM
NA_HEADS = 8
NA_W = NA_HEADS * HEAD_DIM
NA_KH = 8
NA_KW = 16
C_GROUPS = 4
C_W = C_GROUPS * HEAD_DIM
CHUNK = 128
D_FF = 5632
CONV_W = 3
EPS = 1e-6
NEG_INF = -1e30
IN_W = F_W + 3 * NA_W + 2 * C_W + 3 * D_MODEL

N_LAT = BATCH * SEQ
N_CTX = BATCH * CTX_LEN
N_TOK = N_LAT + N_CTX
MOD_ROWS = 8

TM = 1024
TN = 512
N_CHUNKS = 8
TM_RES = 256
MERGE_TN = 2048
TM_DOWN = 512
HALO = 16
QROWS = 4
KROWS = 12
ATTN_UNROLL = 4
DFT_UNROLL = 8
PITCH = 72
VMEM_LIMIT = 56 * 1024 * 1024


def _cparams(sem):
    return pltpu.CompilerParams(dimension_semantics=sem, vmem_limit_bytes=VMEM_LIMIT)


def _mod_row(i, tm):
    return jnp.minimum((i * tm) // SEQ, BATCH)


def _norm_mod(x, nw, sh, sc):
    ms = jnp.mean(x * x, axis=-1, keepdims=True)
    return (x * lax.rsqrt(ms + EPS) * nw) * (1.0 + sc) + sh


def _rms(x, nw):
    ms = jnp.mean(x * x, axis=-1, keepdims=True)
    return x * lax.rsqrt(ms + EPS) * nw


def _resident(shape):
    return pl.BlockSpec(shape, lambda *_: (0,) * len(shape), pipeline_mode=pl.Buffered(1))


def _mod_kernel(c_ref, w_ref, b_ref, o_ref):
    c = c_ref[...]
    s = (c * jax.nn.sigmoid(c)).astype(BF16)
    o_ref[0] = jnp.dot(s, w_ref[0].astype(BF16), preferred_element_type=F32) + b_ref[0]


def _modulation(cvec, ada_w, ada_b):
    tn = 1024
    return pl.pallas_call(
        _mod_kernel,
        grid=(DEPTH, 6 * D_MODEL // tn),
        in_specs=[
            pl.BlockSpec((MOD_ROWS, D_MODEL), lambda l, j: (0, 0)),
            pl.BlockSpec((1, D_MODEL, tn), lambda l, j: (l, 0, j)),
            pl.BlockSpec((1, 1, tn), lambda l, j: (l, 0, j)),
        ],
        out_specs=pl.BlockSpec((1, MOD_ROWS, tn), lambda l, j: (l, 0, j)),
        out_shape=jax.ShapeDtypeStruct((DEPTH, MOD_ROWS, 6 * D_MODEL), F32),
        compiler_params=_cparams(("arbitrary", "arbitrary")),
        name="modulation",
    )(cvec, ada_w, ada_b.reshape(DEPTH, 1, 6 * D_MODEL))


def _prenorm_kernel(xa_ref, xb_ref, sh_ref, sc_ref, nw_ref, o_ref, *, n_a_tiles):
    i = pl.program_id(0)
    r = _mod_row(i, TM)
    x = jnp.where(i < n_a_tiles, xa_ref[...], xb_ref[...])
    h = _norm_mod(x, nw_ref[...], sh_ref[pl.ds(r, 1), :], sc_ref[pl.ds(r, 1), :])
    o_ref[...] = h.astype(BF16)


def _prenorm(x2d, ctx2d, mod, norm_w):
    na = N_LAT // TM
    return pl.pallas_call(
        functools.partial(_prenorm_kernel, n_a_tiles=na),
        grid=(N_TOK // TM,),
        in_specs=[
            pl.BlockSpec((TM, D_MODEL), lambda i: (jnp.minimum(i, na - 1), 0)),
            pl.BlockSpec((TM, D_MODEL), lambda i: (jnp.maximum(i - na, 0), 0)),
            pl.BlockSpec((MOD_ROWS, D_MODEL), lambda i: (0, 0)),
            pl.BlockSpec((MOD_ROWS, D_MODEL), lambda i: (0, 1)),
            pl.BlockSpec((1, D_MODEL), lambda i: (0, 0)),
        ],
        out_specs=pl.BlockSpec((TM, D_MODEL), lambda i: (i, 0)),
        out_shape=jax.ShapeDtypeStruct((N_TOK, D_MODEL), BF16),
        compiler_params=_cparams(("arbitrary",)),
        name="prenorm",
    )(x2d, ctx2d, mod, mod, norm_w.reshape(1, D_MODEL))


def _ident(a):
    return a


def _gelu(a):
    return 0.5 * a * (1.0 + lax.erf(a * math.sqrt(0.5)))


def _proj_tile(n_rows):
    tp = n_rows // 8
    assert tp * 8 == n_rows and tp % (N_CHUNKS * HALO) == 0
    return tp


def _proj_kernel(h_ref, w_ref, o_ref, w_scr, *, epilogue):
    @pl.when(pl.program_id(1) == 0)
    def _():
        w_scr[...] = w_ref[...].astype(BF16)

    rc = h_ref.shape[0] // N_CHUNKS
    for c in range(N_CHUNKS):
        rows = slice(c * rc, (c + 1) * rc)
        acc = jnp.dot(h_ref[rows, :], w_scr[...], preferred_element_type=F32)
        o_ref[rows, :] = epilogue(acc).astype(BF16)


def _projection(h, w_all, layer, col0, n_cols, tn, epilogue, name, n_rows):
    tp = _proj_tile(n_rows)
    return pl.pallas_call(
        functools.partial(_proj_kernel, epilogue=epilogue),
        grid=(n_cols // tn, n_rows // tp),
        in_specs=[
            pl.BlockSpec((tp, D_MODEL), lambda j, i: (i, 0)),
            pl.BlockSpec((pl.Squeezed(), pl.Element(D_MODEL), pl.Element(tn)),
                         lambda j, i: (layer, 0, pl.multiple_of(col0 + j * tn, 128))),
        ],
        out_specs=pl.BlockSpec((tp, tn), lambda j, i: (i, j)),
        out_shape=jax.ShapeDtypeStruct((n_rows, n_cols), BF16),
        scratch_shapes=[pltpu.VMEM((D_MODEL, tn), BF16)],
        compiler_params=_cparams(("arbitrary", "arbitrary")),
        name=name,
    )(h, w_all)


def _proj_f_kernel(h_ref, w_ref, *rest, with_tok):
    if with_tok:
        og_ref, ot_ref, w_scr, t_scr = rest
    else:
        og_ref, w_scr, t_scr = rest

    @pl.when(pl.program_id(0) == 0)
    def _():
        w_scr[...] = w_ref[...].astype(BF16)

    n1 = GRID_W
    acc = jnp.dot(h_ref[...], w_scr[...], preferred_element_type=F32)
    if with_tok:
        ot_ref[...] = acc.astype(BF16)
    for a in range(TM // n1):
        for g in range(F_GROUPS):
            t_scr[g, a * PITCH:a * PITCH + n1, :] = acc[a * n1:(a + 1) * n1, g * HEAD_DIM:(g + 1) * HEAD_DIM]
    for b in range(n1):
        for g in range(F_GROUPS):
            col = b * F_W + g * HEAD_DIM
            og_ref[:, col:col + HEAD_DIM] = t_scr[g, pl.ds(b, TM // n1, stride=PITCH), :].astype(BF16)


def _projection_f(h, w_all, layer, n_rows, with_tok):
    n1 = GRID_W
    out_specs = [pl.BlockSpec((TM // n1, n1 * F_W), lambda i: (i, 0))]
    out_shape = [jax.ShapeDtypeStruct((n_rows // n1, n1 * F_W), BF16)]
    if with_tok:
        out_specs.append(pl.BlockSpec((TM, F_W), lambda i: (i, 0)))
        out_shape.append(jax.ShapeDtypeStruct((n_rows, F_W), BF16))
    return pl.pallas_call(
        functools.partial(_proj_f_kernel, with_tok=with_tok),
        grid=(n_rows // TM,),
        in_specs=[
            pl.BlockSpec((TM, D_MODEL), lambda i: (i, 0)),
            pl.BlockSpec((pl.Squeezed(), D_MODEL, F_W), lambda i: (layer, 0, 0)),
        ],
        out_specs=out_specs,
        out_shape=out_shape,
        scratch_shapes=[pltpu.VMEM((D_MODEL, F_W), BF16),
                        pltpu.VMEM((F_GROUPS, (TM // n1) * PITCH, HEAD_DIM), F32)],
        compiler_params=_cparams(("arbitrary",)),
        name="in_proj_f",
    )(h, w_all)


_N_QBLK = ROWS // QROWS
_BIAS_VARIANTS = (0, 1, _N_QBLK - 1)


def _key_row0(nb):
    return min(max(QROWS * nb - NA_KH // 2, 0), ROWS - KROWS)


def _bias_kernel(rpb_ref, o_ref, pair_scr):
    h = pl.program_id(0)
    qc = lax.broadcasted_iota(jnp.int32, (GRID_W, 2 * GRID_W), 0)
    lane = lax.broadcasted_iota(jnp.int32, (GRID_W, 2 * GRID_W), 1)
    kc = lane & (GRID_W - 1)
    low = lane < GRID_W
    dc = jnp.clip(kc - qc + NA_KW - 1, 0, 2 * NA_KW - 2)
    c_start = jnp.clip(qc - NA_KW // 2, 0, GRID_W - NA_KW)
    col_in = (kc >= c_start) & (kc < c_start + NA_KW)
    n_dr = 2 * NA_KH - 1
    for dr0 in range(-1, n_dr):
        acc = jnp.zeros((GRID_W, 2 * GRID_W), F32)
        for d in range(2 * NA_KW - 1):
            lo = rpb_ref[h, dr0, d] if 0 <= dr0 < n_dr else 0.0
            hi = rpb_ref[h, dr0 + 1, d] if 0 <= dr0 + 1 < n_dr else 0.0
            acc = jnp.where(dc == d, jnp.where(low, lo, hi), acc)
        pair_scr[dr0 + 1] = acc
    for vi, nb in enumerate(_BIAS_VARIANTS):
        r0 = _key_row0(nb)
        for i in range(QROWS):
            r = QROWS * nb + i
            r_start = min(max(r - NA_KH // 2, 0), ROWS - NA_KH)
            for tp in range(KROWS // 2):
                kr = r0 + 2 * tp
                ok_lo = r_start <= kr < r_start + NA_KH
                ok_hi = r_start <= kr + 1 < r_start + NA_KH
                dr0 = kr - r + NA_KH - 1
                if not (ok_lo or ok_hi):
                    tile = jnp.full((GRID_W, 2 * GRID_W), NEG_INF, F32)
                else:
                    valid = col_in
                    if not ok_lo:
                        valid = valid & jnp.logical_not(low)
                    if not ok_hi:
                        valid = valid & low
                    tile = jnp.where(valid, pair_scr[dr0 + 1], NEG_INF)
                o_ref[vi, 0, i * GRID_W:(i + 1) * GRID_W, tp * 2 * GRID_W:(tp + 1) * 2 * GRID_W] = tile


def _bias_table(rpb):
    nv = len(_BIAS_VARIANTS)
    return pl.pallas_call(
        _bias_kernel,
        grid=(NA_HEADS,),
        in_specs=[pl.BlockSpec(memory_space=pltpu.SMEM)],
        out_specs=pl.BlockSpec((nv, 1, QROWS * GRID_W, KROWS * GRID_W), lambda h: (0, h, 0, 0)),
        out_shape=jax.ShapeDtypeStruct((nv, NA_HEADS, QROWS * GRID_W, KROWS * GRID_W), F32),
        scratch_shapes=[pltpu.VMEM((2 * NA_KH, GRID_W, 2 * GRID_W), F32)],
        compiler_params=_cparams(("arbitrary",)),
        name="attn_bias_table",
    )(rpb)


_QK_DIMS = (((1,), (1,)), ((), ()))
_SCALE = HEAD_DIM ** -0.5
_QB = QROWS * GRID_W


def _attn_kernel(q_ref, k_ref, v_ref, kc_ref, vc_ref, bias_ref, *rest, with_ctx):
    kc = kc_ref[...]
    vc = vc_ref[...]
    if with_ctx:
        qc_ref, o_ref, oc_ref, sw_scr, sc_scr = rest
    else:
        o_ref, sw_scr, sc_scr = rest

    def key_start(nb):
        r0 = jnp.clip(QROWS * nb - NA_KH // 2, 0, ROWS - KROWS)
        return pl.multiple_of(r0 * GRID_W, GRID_W)

    def scores(nb, slot):
        qrow = pl.multiple_of(nb * _QB, _QB)
        variant = jnp.where(nb == 0, 0, jnp.where(nb == _N_QBLK - 1, 2, 1))
        q = (q_ref[pl.ds(qrow, _QB), :].astype(F32) * _SCALE).astype(BF16)
        kw = k_ref[pl.ds(key_start(nb), KROWS * GRID_W), :]
        sw_scr[slot] = lax.dot_general(q, kw, _QK_DIMS, preferred_element_type=F32) + bias_ref[variant, 0]
        sc_scr[slot] = lax.dot_general(q, kc, _QK_DIMS, preferred_element_type=F32)

    def finish(nb, slot):
        qrow = pl.multiple_of(nb * _QB, _QB)
        vw = v_ref[pl.ds(key_start(nb), KROWS * GRID_W), :]
        s_w = sw_scr[slot]
        s_c = sc_scr[slot]
        m = jnp.maximum(jnp.max(s_w, axis=-1, keepdims=True), jnp.max(s_c, axis=-1, keepdims=True))
        p_w = jnp.exp(s_w - m)
        p_c = jnp.exp(s_c - m)
        denom = jnp.sum(p_w, axis=-1, keepdims=True) + jnp.sum(p_c, axis=-1, keepdims=True)
        o = (jnp.dot(p_w.astype(BF16), vw, preferred_element_type=F32)
             + jnp.dot(p_c.astype(BF16), vc, preferred_element_type=F32))
        o_ref[pl.ds(qrow, _QB), :] = (o / denom).astype(BF16)

    if with_ctx:
        q = (qc_ref[...].astype(F32) * _SCALE).astype(BF16)
        s = lax.dot_general(q, kc, _QK_DIMS, preferred_element_type=F32)
        p = jnp.exp(s - jnp.max(s, axis=-1, keepdims=True))
        o = jnp.dot(p.astype(BF16), vc, preferred_element_type=F32)
        oc_ref[...] = (o / jnp.sum(p, axis=-1, keepdims=True)).astype(BF16)

    scores(0, 0)

    def pair(t, carry):
        nb = 2 * t
        scores(nb + 1, 1)
        finish(nb, 0)
        scores(jnp.minimum(nb + 2, _N_QBLK - 1), 0)
        finish(nb + 1, 1)
        return carry

    lax.fori_loop(0, _N_QBLK // 2, pair, 0, unroll=ATTN_UNROLL)


def _attention(qkv, bias, with_ctx):
    ctx0 = N_LAT // CTX_LEN
    nv = len(_BIAS_VARIANTS)
    in_specs = [
        pl.BlockSpec((SEQ, HEAD_DIM), lambda b, h: (b, h)),
        pl.BlockSpec((SEQ, HEAD_DIM), lambda b, h: (b, NA_HEADS + h)),
        pl.BlockSpec((SEQ, HEAD_DIM), lambda b, h: (b, 2 * NA_HEADS + h)),
        pl.BlockSpec((CTX_LEN, HEAD_DIM), lambda b, h: (ctx0 + b, NA_HEADS + h)),
        pl.BlockSpec((CTX_LEN, HEAD_DIM), lambda b, h: (ctx0 + b, 2 * NA_HEADS + h)),
        pl.BlockSpec((nv, 1, _QB, KROWS * GRID_W), lambda b, h: (0, h, 0, 0)),
    ]
    out_specs = [pl.BlockSpec((SEQ, HEAD_DIM), lambda b, h: (b, h))]
    out_shape = [jax.ShapeDtypeStruct((N_LAT, NA_W), BF16)]
    operands = [qkv, qkv, qkv, qkv, qkv, bias]
    if with_ctx:
        in_specs.append(pl.BlockSpec((CTX_LEN, HEAD_DIM), lambda b, h: (ctx0 + b, h)))
        out_specs.append(pl.BlockSpec((CTX_LEN, HEAD_DIM), lambda b, h: (b, h)))
        out_shape.append(jax.ShapeDtypeStruct((N_CTX, NA_W), BF16))
        operands.append(qkv)
    return pl.pallas_call(
        functools.partial(_attn_kernel, with_ctx=with_ctx),
        grid=(BATCH, NA_HEADS),
        in_specs=in_specs,
        out_specs=out_specs,
        out_shape=out_shape,
        scratch_shapes=[pltpu.VMEM((2, _QB, KROWS * GRID_W), F32), pltpu.VMEM((2, _QB, CTX_LEN), F32)],
        compiler_params=_cparams(("arbitrary", "arbitrary")),
        name="neighborhood_attention",
    )(*operands)


def _dft_tables():
    n1 = GRID_W
    idx = np.arange(n1)
    ang1 = 2 * np.pi * np.outer(idx, idx) / n1
    w1 = np.concatenate([np.cos(ang1), -np.sin(ang1)], axis=0)
    q = idx[:, None, None]
    p = idx[None, :, None]
    b = idx[None, None, :]
    ang2 = 2 * np.pi * b * (n1 * p + q) / SEQ
    wc, ws = np.cos(ang2), np.sin(ang2)
    m2 = np.concatenate([np.concatenate([wc, ws], axis=2),
                         np.concatenate([-ws, wc], axis=2)], axis=1)
    d = np.arange(HEAD_DIM)
    ang3 = 2 * np.pi * np.outer(d, d) / HEAD_DIM
    cs = np.concatenate([np.cos(ang3), np.sin(ang3)], axis=0) / math.sqrt(SEQ * HEAD_DIM)
    n = np.arange(CTX_LEN)
    angc = 2 * np.pi * np.outer(n, n) / CTX_LEN
    t1 = np.concatenate([np.cos(angc), np.sin(angc)], axis=0)
    cs_ctx = np.concatenate([np.cos(ang3), -np.sin(ang3)], axis=0) / math.sqrt(CTX_LEN * HEAD_DIM)
    as_bf16 = lambda a: jnp.asarray(a, dtype=F32).astype(BF16)
    return as_bf16(w1), as_bf16(m2), as_bf16(cs), as_bf16(t1), as_bf16(cs_ctx)


def _fourier_kernel(x_ref, w1_ref, m2_ref, cs_ref, *rest, with_ctx):
    if with_ctx:
        xc_ref, t1_ref, csc_ref, o_ref, oc_ref, re_scr, im_scr, y_scr = rest
        t = jnp.dot(t1_ref[...], xc_ref[...], preferred_element_type=F32)
        csc = csc_ref[...]
        for g in range(F_GROUPS):
            sl = slice(g * HEAD_DIM, (g + 1) * HEAD_DIM)
            lhs = jnp.concatenate([t[:CTX_LEN, sl], t[CTX_LEN:, sl]], axis=1).astype(BF16)
            oc_ref[:, sl] = jnp.dot(lhs, csc, preferred_element_type=F32).astype(BF16)
    else:
        o_ref, re_scr, im_scr, y_scr = rest
    n1 = GRID_W
    w1 = w1_ref[...]
    for b in range(n1):
        a = jnp.dot(w1, x_ref[:, b * F_W:(b + 1) * F_W], preferred_element_type=F32)
        for g in range(F_GROUPS):
            sl = slice(g * HEAD_DIM, (g + 1) * HEAD_DIM)
            re_scr[g, pl.ds(b, n1, stride=PITCH), :] = a[:n1, sl]
            im_scr[g, pl.ds(b, n1, stride=PITCH), :] = a[n1:, sl]

    cs = cs_ref[...]

    def column_dft(q):
        row = pl.multiple_of(q * PITCH, 8)
        a = jnp.concatenate(
            [jnp.concatenate([re_scr[g, pl.ds(row, n1), :] for g in range(F_GROUPS)], axis=1),
             jnp.concatenate([im_scr[g, pl.ds(row, n1), :] for g in range(F_GROUPS)], axis=1)], axis=0)
        return jnp.dot(m2_ref[q], a.astype(BF16), preferred_element_type=F32)

    def channel_dft(q, z):
        lhs = jnp.concatenate(
            [jnp.concatenate([z[:n1, g * HEAD_DIM:(g + 1) * HEAD_DIM],
                              z[n1:, g * HEAD_DIM:(g + 1) * HEAD_DIM]], axis=1)
             for g in range(F_GROUPS)], axis=0).astype(BF16)
        y = jnp.dot(lhs, cs, preferred_element_type=F32)
        for g in range(F_GROUPS):
            y_scr[g, pl.ds(q, n1, stride=PITCH), :] = y[g * n1:(g + 1) * n1]

    def per_group(t, carry):
        qs = [t * DFT_UNROLL + k for k in range(DFT_UNROLL)]
        zs = [column_dft(q) for q in qs]
        for q, z in zip(qs, zs):
            channel_dft(q, z)
        return carry

    lax.fori_loop(0, n1 // DFT_UNROLL, per_group, 0)
    for p in range(n1):
        for g in range(F_GROUPS):
            o_ref[p * n1:(p + 1) * n1, g * HEAD_DIM:(g + 1) * HEAD_DIM] = (
                y_scr[g, p * PITCH:p * PITCH + n1, :].astype(BF16))


def _fourier(fx, f_ctx, tables, with_ctx):
    w1, m2, cs, t1, cs_ctx = tables
    n1 = GRID_W
    const = lambda shape: pl.BlockSpec(shape, lambda b: (0,) * len(shape))
    in_specs = [pl.BlockSpec((n1, n1 * F_W), lambda b: (b, 0)), const((2 * n1, n1)),
                const((n1, 2 * n1, 2 * n1)), const((2 * HEAD_DIM, HEAD_DIM))]
    out_specs = [pl.BlockSpec((SEQ, F_W), lambda b: (b, 0))]
    out_shape = [jax.ShapeDtypeStruct((N_LAT, F_W), BF16)]
    operands = [fx, w1, m2, cs]
    if with_ctx:
        ctx0 = N_LAT // CTX_LEN
        in_specs += [pl.BlockSpec((CTX_LEN, F_W), lambda b: (ctx0 + b, 0)), const((2 * CTX_LEN, CTX_LEN)),
                     const((2 * HEAD_DIM, HEAD_DIM))]
        out_specs.append(pl.BlockSpec((CTX_LEN, F_W), lambda b: (b, 0)))
        out_shape.append(jax.ShapeDtypeStruct((N_CTX, F_W), BF16))
        operands += [f_ctx, t1, cs_ctx]
    return pl.pallas_call(
        functools.partial(_fourier_kernel, with_ctx=with_ctx),
        grid=(BATCH,),
        in_specs=in_specs,
        out_specs=out_specs,
        out_shape=out_shape,
        scratch_shapes=[pltpu.VMEM((F_GROUPS, n1 * PITCH, HEAD_DIM), F32)] * 3,
        compiler_params=_cparams(("arbitrary",)),
        name="fourier_mix",
    )(*operands)


_Z_ROWS = 2 * CHUNK


def _proj_gating_kernel(h_ref, w_ref, nw_ref, ws_ref, bs_ref, o_ref, w_scr):
    @pl.when(pl.program_id(0) == 0)
    def _():
        w_scr[...] = w_ref[...].astype(BF16)

    nw = nw_ref[...]

    def project(r):
        rows = slice(r * _Z_ROWS, (r + 1) * _Z_ROWS)
        return _gelu(jnp.dot(h_ref[rows, :], w_scr[...], preferred_element_type=F32))

    n_r = TM // _Z_ROWS
    z_next = project(0)
    for r in range(n_r):
        z = z_next
        if r + 1 < n_r:
            z_next = project(r + 1)
        for g in range(C_GROUPS):
            sl = slice(g * HEAD_DIM, (g + 1) * HEAD_DIM)
            v = z[:, C_W + g * HEAD_DIM:C_W + (g + 1) * HEAD_DIM]
            mu = jnp.mean(v, axis=-1, keepdims=True)
            vc = v - mu
            var = jnp.mean(vc * vc, axis=-1, keepdims=True)
            vn = (vc * lax.rsqrt(var + EPS) * nw[:, sl]).astype(BF16)
            for c in range(_Z_ROWS // CHUNK):
                cr = slice(c * CHUNK, (c + 1) * CHUNK)
                sp = jnp.dot(ws_ref[g], vn[cr], preferred_element_type=F32) + bs_ref[g]
                o_ref[r * _Z_ROWS + c * CHUNK:r * _Z_ROWS + (c + 1) * CHUNK, sl] = (
                    z[cr, sl] * sp).astype(BF16)


def _projection_gating(h, w_all, layer, norm_w, ws, bs, n_rows):
    bs_exp = jnp.broadcast_to(bs[:, :, None], (C_GROUPS, CHUNK, HEAD_DIM))
    col0 = F_W + 3 * NA_W
    return pl.pallas_call(
        _proj_gating_kernel,
        grid=(n_rows // TM,),
        in_specs=[
            pl.BlockSpec((TM, D_MODEL), lambda i: (i, 0)),
            pl.BlockSpec((pl.Squeezed(), pl.Element(D_MODEL), pl.Element(2 * C_W)),
                         lambda i: (layer, 0, col0)),
            pl.BlockSpec((1, C_W), lambda i: (0, 0)),
            pl.BlockSpec((C_GROUPS, CHUNK, CHUNK), lambda i: (0, 0, 0)),
            pl.BlockSpec((C_GROUPS, CHUNK, HEAD_DIM), lambda i: (0, 0, 0)),
        ],
        out_specs=pl.BlockSpec((TM, C_W), lambda i: (i, 0)),
        out_shape=jax.ShapeDtypeStruct((n_rows, C_W), BF16),
        scratch_shapes=[pltpu.VMEM((D_MODEL, 2 * C_W), BF16)],
        compiler_params=_cparams(("arbitrary",)),
        name="in_proj_gmlp",
    )(h, w_all, norm_w.reshape(1, C_W), ws.astype(BF16), bs_exp)


def _merge_kernel(ya_ref, yb_ref, atta_ref, attb_ref, sg_ref, gate_ref, xa_ref, xb_ref, g1_ref, sh_ref,
                  sc_ref, nw_ref, wf_ref, wna_ref, wc_ref, wo_ref, o_ref, h_ref, m_scr, *, n_a_tiles):
    i = pl.program_id(0)
    r = _mod_row(i, TM_RES)
    from_a = i < n_a_tiles
    y = jnp.where(from_a, ya_ref[...], yb_ref[...])
    att = jnp.where(from_a, atta_ref[...], attb_ref[...])
    sg = sg_ref[...]
    n_c = D_MODEL // MERGE_TN
    cols = lambda c: slice(c * MERGE_TN, (c + 1) * MERGE_TN)

    def branches(c):
        return (jnp.dot(y, wf_ref[:, cols(c)], preferred_element_type=F32),
                jnp.dot(att, wna_ref[:, cols(c)], preferred_element_type=F32),
                jnp.dot(sg, wc_ref[:, cols(c)], preferred_element_type=F32))

    nxt = branches(0)
    for c in range(n_c):
        a_f, a_na, a_c = nxt
        if c + 1 < n_c:
            nxt = branches(c + 1)
        g_f = gate_ref[:, c * MERGE_TN:(c + 1) * MERGE_TN].astype(F32)
        g_na = gate_ref[:, D_MODEL + c * MERGE_TN:D_MODEL + (c + 1) * MERGE_TN].astype(F32)
        g_c = gate_ref[:, 2 * D_MODEL + c * MERGE_TN:2 * D_MODEL + (c + 1) * MERGE_TN].astype(F32)
        m_scr[:, cols(c)] = (g_f * a_f + g_na * a_na + g_c * a_c).astype(BF16)
    m = m_scr[...]
    ss = jnp.zeros((TM_RES, 1), F32)
    out_proj = lambda c: jnp.dot(m, wo_ref[:, cols(c)], preferred_element_type=F32)
    o_nxt = out_proj(0)
    for c in range(n_c):
        sl = cols(c)
        o = o_nxt
        if c + 1 < n_c:
            o_nxt = out_proj(c + 1)
        x = jnp.where(from_a, xa_ref[:, sl], xb_ref[:, sl])
        xn = x + g1_ref[pl.ds(r, 1), sl] * o
        o_ref[:, sl] = xn
        ss = ss + jnp.sum(xn * xn, axis=-1, keepdims=True)
    wmod = nw_ref[...] * (1.0 + sc_ref[pl.ds(r, 1), :])
    h = o_ref[...] * lax.rsqrt(ss * (1.0 / D_MODEL) + EPS) * wmod + sh_ref[pl.ds(r, 1), :]
    h_ref[...] = h.astype(BF16)


def _merge(ya, yb, atta, attb, sg, gate, xa, xb, mod, norm2_w, wf, wna, wc, wo, n_rows):
    tm = TM_RES
    na = N_LAT // tm
    nbt = max((n_rows - N_LAT) // tm, 1)
    a_spec = lambda w: pl.BlockSpec((tm, w), lambda i: (jnp.minimum(i, na - 1), 0))
    b_spec = lambda w: pl.BlockSpec((tm, w), lambda i: (jnp.clip(i - na, 0, nbt - 1), 0))
    return pl.pallas_call(
        functools.partial(_merge_kernel, n_a_tiles=na),
        grid=(n_rows // tm,),
        in_specs=[
            a_spec(F_W), b_spec(F_W), a_spec(NA_W), b_spec(NA_W),
            pl.BlockSpec((tm, C_W), lambda i: (i, 0)),
            pl.BlockSpec((tm, 3 * D_MODEL), lambda i: (i, 0)),
            a_spec(D_MODEL), b_spec(D_MODEL),
            pl.BlockSpec((MOD_ROWS, D_MODEL), lambda i: (0, 2)),
            pl.BlockSpec((MOD_ROWS, D_MODEL), lambda i: (0, 3)),
            pl.BlockSpec((MOD_ROWS, D_MODEL), lambda i: (0, 4)),
            pl.BlockSpec((1, D_MODEL), lambda i: (0, 0)),
            _resident((F_W, D_MODEL)),
            _resident((NA_W, D_MODEL)),
            _resident((C_W, D_MODEL)),
            _resident((D_MODEL, D_MODEL)),
        ],
        out_specs=[pl.BlockSpec((tm, D_MODEL), lambda i: (i, 0)),
                   pl.BlockSpec((tm, D_MODEL), lambda i: (i, 0))],
        out_shape=[jax.ShapeDtypeStruct((n_rows, D_MODEL), F32),
                   jax.ShapeDtypeStruct((n_rows, D_MODEL), BF16)],
        scratch_shapes=[pltpu.VMEM((tm, D_MODEL), BF16)],
        compiler_params=_cparams(("arbitrary",)),
        name="gated_merge",
    )(ya, yb, atta, attb, sg, gate, xa, xb, mod, mod, mod, norm2_w.reshape(1, D_MODEL), wf, wna, wc, wo)


_FF_TILES = D_FF // TN


def _ffn_up_kernel(h_ref, hp_ref, hn_ref, wa_ref, wg_ref, cw_ref, cb_ref, o_ref, wa_scr, wg_scr, a_scr):
    i = pl.program_id(1)
    tp = h_ref.shape[0]
    rc = tp // N_CHUNKS

    @pl.when(i == 0)
    def _():
        wa_scr[...] = wa_ref[...].astype(BF16)
        wg_scr[...] = wg_ref[...].astype(BF16)

    n_chunks = N_CHUNKS
    for c in range(n_chunks):
        parts = [h_ref[c * rc:(c + 1) * rc, :]]
        lo, hi = HALO + c * rc, HALO + (c + 1) * rc
        if c == 0:
            parts.insert(0, hp_ref[...])
            lo -= HALO
        if c == n_chunks - 1:
            parts.append(hn_ref[...])
            hi += HALO
        lhs = parts[0] if len(parts) == 1 else jnp.concatenate(parts, axis=0)
        a_scr[lo:hi, :] = jnp.dot(lhs, wa_scr[...], preferred_element_type=F32)

    w0, w1, w2, cb = cw_ref[0:1], cw_ref[1:2], cw_ref[2:3], cb_ref[...]
    gate_proj = lambda c: jnp.dot(h_ref[c * rc:(c + 1) * rc, :], wg_scr[...], preferred_element_type=F32)
    gv_nxt = gate_proj(0)
    for c in range(n_chunks):
        r0 = HALO + c * rc
        gv = gv_nxt
        if c + 1 < n_chunks:
            gv_nxt = gate_proj(c + 1)
        row = i * tp + c * rc + lax.broadcasted_iota(jnp.int32, (rc, TN), 0)
        seq_mask = jnp.where(row < N_LAT, SEQ - 1, CTX_LEN - 1)
        pos = row & seq_mask
        prev = jnp.where(pos != 0, a_scr[r0 - 1:r0 - 1 + rc, :], 0.0)
        nxt = jnp.where(pos != seq_mask, a_scr[r0 + 1:r0 + 1 + rc, :], 0.0)
        a = cb + w0 * prev + w1 * a_scr[r0:r0 + rc, :] + w2 * nxt
        o_ref[c * rc:(c + 1) * rc, :] = (a * jax.nn.sigmoid(a) * gv).astype(BF16)


def _ffn_up(h2, w_up_all, layer, conv_w, conv_b, n_rows):
    tp = _proj_tile(n_rows)
    hb = tp // HALO
    last = h2.shape[0] // HALO - 1

    def wspec(off):
        return pl.BlockSpec((pl.Squeezed(), D_MODEL, TN), lambda j, i: (layer, 0, off + j))

    return pl.pallas_call(
        _ffn_up_kernel,
        grid=(_FF_TILES, n_rows // tp),
        in_specs=[
            pl.BlockSpec((tp, D_MODEL), lambda j, i: (i, 0)),
            pl.BlockSpec((HALO, D_MODEL), lambda j, i: (jnp.maximum(i * hb - 1, 0), 0)),
            pl.BlockSpec((HALO, D_MODEL), lambda j, i: (jnp.minimum((i + 1) * hb, last), 0)),
            wspec(0),
            wspec(_FF_TILES),
            pl.BlockSpec((CONV_W, TN), lambda j, i: (0, j)),
            pl.BlockSpec((1, TN), lambda j, i: (0, j)),
        ],
        out_specs=pl.BlockSpec((tp, TN), lambda j, i: (i, j)),
        out_shape=jax.ShapeDtypeStruct((n_rows, D_FF), BF16),
        scratch_shapes=[pltpu.VMEM((D_MODEL, TN), BF16), pltpu.VMEM((D_MODEL, TN), BF16),
                        pltpu.VMEM((tp + 2 * HALO, TN), F32)],
        compiler_params=_cparams(("arbitrary", "arbitrary")),
        name="ffn_up_conv",
    )(h2, h2, h2, w_up_all, w_up_all, conv_w, conv_b.reshape(1, D_FF))


def _ffn_down_kernel(a_ref, w_ref, x_ref, g2_ref, sh_ref, sc_ref, nw_ref, *refs, final):
    tm = a_ref.shape[0]
    r = _mod_row(pl.program_id(0), tm)
    xs_ref = refs[1] if final else refs[0]
    act = a_ref[...]
    ss = jnp.zeros((tm, 1), F32)
    n_c = D_MODEL // TN
    down = lambda c: jnp.dot(act, w_ref[:, c * TN:(c + 1) * TN], preferred_element_type=F32)
    o_nxt = down(0)
    for c in range(n_c):
        sl = slice(c * TN, (c + 1) * TN)
        o = o_nxt
        if c + 1 < n_c:
            o_nxt = down(c + 1)
        xn = x_ref[:, sl] + g2_ref[pl.ds(r, 1), sl] * o
        xs_ref[:, sl] = xn
        ss = ss + jnp.sum(xn * xn, axis=-1, keepdims=True)
    inv = lax.rsqrt(ss * (1.0 / D_MODEL) + EPS)
    if final:
        refs[0][...] = xs_ref[...] * inv * nw_ref[...]
    else:
        wmod = nw_ref[...] * (1.0 + sc_ref[pl.ds(r, 1), :])
        refs[1][...] = (xs_ref[...] * inv * wmod + sh_ref[pl.ds(r, 1), :]).astype(BF16)


def _ffn_down(act, w_down, x_all, mod, mod_next, norm_w, n_rows, final):
    tm = TM_DOWN
    assert n_rows % tm == 0 and N_LAT % tm == 0
    row = lambda i: (i, 0)
    if final:
        out_specs = pl.BlockSpec((tm, D_MODEL), row)
        out_shape = jax.ShapeDtypeStruct((n_rows, D_MODEL), F32)
        scratch = [pltpu.VMEM((tm, D_MODEL), F32)]
    else:
        out_specs = [pl.BlockSpec((tm, D_MODEL), row), pl.BlockSpec((tm, D_MODEL), row)]
        out_shape = [jax.ShapeDtypeStruct((n_rows, D_MODEL), F32),
                     jax.ShapeDtypeStruct((n_rows, D_MODEL), BF16)]
        scratch = []
    return pl.pallas_call(
        functools.partial(_ffn_down_kernel, final=final),
        grid=(n_rows // tm,),
        in_specs=[
            pl.BlockSpec((tm, D_FF), row),
            _resident((D_FF, D_MODEL)),
            pl.BlockSpec((tm, D_MODEL), row),
            pl.BlockSpec((MOD_ROWS, D_MODEL), lambda i: (0, 5)),
            pl.BlockSpec((MOD_ROWS, D_MODEL), lambda i: (0, 0)),
            pl.BlockSpec((MOD_ROWS, D_MODEL), lambda i: (0, 1)),
            pl.BlockSpec((1, D_MODEL), lambda i: (0, 0)),
        ],
        out_specs=out_specs,
        out_shape=out_shape,
        scratch_shapes=scratch,
        compiler_params=pltpu.CompilerParams(dimension_semantics=("arbitrary",),
                                             vmem_limit_bytes=60 * 1024 * 1024),
        name="ffn_down",
    )(act, w_down, x_all, mod, mod_next, mod_next, norm_w.reshape(1, D_MODEL))


def kernel(x, c, ctx, c_ctx, ada_w, ada_b, norm1_w, norm2_w, w_in, na_rpb, gmlp_norm_w, gmlp_ws,
           gmlp_bs, w_f_out, w_na_out, w_c_out, w_o, ffn_up, ffn_conv_w, ffn_conv_b, ffn_down,
           final_norm_w):
    assert x.shape == (BATCH, SEQ, D_MODEL) and ctx.shape == (BATCH, CTX_LEN, D_MODEL)
    cvec = jnp.concatenate(
        [c, c_ctx[None, :], jnp.zeros((MOD_ROWS - BATCH - 1, D_MODEL), F32)], axis=0)
    mod_all = _modulation(cvec, ada_w, ada_b)
    tables = _dft_tables()
    xa = x.reshape(N_LAT, D_MODEL)
    xb = ctx.reshape(N_CTX, D_MODEL)
    h1 = _prenorm(xa, xb, mod_all[0], norm1_w[0])

    for l in range(DEPTH):
        last = l == DEPTH - 1
        n_rows = N_LAT if last else N_TOK
        mod = mod_all[l]
        fs = _projection_f(h1, w_in, l, n_rows, not last)
        qkv = _projection(h1, w_in, l, F_W, 3 * NA_W, 1024, _ident, "in_proj_qkv", N_TOK)
        sg = _projection_gating(h1, w_in, l, gmlp_norm_w[l], gmlp_ws[l], gmlp_bs[l], n_rows)
        gate = _projection(h1, w_in, l, F_W + 3 * NA_W + 2 * C_W, 3 * D_MODEL, 1024, jax.nn.sigmoid,
                           "in_proj_gate", n_rows)
        atts = _attention(qkv, _bias_table(na_rpb[l]), not last)
        ys = _fourier(fs[0], fs[-1], tables, not last)
        x_mid, h2 = _merge(ys[0], ys[-1], atts[0], atts[-1], sg, gate, xa, xb, mod, norm2_w[l],
                           w_f_out[l].astype(BF16),
                           w_na_out[l].astype(BF16), w_c_out[l].astype(BF16), w_o[l].astype(BF16),
                           n_rows)
        act = _ffn_up(h2, ffn_up, l, ffn_conv_w[l], ffn_conv_b[l], n_rows)
        if last:
            out = _ffn_down(act, ffn_down[l].astype(BF16), x_mid, mod, mod, final_norm_w, n_rows, True)
        else:
            xa, h1 = _ffn_down(act, ffn_down[l].astype(BF16), x_mid, mod, mod_all[l + 1],
                               norm1_w[l + 1], n_rows, False)
            xb = xa
    return out.reshape(BATCH, SEQ, D_MODEL)
```

```python
import functools
import math

import numpy as np
import jax
import jax.numpy as jnp
from jax import lax
from jax.experimental import pallas as pl
from jax.experimental.pallas import tpu as pltpu

F32 = jnp.float32
BF16 = jnp.bfloat16

D_MODEL = 2048
BATCH = 4
SEQ = 4096
DEPTH = 2
GRID_W = 64
ROWS = SEQ // GRID_W
CTX_LEN = 256
HEAD_DIM = 128
F_GROUPS = 4
F_W = F_GROUPS * HEAD_DIM
NA_HEADS = 8
NA_W = NA_HEADS * HEAD_DIM
NA_KH = 8
NA_KW = 16
C_GROUPS = 4
C_W = C_GROUPS * HEAD_DIM
CHUNK = 128
D_FF = 5632
CONV_W = 3
EPS = 1e-6
NEG_INF = -1e30
IN_W = F_W + 3 * NA_W + 2 * C_W + 3 * D_MODEL

N_LAT = BATCH * SEQ
N_CTX = BATCH * CTX_LEN
N_TOK = N_LAT + N_CTX
MOD_ROWS = 8

TM = 1024
TN = 512
N_CHUNKS = 8
TM_RES = 256
TM_DOWN = 512
HALO = 16
QROWS = 4
KROWS = 12
ATTN_UNROLL = 4
DFT_UNROLL = 8
PITCH = 72
VMEM_LIMIT = 56 * 1024 * 1024


def _cparams(sem):
    return pltpu.CompilerParams(dimension_semantics=sem, vmem_limit_bytes=VMEM_LIMIT)


def _mod_row(i, tm):
    return jnp.minimum((i * tm) // SEQ, BATCH)


def _norm_mod(x, nw, sh, sc):
    ms = jnp.mean(x * x, axis=-1, keepdims=True)
    return (x * lax.rsqrt(ms + EPS) * nw) * (1.0 + sc) + sh


def _rms(x, nw):
    ms = jnp.mean(x * x, axis=-1, keepdims=True)
    return x * lax.rsqrt(ms + EPS) * nw


def _resident(shape, layer):
    return pl.BlockSpec((pl.Squeezed(),) + shape, lambda *_: (layer,) + (0,) * len(shape),
                        pipeline_mode=pl.Buffered(1))


def _mod_kernel(c_ref, w_ref, b_ref, o_ref):
    c = c_ref[...]
    s = (c * jax.nn.sigmoid(c)).astype(BF16)
    o_ref[0] = jnp.dot(s, w_ref[0].astype(BF16), preferred_element_type=F32) + b_ref[0]


def _modulation(cvec, ada_w, ada_b):
    tn = 1024
    return pl.pallas_call(
        _mod_kernel,
        grid=(DEPTH, 6 * D_MODEL // tn),
        in_specs=[
            pl.BlockSpec((MOD_ROWS, D_MODEL), lambda l, j: (0, 0)),
            pl.BlockSpec((1, D_MODEL, tn), lambda l, j: (l, 0, j)),
            pl.BlockSpec((1, 1, tn), lambda l, j: (l, 0, j)),
        ],
        out_specs=pl.BlockSpec((1, MOD_ROWS, tn), lambda l, j: (l, 0, j)),
        out_shape=jax.ShapeDtypeStruct((DEPTH, MOD_ROWS, 6 * D_MODEL), F32),
        compiler_params=_cparams(("arbitrary", "arbitrary")),
        name="modulation",
    )(cvec, ada_w, ada_b.reshape(DEPTH, 1, 6 * D_MODEL))


def _prenorm_kernel(xa_ref, xb_ref, sh_ref, sc_ref, nw_ref, o_ref, *, n_a_tiles):
    i = pl.program_id(0)
    r = _mod_row(i, TM)
    x = jnp.where(i < n_a_tiles, xa_ref[...], xb_ref[...])
    h = _norm_mod(x, nw_ref[...], sh_ref[pl.ds(r, 1), :], sc_ref[pl.ds(r, 1), :])
    o_ref[...] = h.astype(BF16)


def _prenorm(x2d, ctx2d, mod, norm_w):
    na = N_LAT // TM
    return pl.pallas_call(
        functools.partial(_prenorm_kernel, n_a_tiles=na),
        grid=(N_TOK // TM,),
        in_specs=[
            pl.BlockSpec((TM, D_MODEL), lambda i: (jnp.minimum(i, na - 1), 0)),
            pl.BlockSpec((TM, D_MODEL), lambda i: (jnp.maximum(i - na, 0), 0)),
            pl.BlockSpec((MOD_ROWS, D_MODEL), lambda i: (0, 0)),
            pl.BlockSpec((MOD_ROWS, D_MODEL), lambda i: (0, 1)),
            pl.BlockSpec((1, D_MODEL), lambda i: (0, 0)),
        ],
        out_specs=pl.BlockSpec((TM, D_MODEL), lambda i: (i, 0)),
        out_shape=jax.ShapeDtypeStruct((N_TOK, D_MODEL), BF16),
        compiler_params=_cparams(("arbitrary",)),
        name="prenorm",
    )(x2d, ctx2d, mod, mod, norm_w.reshape(1, D_MODEL))


def _ident(a):
    return a


def _gelu(a):
    return 0.5 * a * (1.0 + lax.erf(a * math.sqrt(0.5)))


def _proj_tile(n_rows):
    tp = n_rows // 8
    assert tp * 8 == n_rows and tp % (N_CHUNKS * HALO) == 0
    return tp


def _proj_kernel(h_ref, w_ref, o_ref, w_scr, *, epilogue):
    @pl.when(pl.program_id(1) == 0)
    def _():
        w_scr[...] = w_ref[...].astype(BF16)

    rc = h_ref.shape[0] // N_CHUNKS
    for c in range(N_CHUNKS):
        rows = slice(c * rc, (c + 1) * rc)
        acc = jnp.dot(h_ref[rows, :], w_scr[...], preferred_element_type=F32)
        o_ref[rows, :] = epilogue(acc).astype(BF16)


def _projection(h, w_all, layer, col0, n_cols, tn, epilogue, name, n_rows):
    tp = _proj_tile(n_rows)
    return pl.pallas_call(
        functools.partial(_proj_kernel, epilogue=epilogue),
        grid=(n_cols // tn, n_rows // tp),
        in_specs=[
            pl.BlockSpec((tp, D_MODEL), lambda j, i: (i, 0)),
            pl.BlockSpec((pl.Squeezed(), pl.Element(D_MODEL), pl.Element(tn)),
                         lambda j, i: (layer, 0, pl.multiple_of(col0 + j * tn, 128))),
        ],
        out_specs=pl.BlockSpec((tp, tn), lambda j, i: (i, j)),
        out_shape=jax.ShapeDtypeStruct((n_rows, n_cols), BF16),
        scratch_shapes=[pltpu.VMEM((D_MODEL, tn), BF16)],
        compiler_params=_cparams(("arbitrary", "arbitrary")),
        name=name,
    )(h, w_all)


def _proj_f_kernel(h_ref, w_ref, *rest, with_tok):
    if with_tok:
        og_ref, ot_ref, w_scr, t_scr = rest
    else:
        og_ref, w_scr, t_scr = rest

    @pl.when(pl.program_id(0) == 0)
    def _():
        w_scr[...] = w_ref[...].astype(BF16)

    n1 = GRID_W
    acc = jnp.dot(h_ref[...], w_scr[...], preferred_element_type=F32)
    if with_tok:
        ot_ref[...] = acc.astype(BF16)
    for a in range(TM // n1):
        for g in range(F_GROUPS):
            t_scr[g, a * PITCH:a * PITCH + n1, :] = acc[a * n1:(a + 1) * n1, g * HEAD_DIM:(g + 1) * HEAD_DIM]
    for b in range(n1):
        for g in range(F_GROUPS):
            col = b * F_W + g * HEAD_DIM
            og_ref[:, col:col + HEAD_DIM] = t_scr[g, pl.ds(b, TM // n1, stride=PITCH), :].astype(BF16)


def _projection_f(h, w_all, layer, n_rows, with_tok):
    n1 = GRID_W
    out_specs = [pl.BlockSpec((TM // n1, n1 * F_W), lambda i: (i, 0))]
    out_shape = [jax.ShapeDtypeStruct((n_rows // n1, n1 * F_W), BF16)]
    if with_tok:
        out_specs.append(pl.BlockSpec((TM, F_W), lambda i: (i, 0)))
        out_shape.append(jax.ShapeDtypeStruct((n_rows, F_W), BF16))
    return pl.pallas_call(
        functools.partial(_proj_f_kernel, with_tok=with_tok),
        grid=(n_rows // TM,),
        in_specs=[
            pl.BlockSpec((TM, D_MODEL), lambda i: (i, 0)),
            pl.BlockSpec((pl.Squeezed(), D_MODEL, F_W), lambda i: (layer, 0, 0)),
        ],
        out_specs=out_specs,
        out_shape=out_shape,
        scratch_shapes=[pltpu.VMEM((D_MODEL, F_W), BF16),
                        pltpu.VMEM((F_GROUPS, (TM // n1) * PITCH, HEAD_DIM), F32)],
        compiler_params=_cparams(("arbitrary",)),
        name="in_proj_f",
    )(h, w_all)


_N_QBLK = ROWS // QROWS
_BIAS_VARIANTS = (0, 1, _N_QBLK - 1)


def _key_row0(nb):
    return min(max(QROWS * nb - NA_KH // 2, 0), ROWS - KROWS)


def _bias_kernel(rpb_ref, o_ref, pair_scr):
    h = pl.program_id(0)
    qc = lax.broadcasted_iota(jnp.int32, (GRID_W, 2 * GRID_W), 0)
    lane = lax.broadcasted_iota(jnp.int32, (GRID_W, 2 * GRID_W), 1)
    kc = lane & (GRID_W - 1)
    low = lane < GRID_W
    dc = jnp.clip(kc - qc + NA_KW - 1, 0, 2 * NA_KW - 2)
    c_start = jnp.clip(qc - NA_KW // 2, 0, GRID_W - NA_KW)
    col_in = (kc >= c_start) & (kc < c_start + NA_KW)
    n_dr = 2 * NA_KH - 1
    for dr0 in range(-1, n_dr):
        acc = jnp.zeros((GRID_W, 2 * GRID_W), F32)
        for d in range(2 * NA_KW - 1):
            lo = rpb_ref[h, dr0, d] if 0 <= dr0 < n_dr else 0.0
            hi = rpb_ref[h, dr0 + 1, d] if 0 <= dr0 + 1 < n_dr else 0.0
            acc = jnp.where(dc == d, jnp.where(low, lo, hi), acc)
        pair_scr[dr0 + 1] = acc
    for vi, nb in enumerate(_BIAS_VARIANTS):
        r0 = _key_row0(nb)
        for i in range(QROWS):
            r = QROWS * nb + i
            r_start = min(max(r - NA_KH // 2, 0), ROWS - NA_KH)
            for tp in range(KROWS // 2):
                kr = r0 + 2 * tp
                ok_lo = r_start <= kr < r_start + NA_KH
                ok_hi = r_start <= kr + 1 < r_start + NA_KH
                dr0 = kr - r + NA_KH - 1
                if not (ok_lo or ok_hi):
                    tile = jnp.full((GRID_W, 2 * GRID_W), NEG_INF, F32)
                else:
                    valid = col_in
                    if not ok_lo:
                        valid = valid & jnp.logical_not(low)
                    if not ok_hi:
                        valid = valid & low
                    tile = jnp.where(valid, pair_scr[dr0 + 1], NEG_INF)
                o_ref[vi, 0, i * GRID_W:(i + 1) * GRID_W, tp * 2 * GRID_W:(tp + 1) * 2 * GRID_W] = tile


def _bias_table(rpb):
    nv = len(_BIAS_VARIANTS)
    return pl.pallas_call(
        _bias_kernel,
        grid=(NA_HEADS,),
        in_specs=[pl.BlockSpec(memory_space=pltpu.SMEM)],
        out_specs=pl.BlockSpec((nv, 1, QROWS * GRID_W, KROWS * GRID_W), lambda h: (0, h, 0, 0)),
        out_shape=jax.ShapeDtypeStruct((nv, NA_HEADS, QROWS * GRID_W, KROWS * GRID_W), F32),
        scratch_shapes=[pltpu.VMEM((2 * NA_KH, GRID_W, 2 * GRID_W), F32)],
        compiler_params=_cparams(("arbitrary",)),
        name="attn_bias_table",
    )(rpb)


_QK_DIMS = (((1,), (1,)), ((), ()))
_SCALE = HEAD_DIM ** -0.5
_QB = QROWS * GRID_W


def _attn_kernel(q_ref, k_ref, v_ref, kc_ref, vc_ref, bias_ref, *rest, with_ctx):
    kc = kc_ref[...]
    vc = vc_ref[...]
    if with_ctx:
        qc_ref, o_ref, oc_ref, sw_scr, sc_scr = rest
    else:
        o_ref, sw_scr, sc_scr = rest

    def key_start(nb):
        r0 = jnp.clip(QROWS * nb - NA_KH // 2, 0, ROWS - KROWS)
        return pl.multiple_of(r0 * GRID_W, GRID_W)

    def scores(nb, slot):
        qrow = pl.multiple_of(nb * _QB, _QB)
        variant = jnp.where(nb == 0, 0, jnp.where(nb == _N_QBLK - 1, 2, 1))
        q = (q_ref[pl.ds(qrow, _QB), :].astype(F32) * _SCALE).astype(BF16)
        kw = k_ref[pl.ds(key_start(nb), KROWS * GRID_W), :]
        sw_scr[slot] = lax.dot_general(q, kw, _QK_DIMS, preferred_element_type=F32) + bias_ref[variant, 0]
        sc_scr[slot] = lax.dot_general(q, kc, _QK_DIMS, preferred_element_type=F32)

    def finish(nb, slot):
        qrow = pl.multiple_of(nb * _QB, _QB)
        vw = v_ref[pl.ds(key_start(nb), KROWS * GRID_W), :]
        s_w = sw_scr[slot]
        s_c = sc_scr[slot]
        m = jnp.maximum(jnp.max(s_w, axis=-1, keepdims=True), jnp.max(s_c, axis=-1, keepdims=True))
        p_w = jnp.exp(s_w - m)
        p_c = jnp.exp(s_c - m)
        denom = jnp.sum(p_w, axis=-1, keepdims=True) + jnp.sum(p_c, axis=-1, keepdims=True)
        o = (jnp.dot(p_w.astype(BF16), vw, preferred_element_type=F32)
             + jnp.dot(p_c.astype(BF16), vc, preferred_element_type=F32))
        o_ref[pl.ds(qrow, _QB), :] = (o / denom).astype(BF16)

    if with_ctx:
        q = (qc_ref[...].astype(F32) * _SCALE).astype(BF16)
        s = lax.dot_general(q, kc, _QK_DIMS, preferred_element_type=F32)
        p = jnp.exp(s - jnp.max(s, axis=-1, keepdims=True))
        o = jnp.dot(p.astype(BF16), vc, preferred_element_type=F32)
        oc_ref[...] = (o / jnp.sum(p, axis=-1, keepdims=True)).astype(BF16)

    scores(0, 0)

    def pair(t, carry):
        nb = 2 * t
        scores(nb + 1, 1)
        finish(nb, 0)
        scores(jnp.minimum(nb + 2, _N_QBLK - 1), 0)
        finish(nb + 1, 1)
        return carry

    lax.fori_loop(0, _N_QBLK // 2, pair, 0, unroll=ATTN_UNROLL)


def _attention(qkv, bias, with_ctx):
    ctx0 = N_LAT // CTX_LEN
    nv = len(_BIAS_VARIANTS)
    in_specs = [
        pl.BlockSpec((SEQ, HEAD_DIM), lambda b, h: (b, h)),
        pl.BlockSpec((SEQ, HEAD_DIM), lambda b, h: (b, NA_HEADS + h)),
        pl.BlockSpec((SEQ, HEAD_DIM), lambda b, h: (b, 2 * NA_HEADS + h)),
        pl.BlockSpec((CTX_LEN, HEAD_DIM), lambda b, h: (ctx0 + b, NA_HEADS + h)),
        pl.BlockSpec((CTX_LEN, HEAD_DIM), lambda b, h: (ctx0 + b, 2 * NA_HEADS + h)),
        pl.BlockSpec((nv, 1, _QB, KROWS * GRID_W), lambda b, h: (0, h, 0, 0)),
    ]
    out_specs = [pl.BlockSpec((SEQ, HEAD_DIM), lambda b, h: (b, h))]
    out_shape = [jax.ShapeDtypeStruct((N_LAT, NA_W), BF16)]
    operands = [qkv, qkv, qkv, qkv, qkv, bias]
    if with_ctx:
        in_specs.append(pl.BlockSpec((CTX_LEN, HEAD_DIM), lambda b, h: (ctx0 + b, h)))
        out_specs.append(pl.BlockSpec((CTX_LEN, HEAD_DIM), lambda b, h: (b, h)))
        out_shape.append(jax.ShapeDtypeStruct((N_CTX, NA_W), BF16))
        operands.append(qkv)
    return pl.pallas_call(
        functools.partial(_attn_kernel, with_ctx=with_ctx),
        grid=(BATCH, NA_HEADS),
        in_specs=in_specs,
        out_specs=out_specs,
        out_shape=out_shape,
        scratch_shapes=[pltpu.VMEM((2, _QB, KROWS * GRID_W), F32), pltpu.VMEM((2, _QB, CTX_LEN), F32)],
        compiler_params=_cparams(("arbitrary", "arbitrary")),
        name="neighborhood_attention",
    )(*operands)


def _dft_tables():
    n1 = GRID_W
    idx = np.arange(n1)
    ang1 = 2 * np.pi * np.outer(idx, idx) / n1
    w1 = np.concatenate([np.cos(ang1), -np.sin(ang1)], axis=0)
    q = idx[:, None, None]
    p = idx[None, :, None]
    b = idx[None, None, :]
    ang2 = 2 * np.pi * b * (n1 * p + q) / SEQ
    wc, ws = np.cos(ang2), np.sin(ang2)
    m2 = np.concatenate([np.concatenate([wc, ws], axis=2),
                         np.concatenate([-ws, wc], axis=2)], axis=1)
    d = np.arange(HEAD_DIM)
    ang3 = 2 * np.pi * np.outer(d, d) / HEAD_DIM
    cs = np.concatenate([np.cos(ang3), np.sin(ang3)], axis=0) / math.sqrt(SEQ * HEAD_DIM)
    n = np.arange(CTX_LEN)
    angc = 2 * np.pi * np.outer(n, n) / CTX_LEN
    t1 = np.concatenate([np.cos(angc), np.sin(angc)], axis=0)
    cs_ctx = np.concatenate([np.cos(ang3), -np.sin(ang3)], axis=0) / math.sqrt(CTX_LEN * HEAD_DIM)
    as_bf16 = lambda a: jnp.asarray(a, dtype=F32).astype(BF16)
    return as_bf16(w1), as_bf16(m2), as_bf16(cs), as_bf16(t1), as_bf16(cs_ctx)


def _fourier_kernel(x_ref, w1_ref, m2_ref, cs_ref, *rest, with_ctx):
    if with_ctx:
        xc_ref, t1_ref, csc_ref, o_ref, oc_ref, re_scr, im_scr, y_scr = rest
        t = jnp.dot(t1_ref[...], xc_ref[...], preferred_element_type=F32)
        csc = csc_ref[...]
        for g in range(F_GROUPS):
            sl = slice(g * HEAD_DIM, (g + 1) * HEAD_DIM)
            lhs = jnp.concatenate([t[:CTX_LEN, sl], t[CTX_LEN:, sl]], axis=1).astype(BF16)
            oc_ref[:, sl] = jnp.dot(lhs, csc, preferred_element_type=F32).astype(BF16)
    else:
        o_ref, re_scr, im_scr, y_scr = rest
    n1 = GRID_W
    w1 = w1_ref[...]
    for b in range(n1):
        a = jnp.dot(w1, x_ref[:, b * F_W:(b + 1) * F_W], preferred_element_type=F32)
        for g in range(F_GROUPS):
            sl = slice(g * HEAD_DIM, (g + 1) * HEAD_DIM)
            re_scr[g, pl.ds(b, n1, stride=PITCH), :] = a[:n1, sl]
            im_scr[g, pl.ds(b, n1, stride=PITCH), :] = a[n1:, sl]

    cs = cs_ref[...]

    def column_dft(q):
        row = pl.multiple_of(q * PITCH, 8)
        a = jnp.concatenate(
            [jnp.concatenate([re_scr[g, pl.ds(row, n1), :] for g in range(F_GROUPS)], axis=1),
             jnp.concatenate([im_scr[g, pl.ds(row, n1), :] for g in range(F_GROUPS)], axis=1)], axis=0)
        return jnp.dot(m2_ref[q], a.astype(BF16), preferred_element_type=F32)

    def channel_dft(q, z):
        lhs = jnp.concatenate(
            [jnp.concatenate([z[:n1, g * HEAD_DIM:(g + 1) * HEAD_DIM],
                              z[n1:, g * HEAD_DIM:(g + 1) * HEAD_DIM]], axis=1)
             for g in range(F_GROUPS)], axis=0).astype(BF16)
        y = jnp.dot(lhs, cs, preferred_element_type=F32)
        for g in range(F_GROUPS):
            y_scr[g, pl.ds(q, n1, stride=PITCH), :] = y[g * n1:(g + 1) * n1]

    def per_group(t, carry):
        qs = [t * DFT_UNROLL + k for k in range(DFT_UNROLL)]
        zs = [column_dft(q) for q in qs]
        for q, z in zip(qs, zs):
            channel_dft(q, z)
        return carry

    lax.fori_loop(0, n1 // DFT_UNROLL, per_group, 0)
    for p in range(n1):
        for g in range(F_GROUPS):
            o_ref[p * n1:(p + 1) * n1, g * HEAD_DIM:(g + 1) * HEAD_DIM] = (
                y_scr[g, p * PITCH:p * PITCH + n1, :].astype(BF16))


def _fourier(fx, f_ctx, tables, with_ctx):
    w1, m2, cs, t1, cs_ctx = tables
    n1 = GRID_W
    const = lambda shape: pl.BlockSpec(shape, lambda b: (0,) * len(shape))
    in_specs = [pl.BlockSpec((n1, n1 * F_W), lambda b: (b, 0)), const((2 * n1, n1)),
                const((n1, 2 * n1, 2 * n1)), const((2 * HEAD_DIM, HEAD_DIM))]
    out_specs = [pl.BlockSpec((SEQ, F_W), lambda b: (b, 0))]
    out_shape = [jax.ShapeDtypeStruct((N_LAT, F_W), BF16)]
    operands = [fx, w1, m2, cs]
    if with_ctx:
        ctx0 = N_LAT // CTX_LEN
        in_specs += [pl.BlockSpec((CTX_LEN, F_W), lambda b: (ctx0 + b, 0)), const((2 * CTX_LEN, CTX_LEN)),
                     const((2 * HEAD_DIM, HEAD_DIM))]
        out_specs.append(pl.BlockSpec((CTX_LEN, F_W), lambda b: (b, 0)))
        out_shape.append(jax.ShapeDtypeStruct((N_CTX, F_W), BF16))
        operands += [f_ctx, t1, cs_ctx]
    return pl.pallas_call(
        functools.partial(_fourier_kernel, with_ctx=with_ctx),
        grid=(BATCH,),
        in_specs=in_specs,
        out_specs=out_specs,
        out_shape=out_shape,
        scratch_shapes=[pltpu.VMEM((F_GROUPS, n1 * PITCH, HEAD_DIM), F32)] * 3,
        compiler_params=_cparams(("arbitrary",)),
        name="fourier_mix",
    )(*operands)


_Z_ROWS = 2 * CHUNK


def _proj_gating_kernel(h_ref, w_ref, nw_ref, ws_ref, bs_ref, o_ref, w_scr):
    @pl.when(pl.program_id(0) == 0)
    def _():
        w_scr[...] = w_ref[...].astype(BF16)

    nw = nw_ref[...]

    def project(r):
        rows = slice(r * _Z_ROWS, (r + 1) * _Z_ROWS)
        return _gelu(jnp.dot(h_ref[rows, :], w_scr[...], preferred_element_type=F32))

    n_r = TM // _Z_ROWS
    z_next = project(0)
    for r in range(n_r):
        z = z_next
        if r + 1 < n_r:
            z_next = project(r + 1)
        for g in range(C_GROUPS):
            sl = slice(g * HEAD_DIM, (g + 1) * HEAD_DIM)
            v = z[:, C_W + g * HEAD_DIM:C_W + (g + 1) * HEAD_DIM]
            mu = jnp.mean(v, axis=-1, keepdims=True)
            vc = v - mu
            var = jnp.mean(vc * vc, axis=-1, keepdims=True)
            vn = (vc * lax.rsqrt(var + EPS) * nw[:, sl]).astype(BF16)
            n_ch = _Z_ROWS // CHUNK
            vn2 = jnp.concatenate([vn[c * CHUNK:(c + 1) * CHUNK] for c in range(n_ch)], axis=1)
            sp2 = jnp.dot(ws_ref[g], vn2, preferred_element_type=F32)
            for c in range(n_ch):
                cr = slice(c * CHUNK, (c + 1) * CHUNK)
                sp = sp2[:, c * HEAD_DIM:(c + 1) * HEAD_DIM] + bs_ref[g]
                o_ref[r * _Z_ROWS + c * CHUNK:r * _Z_ROWS + (c + 1) * CHUNK, sl] = (
                    z[cr, sl] * sp).astype(BF16)


def _projection_gating(h, w_all, layer, norm_w, ws, bs, n_rows):
    bs_exp = jnp.broadcast_to(bs[:, :, None], (C_GROUPS, CHUNK, HEAD_DIM))
    col0 = F_W + 3 * NA_W
    return pl.pallas_call(
        _proj_gating_kernel,
        grid=(n_rows // TM,),
        in_specs=[
            pl.BlockSpec((TM, D_MODEL), lambda i: (i, 0)),
            pl.BlockSpec((pl.Squeezed(), pl.Element(D_MODEL), pl.Element(2 * C_W)),
                         lambda i: (layer, 0, col0)),
            pl.BlockSpec((1, C_W), lambda i: (0, 0)),
            pl.BlockSpec((C_GROUPS, CHUNK, CHUNK), lambda i: (0, 0, 0)),
            pl.BlockSpec((C_GROUPS, CHUNK, HEAD_DIM), lambda i: (0, 0, 0)),
        ],
        out_specs=pl.BlockSpec((TM, C_W), lambda i: (i, 0)),
        out_shape=jax.ShapeDtypeStruct((n_rows, C_W), BF16),
        scratch_shapes=[pltpu.VMEM((D_MODEL, 2 * C_W), BF16)],
        compiler_params=_cparams(("arbitrary",)),
        name="in_proj_gmlp",
    )(h, w_all, norm_w.reshape(1, C_W), ws.astype(BF16), bs_exp)


def _merge_kernel(ya_ref, yb_ref, atta_ref, attb_ref, sg_ref, gate_ref, xa_ref, xb_ref, g1_ref, sh_ref,
                  sc_ref, nw_ref, wf_ref, wna_ref, wc_ref, wo_ref, o_ref, h_ref, m_scr, *, n_a_tiles):
    i = pl.program_id(0)
    r = _mod_row(i, TM_RES)
    from_a = i < n_a_tiles
    y = jnp.where(from_a, ya_ref[...], yb_ref[...])
    att = jnp.where(from_a, atta_ref[...], attb_ref[...])
    a_f = jnp.dot(y, wf_ref[...], preferred_element_type=F32)
    a_na = jnp.dot(att, wna_ref[...], preferred_element_type=F32)
    a_c = jnp.dot(sg_ref[...], wc_ref[...], preferred_element_type=F32)
    g_f = gate_ref[:, :D_MODEL].astype(F32)
    g_na = gate_ref[:, D_MODEL:2 * D_MODEL].astype(F32)
    g_c = gate_ref[:, 2 * D_MODEL:].astype(F32)
    m_scr[...] = (g_f * a_f + g_na * a_na + g_c * a_c).astype(BF16)
    o = jnp.dot(m_scr[...], wo_ref[...], preferred_element_type=F32)
    x = jnp.where(from_a, xa_ref[...], xb_ref[...])
    xn = x + g1_ref[pl.ds(r, 1), :] * o
    o_ref[...] = xn
    ms = jnp.mean(xn * xn, axis=-1, keepdims=True)
    wmod = nw_ref[...] * (1.0 + sc_ref[pl.ds(r, 1), :])
    h_ref[...] = (xn * lax.rsqrt(ms + EPS) * wmod + sh_ref[pl.ds(r, 1), :]).astype(BF16)


def _merge(ya, yb, atta, attb, sg, gate, xa, xb, mod, norm2_w, wf, wna, wc, wo, layer, n_rows):
    tm = TM_RES
    na = N_LAT // tm
    nbt = max((n_rows - N_LAT) // tm, 1)
    a_spec = lambda w: pl.BlockSpec((tm, w), lambda i: (jnp.minimum(i, na - 1), 0))
    b_spec = lambda w: pl.BlockSpec((tm, w), lambda i: (jnp.clip(i - na, 0, nbt - 1), 0))
    return pl.pallas_call(
        functools.partial(_merge_kernel, n_a_tiles=na),
        grid=(n_rows // tm,),
        in_specs=[
            a_spec(F_W), b_spec(F_W), a_spec(NA_W), b_spec(NA_W),
            pl.BlockSpec((tm, C_W), lambda i: (i, 0)),
            pl.BlockSpec((tm, 3 * D_MODEL), lambda i: (i, 0)),
            a_spec(D_MODEL), b_spec(D_MODEL),
            pl.BlockSpec((MOD_ROWS, D_MODEL), lambda i: (0, 2)),
            pl.BlockSpec((MOD_ROWS, D_MODEL), lambda i: (0, 3)),
            pl.BlockSpec((MOD_ROWS, D_MODEL), lambda i: (0, 4)),
            pl.BlockSpec((1, D_MODEL), lambda i: (0, 0)),
            _resident((F_W, D_MODEL), layer),
            _resident((NA_W, D_MODEL), layer),
            _resident((C_W, D_MODEL), layer),
            _resident((D_MODEL, D_MODEL), layer),
        ],
        out_specs=[pl.BlockSpec((tm, D_MODEL), lambda i: (i, 0)),
                   pl.BlockSpec((tm, D_MODEL), lambda i: (i, 0))],
        out_shape=[jax.ShapeDtypeStruct((n_rows, D_MODEL), F32),
                   jax.ShapeDtypeStruct((n_rows, D_MODEL), BF16)],
        scratch_shapes=[pltpu.VMEM((tm, D_MODEL), BF16)],
        compiler_params=_cparams(("arbitrary",)),
        name="gated_merge",
    )(ya, yb, atta, attb, sg, gate, xa, xb, mod, mod, mod, norm2_w.reshape(1, D_MODEL), wf, wna, wc, wo)


_FF_TILES = D_FF // TN


def _ffn_up_kernel(h_ref, hp_ref, hn_ref, wa_ref, wg_ref, cw_ref, cb_ref, o_ref, wa_scr, wg_scr, a_scr):
    i = pl.program_id(1)
    tp = h_ref.shape[0]
    rc = tp // N_CHUNKS

    @pl.when(i == 0)
    def _():
        wa_scr[...] = wa_ref[...].astype(BF16)
        wg_scr[...] = wg_ref[...].astype(BF16)

    n_chunks = N_CHUNKS
    for c in range(n_chunks):
        parts = [h_ref[c * rc:(c + 1) * rc, :]]
        lo, hi = HALO + c * rc, HALO + (c + 1) * rc
        if c == 0:
            parts.insert(0, hp_ref[...])
            lo -= HALO
        if c == n_chunks - 1:
            parts.append(hn_ref[...])
            hi += HALO
        lhs = parts[0] if len(parts) == 1 else jnp.concatenate(parts, axis=0)
        a_scr[lo:hi, :] = jnp.dot(lhs, wa_scr[...], preferred_element_type=F32)

    w0, w1, w2, cb = cw_ref[0:1], cw_ref[1:2], cw_ref[2:3], cb_ref[...]
    gate_proj = lambda c: jnp.dot(h_ref[c * rc:(c + 1) * rc, :], wg_scr[...], preferred_element_type=F32)
    gv_nxt = gate_proj(0)
    for c in range(n_chunks):
        r0 = HALO + c * rc
        gv = gv_nxt
        if c + 1 < n_chunks:
            gv_nxt = gate_proj(c + 1)
        row = i * tp + c * rc + lax.broadcasted_iota(jnp.int32, (rc, TN), 0)
        seq_mask = jnp.where(row < N_LAT, SEQ - 1, CTX_LEN - 1)
        pos = row & seq_mask
        prev = jnp.where(pos != 0, a_scr[r0 - 1:r0 - 1 + rc, :], 0.0)
        nxt = jnp.where(pos != seq_mask, a_scr[r0 + 1:r0 + 1 + rc, :], 0.0)
        a = cb + w0 * prev + w1 * a_scr[r0:r0 + rc, :] + w2 * nxt
        o_ref[c * rc:(c + 1) * rc, :] = (a * jax.nn.sigmoid(a) * gv).astype(BF16)


def _ffn_up(h2, w_up_all, layer, conv_w, conv_b, n_rows):
    tp = _proj_tile(n_rows)
    hb = tp // HALO
    last = h2.shape[0] // HALO - 1

    def wspec(off):
        return pl.BlockSpec((pl.Squeezed(), D_MODEL, TN), lambda j, i: (layer, 0, off + j))

    return pl.pallas_call(
        _ffn_up_kernel,
        grid=(_FF_TILES, n_rows // tp),
        in_specs=[
            pl.BlockSpec((tp, D_MODEL), lambda j, i: (i, 0)),
            pl.BlockSpec((HALO, D_MODEL), lambda j, i: (jnp.maximum(i * hb - 1, 0), 0)),
            pl.BlockSpec((HALO, D_MODEL), lambda j, i: (jnp.minimum((i + 1) * hb, last), 0)),
            wspec(0),
            wspec(_FF_TILES),
            pl.BlockSpec((CONV_W, TN), lambda j, i: (0, j)),
            pl.BlockSpec((1, TN), lambda j, i: (0, j)),
        ],
        out_specs=pl.BlockSpec((tp, TN), lambda j, i: (i, j)),
        out_shape=jax.ShapeDtypeStruct((n_rows, D_FF), BF16),
        scratch_shapes=[pltpu.VMEM((D_MODEL, TN), BF16), pltpu.VMEM((D_MODEL, TN), BF16),
                        pltpu.VMEM((tp + 2 * HALO, TN), F32)],
        compiler_params=_cparams(("arbitrary", "arbitrary")),
        name="ffn_up_conv",
    )(h2, h2, h2, w_up_all, w_up_all, conv_w, conv_b.reshape(1, D_FF))


def _ffn_down_kernel(a_ref, w_ref, x_ref, g2_ref, sh_ref, sc_ref, nw_ref, *refs, final):
    tm = a_ref.shape[0]
    r = _mod_row(pl.program_id(0), tm)
    xs_ref = refs[1] if final else refs[0]
    act = a_ref[...]
    ss = jnp.zeros((tm, 1), F32)
    n_c = D_MODEL // TN
    down = lambda c: jnp.dot(act, w_ref[:, c * TN:(c + 1) * TN], preferred_element_type=F32)
    o_nxt = down(0)
    for c in range(n_c):
        sl = slice(c * TN, (c + 1) * TN)
        o = o_nxt
        if c + 1 < n_c:
            o_nxt = down(c + 1)
        xn = x_ref[:, sl] + g2_ref[pl.ds(r, 1), sl] * o
        xs_ref[:, sl] = xn
        ss = ss + jnp.sum(xn * xn, axis=-1, keepdims=True)
    inv = lax.rsqrt(ss * (1.0 / D_MODEL) + EPS)
    if final:
        refs[0][...] = xs_ref[...] * inv * nw_ref[...]
    else:
        wmod = nw_ref[...] * (1.0 + sc_ref[pl.ds(r, 1), :])
        refs[1][...] = (xs_ref[...] * inv * wmod + sh_ref[pl.ds(r, 1), :]).astype(BF16)


def _ffn_down(act, w_down, layer, x_all, mod, mod_next, norm_w, n_rows, final):
    tm = TM_DOWN
    assert n_rows % tm == 0 and N_LAT % tm == 0
    row = lambda i: (i, 0)
    if final:
        out_specs = pl.BlockSpec((tm, D_MODEL), row)
        out_shape = jax.ShapeDtypeStruct((n_rows, D_MODEL), F32)
        scratch = [pltpu.VMEM((tm, D_MODEL), F32)]
    else:
        out_specs = [pl.BlockSpec((tm, D_MODEL), row), pl.BlockSpec((tm, D_MODEL), row)]
        out_shape = [jax.ShapeDtypeStruct((n_rows, D_MODEL), F32),
                     jax.ShapeDtypeStruct((n_rows, D_MODEL), BF16)]
        scratch = []
    return pl.pallas_call(
        functools.partial(_ffn_down_kernel, final=final),
        grid=(n_rows // tm,),
        in_specs=[
            pl.BlockSpec((tm, D_FF), row),
            _resident((D_FF, D_MODEL), layer),
            pl.BlockSpec((tm, D_MODEL), row),
            pl.BlockSpec((MOD_ROWS, D_MODEL), lambda i: (0, 5)),
            pl.BlockSpec((MOD_ROWS, D_MODEL), lambda i: (0, 0)),
            pl.BlockSpec((MOD_ROWS, D_MODEL), lambda i: (0, 1)),
            pl.BlockSpec((1, D_MODEL), lambda i: (0, 0)),
        ],
        out_specs=out_specs,
        out_shape=out_shape,
        scratch_shapes=scratch,
        compiler_params=pltpu.CompilerParams(dimension_semantics=("arbitrary",),
                                             vmem_limit_bytes=60 * 1024 * 1024),
        name="ffn_down",
    )(act, w_down, x_all, mod, mod_next, mod_next, norm_w.reshape(1, D_MODEL))


def kernel(x, c, ctx, c_ctx, ada_w, ada_b, norm1_w, norm2_w, w_in, na_rpb, gmlp_norm_w, gmlp_ws,
           gmlp_bs, w_f_out, w_na_out, w_c_out, w_o, ffn_up, ffn_conv_w, ffn_conv_b, ffn_down,
           final_norm_w):
    assert x.shape == (BATCH, SEQ, D_MODEL) and ctx.shape == (BATCH, CTX_LEN, D_MODEL)
    cvec = jnp.concatenate(
        [c, c_ctx[None, :], jnp.zeros((MOD_ROWS - BATCH - 1, D_MODEL), F32)], axis=0)
    mod_all = _modulation(cvec, ada_w, ada_b)
    tables = _dft_tables()
    merge_w = tuple(w.astype(BF16) for w in (w_f_out, w_na_out, w_c_out, w_o))
    w_down = ffn_down.astype(BF16)
    xa = x.reshape(N_LAT, D_MODEL)
    xb = ctx.reshape(N_CTX, D_MODEL)
    h1 = _prenorm(xa, xb, mod_all[0], norm1_w[0])

    for l in range(DEPTH):
        last = l == DEPTH - 1
        n_rows = N_LAT if last else N_TOK
        mod = mod_all[l]
        fs = _projection_f(h1, w_in, l, n_rows, not last)
        qkv = _projection(h1, w_in, l, F_W, 3 * NA_W, 1024, _ident, "in_proj_qkv", N_TOK)
        sg = _projection_gating(h1, w_in, l, gmlp_norm_w[l], gmlp_ws[l], gmlp_bs[l], n_rows)
        gate = _projection(h1, w_in, l, F_W + 3 * NA_W + 2 * C_W, 3 * D_MODEL, 1024, jax.nn.sigmoid,
                           "in_proj_gate", n_rows)
        atts = _attention(qkv, _bias_table(na_rpb[l]), not last)
        ys = _fourier(fs[0], fs[-1], tables, not last)
        x_mid, h2 = _merge(ys[0], ys[-1], atts[0], atts[-1], sg, gate, xa, xb, mod, norm2_w[l],
                           *merge_w, l, n_rows)
        act = _ffn_up(h2, ffn_up, l, ffn_conv_w[l], ffn_conv_b[l], n_rows)
        if last:
            out = _ffn_down(act, w_down, l, x_mid, mod, mod, final_norm_w, n_rows, True)
        else:
            xa, h1 = _ffn_down(act, w_down, l, x_mid, mod, mod_all[l + 1], norm1_w[l + 1], n_rows, False)
            xb = xa
    return out.reshape(BATCH, SEQ, D_MODEL)
```

```python
import functools
import math

import numpy as np
import jax
import jax.numpy as jnp
from jax import lax
from jax.experimental import pallas as pl
from jax.experimental.pallas import tpu as pltpu

F32 = jnp.float32
BF16 = jnp.bfloat16

D_MODEL = 2048
BATCH = 4
SEQ = 4096
DEPTH = 2
GRID_W = 64
ROWS = SEQ // GRID_W
CTX_LEN = 256
HEAD_DIM = 128
F_GROUPS = 4
F_W = F_GROUPS * HEAD_DIM
NA_HEADS = 8
NA_W = NA_HEADS * HEAD_DIM
NA_KH = 8
NA_KW = 16
C_GROUPS = 4
C_W = C_GROUPS * HEAD_DIM
CHUNK = 128
D_FF = 5632
CONV_W = 3
EPS = 1e-6
NEG_INF = -1e30
IN_W = F_W + 3 * NA_W + 2 * C_W + 3 * D_MODEL

N_LAT = BATCH * SEQ
N_CTX = BATCH * CTX_LEN
N_TOK = N_LAT + N_CTX
MOD_ROWS = 8

TM = 1024
TN = 512
N_CHUNKS = 8
TM_RES = 256
TM_DOWN = 512
HALO = 16
QROWS = 4
KROWS = 12
ATTN_UNROLL = 4
DFT_UNROLL = 8
PITCH = 72
VMEM_LIMIT = 56 * 1024 * 1024


def _cparams(sem):
    return pltpu.CompilerParams(dimension_semantics=sem, vmem_limit_bytes=VMEM_LIMIT)


def _mod_row(i, tm):
    return jnp.minimum((i * tm) // SEQ, BATCH)


def _norm_mod(x, nw, sh, sc):
    ms = jnp.mean(x * x, axis=-1, keepdims=True)
    return (x * lax.rsqrt(ms + EPS) * nw) * (1.0 + sc) + sh


def _rms(x, nw):
    ms = jnp.mean(x * x, axis=-1, keepdims=True)
    return x * lax.rsqrt(ms + EPS) * nw


def _resident(shape, layer):
    return pl.BlockSpec((pl.Squeezed(),) + shape, lambda *_: (layer,) + (0,) * len(shape),
                        pipeline_mode=pl.Buffered(1))


def _mod_kernel(c_ref, w_ref, b_ref, o_ref):
    c = c_ref[...]
    s = (c * jax.nn.sigmoid(c)).astype(BF16)
    o_ref[0] = jnp.dot(s, w_ref[0].astype(BF16), preferred_element_type=F32) + b_ref[0]


def _modulation(cvec, ada_w, ada_b):
    tn = 1024
    return pl.pallas_call(
        _mod_kernel,
        grid=(DEPTH, 6 * D_MODEL // tn),
        in_specs=[
            pl.BlockSpec((MOD_ROWS, D_MODEL), lambda l, j: (0, 0)),
            pl.BlockSpec((1, D_MODEL, tn), lambda l, j: (l, 0, j)),
            pl.BlockSpec((1, 1, tn), lambda l, j: (l, 0, j)),
        ],
        out_specs=pl.BlockSpec((1, MOD_ROWS, tn), lambda l, j: (l, 0, j)),
        out_shape=jax.ShapeDtypeStruct((DEPTH, MOD_ROWS, 6 * D_MODEL), F32),
        compiler_params=_cparams(("arbitrary", "arbitrary")),
        name="modulation",
    )(cvec, ada_w, ada_b.reshape(DEPTH, 1, 6 * D_MODEL))


def _prenorm_kernel(xa_ref, xb_ref, sh_ref, sc_ref, nw_ref, o_ref, *, n_a_tiles):
    i = pl.program_id(0)
    r = _mod_row(i, TM)
    x = jnp.where(i < n_a_tiles, xa_ref[...], xb_ref[...])
    h = _norm_mod(x, nw_ref[...], sh_ref[pl.ds(r, 1), :], sc_ref[pl.ds(r, 1), :])
    o_ref[...] = h.astype(BF16)


def _prenorm(x2d, ctx2d, mod, norm_w):
    na = N_LAT // TM
    return pl.pallas_call(
        functools.partial(_prenorm_kernel, n_a_tiles=na),
        grid=(N_TOK // TM,),
        in_specs=[
            pl.BlockSpec((TM, D_MODEL), lambda i: (jnp.minimum(i, na - 1), 0)),
            pl.BlockSpec((TM, D_MODEL), lambda i: (jnp.maximum(i - na, 0), 0)),
            pl.BlockSpec((MOD_ROWS, D_MODEL), lambda i: (0, 0)),
            pl.BlockSpec((MOD_ROWS, D_MODEL), lambda i: (0, 1)),
            pl.BlockSpec((1, D_MODEL), lambda i: (0, 0)),
        ],
        out_specs=pl.BlockSpec((TM, D_MODEL), lambda i: (i, 0)),
        out_shape=jax.ShapeDtypeStruct((N_TOK, D_MODEL), BF16),
        compiler_params=_cparams(("arbitrary",)),
        name="prenorm",
    )(x2d, ctx2d, mod, mod, norm_w.reshape(1, D_MODEL))


def _ident(a):
    return a


def _gelu(a):
    return 0.5 * a * (1.0 + lax.erf(a * math.sqrt(0.5)))


def _proj_tile(n_rows):
    tp = n_rows // 8
    assert tp * 8 == n_rows and tp % (N_CHUNKS * HALO) == 0
    return tp


def _proj_kernel(h_ref, w_ref, o_ref, w_scr, *, epilogue):
    @pl.when(pl.program_id(1) == 0)
    def _():
        w_scr[...] = w_ref[...].astype(BF16)

    rc = h_ref.shape[0] // N_CHUNKS
    for c in range(N_CHUNKS):
        rows = slice(c * rc, (c + 1) * rc)
        acc = jnp.dot(h_ref[rows, :], w_scr[...], preferred_element_type=F32)
        o_ref[rows, :] = epilogue(acc).astype(BF16)


def _projection(h, w_all, layer, col0, n_cols, tn, epilogue, name, n_rows):
    tp = _proj_tile(n_rows)
    return pl.pallas_call(
        functools.partial(_proj_kernel, epilogue=epilogue),
        grid=(n_cols // tn, n_rows // tp),
        in_specs=[
            pl.BlockSpec((tp, D_MODEL), lambda j, i: (i, 0)),
            pl.BlockSpec((pl.Squeezed(), pl.Element(D_MODEL), pl.Element(tn)),
                         lambda j, i: (layer, 0, pl.multiple_of(col0 + j * tn, 128))),
        ],
        out_specs=pl.BlockSpec((tp, tn), lambda j, i: (i, j)),
        out_shape=jax.ShapeDtypeStruct((n_rows, n_cols), BF16),
        scratch_shapes=[pltpu.VMEM((D_MODEL, tn), BF16)],
        compiler_params=_cparams(("arbitrary", "arbitrary")),
        name=name,
    )(h, w_all)


def _proj_f_kernel(h_ref, w_ref, *rest, with_tok):
    if with_tok:
        og_ref, ot_ref, w_scr, t_scr = rest
    else:
        og_ref, w_scr, t_scr = rest

    @pl.when(pl.program_id(0) == 0)
    def _():
        w_scr[...] = w_ref[...].astype(BF16)

    n1 = GRID_W
    acc = jnp.dot(h_ref[...], w_scr[...], preferred_element_type=F32)
    if with_tok:
        ot_ref[...] = acc.astype(BF16)
    for a in range(TM // n1):
        for g in range(F_GROUPS):
            t_scr[g, a * PITCH:a * PITCH + n1, :] = acc[a * n1:(a + 1) * n1, g * HEAD_DIM:(g + 1) * HEAD_DIM]
    for b in range(n1):
        for g in range(F_GROUPS):
            col = b * F_W + g * HEAD_DIM
            og_ref[:, col:col + HEAD_DIM] = t_scr[g, pl.ds(b, TM // n1, stride=PITCH), :].astype(BF16)


def _projection_f(h, w_all, layer, n_rows, with_tok):
    n1 = GRID_W
    out_specs = [pl.BlockSpec((TM // n1, n1 * F_W), lambda i: (i, 0))]
    out_shape = [jax.ShapeDtypeStruct((n_rows // n1, n1 * F_W), BF16)]
    if with_tok:
        out_specs.append(pl.BlockSpec((TM, F_W), lambda i: (i, 0)))
        out_shape.append(jax.ShapeDtypeStruct((n_rows, F_W), BF16))
    return pl.pallas_call(
        functools.partial(_proj_f_kernel, with_tok=with_tok),
        grid=(n_rows // TM,),
        in_specs=[
            pl.BlockSpec((TM, D_MODEL), lambda i: (i, 0)),
            pl.BlockSpec((pl.Squeezed(), D_MODEL, F_W), lambda i: (layer, 0, 0)),
        ],
        out_specs=out_specs,
        out_shape=out_shape,
        scratch_shapes=[pltpu.VMEM((D_MODEL, F_W), BF16),
                        pltpu.VMEM((F_GROUPS, (TM // n1) * PITCH, HEAD_DIM), F32)],
        compiler_params=_cparams(("arbitrary",)),
        name="in_proj_f",
    )(h, w_all)


_N_QBLK = ROWS // QROWS
_BIAS_VARIANTS = (0, 1, _N_QBLK - 1)


def _key_row0(nb):
    return min(max(QROWS * nb - NA_KH // 2, 0), ROWS - KROWS)


def _bias_kernel(rpb_ref, o_ref, pair_scr):
    h = pl.program_id(0)
    qc = lax.broadcasted_iota(jnp.int32, (GRID_W, 2 * GRID_W), 0)
    lane = lax.broadcasted_iota(jnp.int32, (GRID_W, 2 * GRID_W), 1)
    kc = lane & (GRID_W - 1)
    low = lane < GRID_W
    dc = jnp.clip(kc - qc + NA_KW - 1, 0, 2 * NA_KW - 2)
    c_start = jnp.clip(qc - NA_KW // 2, 0, GRID_W - NA_KW)
    col_in = (kc >= c_start) & (kc < c_start + NA_KW)
    n_dr = 2 * NA_KH - 1
    for dr0 in range(-1, n_dr):
        acc = jnp.zeros((GRID_W, 2 * GRID_W), F32)
        for d in range(2 * NA_KW - 1):
            lo = rpb_ref[h, dr0, d] if 0 <= dr0 < n_dr else 0.0
            hi = rpb_ref[h, dr0 + 1, d] if 0 <= dr0 + 1 < n_dr else 0.0
            acc = jnp.where(dc == d, jnp.where(low, lo, hi), acc)
        pair_scr[dr0 + 1] = acc * _LOG2E
    for vi, nb in enumerate(_BIAS_VARIANTS):
        r0 = _key_row0(nb)
        for i in range(QROWS):
            r = QROWS * nb + i
            r_start = min(max(r - NA_KH // 2, 0), ROWS - NA_KH)
            for tp in range(KROWS // 2):
                kr = r0 + 2 * tp
                ok_lo = r_start <= kr < r_start + NA_KH
                ok_hi = r_start <= kr + 1 < r_start + NA_KH
                dr0 = kr - r + NA_KH - 1
                if not (ok_lo or ok_hi):
                    tile = jnp.full((GRID_W, 2 * GRID_W), NEG_INF, F32)
                else:
                    valid = col_in
                    if not ok_lo:
                        valid = valid & jnp.logical_not(low)
                    if not ok_hi:
                        valid = valid & low
                    tile = jnp.where(valid, pair_scr[dr0 + 1], NEG_INF)
                o_ref[vi, 0, i * GRID_W:(i + 1) * GRID_W, tp * 2 * GRID_W:(tp + 1) * 2 * GRID_W] = tile


def _bias_table(rpb):
    nv = len(_BIAS_VARIANTS)
    return pl.pallas_call(
        _bias_kernel,
        grid=(NA_HEADS,),
        in_specs=[pl.BlockSpec(memory_space=pltpu.SMEM)],
        out_specs=pl.BlockSpec((nv, 1, QROWS * GRID_W, KROWS * GRID_W), lambda h: (0, h, 0, 0)),
        out_shape=jax.ShapeDtypeStruct((nv, NA_HEADS, QROWS * GRID_W, KROWS * GRID_W), F32),
        scratch_shapes=[pltpu.VMEM((2 * NA_KH, GRID_W, 2 * GRID_W), F32)],
        compiler_params=_cparams(("arbitrary",)),
        name="attn_bias_table",
    )(rpb)


_QK_DIMS = (((1,), (1,)), ((), ()))
_LOG2E = math.log2(math.e)
_SCALE = HEAD_DIM ** -0.5 * _LOG2E
_QB = QROWS * GRID_W


def _attn_kernel(q_ref, k_ref, v_ref, kc_ref, vc_ref, bias_ref, *rest, with_ctx):
    kc = kc_ref[...]
    vc = vc_ref[...]
    if with_ctx:
        qc_ref, o_ref, oc_ref, sw_scr, sc_scr = rest
    else:
        o_ref, sw_scr, sc_scr = rest

    def key_start(nb):
        r0 = jnp.clip(QROWS * nb - NA_KH // 2, 0, ROWS - KROWS)
        return pl.multiple_of(r0 * GRID_W, GRID_W)

    def scores(nb, slot):
        qrow = pl.multiple_of(nb * _QB, _QB)
        variant = jnp.where(nb == 0, 0, jnp.where(nb == _N_QBLK - 1, 2, 1))
        q = (q_ref[pl.ds(qrow, _QB), :].astype(F32) * _SCALE).astype(BF16)
        kw = k_ref[pl.ds(key_start(nb), KROWS * GRID_W), :]
        sw_scr[slot] = lax.dot_general(q, kw, _QK_DIMS, preferred_element_type=F32) + bias_ref[variant, 0]
        sc_scr[slot] = lax.dot_general(q, kc, _QK_DIMS, preferred_element_type=F32)

    def finish(nb, slot):
        qrow = pl.multiple_of(nb * _QB, _QB)
        vw = v_ref[pl.ds(key_start(nb), KROWS * GRID_W), :]
        s_w = sw_scr[slot]
        s_c = sc_scr[slot]
        m = jnp.maximum(jnp.max(s_w, axis=-1, keepdims=True), jnp.max(s_c, axis=-1, keepdims=True))
        p_w = jnp.exp2(s_w - m)
        p_c = jnp.exp2(s_c - m)
        denom = jnp.sum(p_w, axis=-1, keepdims=True) + jnp.sum(p_c, axis=-1, keepdims=True)
        o = (jnp.dot(p_w.astype(BF16), vw, preferred_element_type=F32)
             + jnp.dot(p_c.astype(BF16), vc, preferred_element_type=F32))
        o_ref[pl.ds(qrow, _QB), :] = (o / denom).astype(BF16)

    if with_ctx:
        q = (qc_ref[...].astype(F32) * _SCALE).astype(BF16)
        s = lax.dot_general(q, kc, _QK_DIMS, preferred_element_type=F32)
        p = jnp.exp2(s - jnp.max(s, axis=-1, keepdims=True))
        o = jnp.dot(p.astype(BF16), vc, preferred_element_type=F32)
        oc_ref[...] = (o / jnp.sum(p, axis=-1, keepdims=True)).astype(BF16)

    scores(0, 0)

    def pair(t, carry):
        nb = 2 * t
        scores(nb + 1, 1)
        finish(nb, 0)
        scores(jnp.minimum(nb + 2, _N_QBLK - 1), 0)
        finish(nb + 1, 1)
        return carry

    lax.fori_loop(0, _N_QBLK // 2, pair, 0, unroll=ATTN_UNROLL)


def _attention(qkv, bias, with_ctx):
    ctx0 = N_LAT // CTX_LEN
    nv = len(_BIAS_VARIANTS)
    in_specs = [
        pl.BlockSpec((SEQ, HEAD_DIM), lambda b, h: (b, h)),
        pl.BlockSpec((SEQ, HEAD_DIM), lambda b, h: (b, NA_HEADS + h)),
        pl.BlockSpec((SEQ, HEAD_DIM), lambda b, h: (b, 2 * NA_HEADS + h)),
        pl.BlockSpec((CTX_LEN, HEAD_DIM), lambda b, h: (ctx0 + b, NA_HEADS + h)),
        pl.BlockSpec((CTX_LEN, HEAD_DIM), lambda b, h: (ctx0 + b, 2 * NA_HEADS + h)),
        pl.BlockSpec((nv, 1, _QB, KROWS * GRID_W), lambda b, h: (0, h, 0, 0)),
    ]
    out_specs = [pl.BlockSpec((SEQ, HEAD_DIM), lambda b, h: (b, h))]
    out_shape = [jax.ShapeDtypeStruct((N_LAT, NA_W), BF16)]
    operands = [qkv, qkv, qkv, qkv, qkv, bias]
    if with_ctx:
        in_specs.append(pl.BlockSpec((CTX_LEN, HEAD_DIM), lambda b, h: (ctx0 + b, h)))
        out_specs.append(pl.BlockSpec((CTX_LEN, HEAD_DIM), lambda b, h: (b, h)))
        out_shape.append(jax.ShapeDtypeStruct((N_CTX, NA_W), BF16))
        operands.append(qkv)
    return pl.pallas_call(
        functools.partial(_attn_kernel, with_ctx=with_ctx),
        grid=(BATCH, NA_HEADS),
        in_specs=in_specs,
        out_specs=out_specs,
        out_shape=out_shape,
        scratch_shapes=[pltpu.VMEM((2, _QB, KROWS * GRID_W), F32), pltpu.VMEM((2, _QB, CTX_LEN), F32)],
        compiler_params=_cparams(("arbitrary", "arbitrary")),
        name="neighborhood_attention",
    )(*operands)


def _dft_tables():
    n1 = GRID_W
    idx = np.arange(n1)
    ang1 = 2 * np.pi * np.outer(idx, idx) / n1
    w1 = np.concatenate([np.cos(ang1), -np.sin(ang1)], axis=0)
    q = idx[:, None, None]
    p = idx[None, :, None]
    b = idx[None, None, :]
    ang2 = 2 * np.pi * b * (n1 * p + q) / SEQ
    wc, ws = np.cos(ang2), np.sin(ang2)
    m2 = np.concatenate([np.concatenate([wc, ws], axis=2),
                         np.concatenate([-ws, wc], axis=2)], axis=1)
    d = np.arange(HEAD_DIM)
    ang3 = 2 * np.pi * np.outer(d, d) / HEAD_DIM
    cs = np.concatenate([np.cos(ang3), np.sin(ang3)], axis=0) / math.sqrt(SEQ * HEAD_DIM)
    n = np.arange(CTX_LEN)
    angc = 2 * np.pi * np.outer(n, n) / CTX_LEN
    t1 = np.concatenate([np.cos(angc), np.sin(angc)], axis=0)
    cs_ctx = np.concatenate([np.cos(ang3), -np.sin(ang3)], axis=0) / math.sqrt(CTX_LEN * HEAD_DIM)
    as_bf16 = lambda a: jnp.asarray(a, dtype=F32).astype(BF16)
    return as_bf16(w1), as_bf16(m2), as_bf16(cs), as_bf16(t1), as_bf16(cs_ctx)


def _fourier_kernel(x_ref, w1_ref, m2_ref, cs_ref, *rest, with_ctx):
    if with_ctx:
        xc_ref, t1_ref, csc_ref, o_ref, oc_ref, re_scr, im_scr, y_scr = rest
        t = jnp.dot(t1_ref[...], xc_ref[...], preferred_element_type=F32)
        csc = csc_ref[...]
        for g in range(F_GROUPS):
            sl = slice(g * HEAD_DIM, (g + 1) * HEAD_DIM)
            lhs = jnp.concatenate([t[:CTX_LEN, sl], t[CTX_LEN:, sl]], axis=1).astype(BF16)
            oc_ref[:, sl] = jnp.dot(lhs, csc, preferred_element_type=F32).astype(BF16)
    else:
        o_ref, re_scr, im_scr, y_scr = rest
    n1 = GRID_W
    w1 = w1_ref[...]
    for b in range(n1):
        a = jnp.dot(w1, x_ref[:, b * F_W:(b + 1) * F_W], preferred_element_type=F32)
        for g in range(F_GROUPS):
            sl = slice(g * HEAD_DIM, (g + 1) * HEAD_DIM)
            re_scr[g, pl.ds(b, n1, stride=PITCH), :] = a[:n1, sl]
            im_scr[g, pl.ds(b, n1, stride=PITCH), :] = a[n1:, sl]

    cs = cs_ref[...]

    def column_dft(q):
        row = pl.multiple_of(q * PITCH, 8)
        a = jnp.concatenate(
            [jnp.concatenate([re_scr[g, pl.ds(row, n1), :] for g in range(F_GROUPS)], axis=1),
             jnp.concatenate([im_scr[g, pl.ds(row, n1), :] for g in range(F_GROUPS)], axis=1)], axis=0)
        return jnp.dot(m2_ref[q], a.astype(BF16), preferred_element_type=F32)

    def channel_dft(q, z):
        lhs = jnp.concatenate(
            [jnp.concatenate([z[:n1, g * HEAD_DIM:(g + 1) * HEAD_DIM],
                              z[n1:, g * HEAD_DIM:(g + 1) * HEAD_DIM]], axis=1)
             for g in range(F_GROUPS)], axis=0).astype(BF16)
        y = jnp.dot(lhs, cs, preferred_element_type=F32)
        for g in range(F_GROUPS):
            y_scr[g, pl.ds(q, n1, stride=PITCH), :] = y[g * n1:(g + 1) * n1]

    def per_group(t, carry):
        qs = [t * DFT_UNROLL + k for k in range(DFT_UNROLL)]
        zs = [column_dft(q) for q in qs]
        for q, z in zip(qs, zs):
            channel_dft(q, z)
        return carry

    lax.fori_loop(0, n1 // DFT_UNROLL, per_group, 0)
    for p in range(n1):
        for g in range(F_GROUPS):
            o_ref[p * n1:(p + 1) * n1, g * HEAD_DIM:(g + 1) * HEAD_DIM] = (
                y_scr[g, p * PITCH:p * PITCH + n1, :].astype(BF16))


def _fourier(fx, f_ctx, tables, with_ctx):
    w1, m2, cs, t1, cs_ctx = tables
    n1 = GRID_W
    const = lambda shape: pl.BlockSpec(shape, lambda b: (0,) * len(shape))
    in_specs = [pl.BlockSpec((n1, n1 * F_W), lambda b: (b, 0)), const((2 * n1, n1)),
                const((n1, 2 * n1, 2 * n1)), const((2 * HEAD_DIM, HEAD_DIM))]
    out_specs = [pl.BlockSpec((SEQ, F_W), lambda b: (b, 0))]
    out_shape = [jax.ShapeDtypeStruct((N_LAT, F_W), BF16)]
    operands = [fx, w1, m2, cs]
    if with_ctx:
        ctx0 = N_LAT // CTX_LEN
        in_specs += [pl.BlockSpec((CTX_LEN, F_W), lambda b: (ctx0 + b, 0)), const((2 * CTX_LEN, CTX_LEN)),
                     const((2 * HEAD_DIM, HEAD_DIM))]
        out_specs.append(pl.BlockSpec((CTX_LEN, F_W), lambda b: (b, 0)))
        out_shape.append(jax.ShapeDtypeStruct((N_CTX, F_W), BF16))
        operands += [f_ctx, t1, cs_ctx]
    return pl.pallas_call(
        functools.partial(_fourier_kernel, with_ctx=with_ctx),
        grid=(BATCH,),
        in_specs=in_specs,
        out_specs=out_specs,
        out_shape=out_shape,
        scratch_shapes=[pltpu.VMEM((F_GROUPS, n1 * PITCH, HEAD_DIM), F32)] * 3,
        compiler_params=_cparams(("arbitrary",)),
        name="fourier_mix",
    )(*operands)


_Z_ROWS = 2 * CHUNK


def _proj_gating_kernel(h_ref, w_ref, nw_ref, ws_ref, bs_ref, o_ref, w_scr):
    @pl.when(pl.program_id(0) == 0)
    def _():
        w_scr[...] = w_ref[...].astype(BF16)

    nw = nw_ref[...]

    def project(r):
        rows = slice(r * _Z_ROWS, (r + 1) * _Z_ROWS)
        return _gelu(jnp.dot(h_ref[rows, :], w_scr[...], preferred_element_type=F32))

    n_r = TM // _Z_ROWS
    z_next = project(0)
    for r in range(n_r):
        z = z_next
        if r + 1 < n_r:
            z_next = project(r + 1)
        for g in range(C_GROUPS):
            sl = slice(g * HEAD_DIM, (g + 1) * HEAD_DIM)
            v = z[:, C_W + g * HEAD_DIM:C_W + (g + 1) * HEAD_DIM]
            mu = jnp.mean(v, axis=-1, keepdims=True)
            vc = v - mu
            var = jnp.mean(vc * vc, axis=-1, keepdims=True)
            vn = (vc * lax.rsqrt(var + EPS) * nw[:, sl]).astype(BF16)
            n_ch = _Z_ROWS // CHUNK
            vn2 = jnp.concatenate([vn[c * CHUNK:(c + 1) * CHUNK] for c in range(n_ch)], axis=1)
            sp2 = jnp.dot(ws_ref[g], vn2, preferred_element_type=F32)
            for c in range(n_ch):
                cr = slice(c * CHUNK, (c + 1) * CHUNK)
                sp = sp2[:, c * HEAD_DIM:(c + 1) * HEAD_DIM] + bs_ref[g]
                o_ref[r * _Z_ROWS + c * CHUNK:r * _Z_ROWS + (c + 1) * CHUNK, sl] = (
                    z[cr, sl] * sp).astype(BF16)


def _projection_gating(h, w_all, layer, norm_w, ws, bs, n_rows):
    bs_exp = jnp.broadcast_to(bs[:, :, None], (C_GROUPS, CHUNK, HEAD_DIM))
    col0 = F_W + 3 * NA_W
    return pl.pallas_call(
        _proj_gating_kernel,
        grid=(n_rows // TM,),
        in_specs=[
            pl.BlockSpec((TM, D_MODEL), lambda i: (i, 0)),
            pl.BlockSpec((pl.Squeezed(), pl.Element(D_MODEL), pl.Element(2 * C_W)),
                         lambda i: (layer, 0, col0)),
            pl.BlockSpec((1, C_W), lambda i: (0, 0)),
            pl.BlockSpec((C_GROUPS, CHUNK, CHUNK), lambda i: (0, 0, 0)),
            pl.BlockSpec((C_GROUPS, CHUNK, HEAD_DIM), lambda i: (0, 0, 0)),
        ],
        out_specs=pl.BlockSpec((TM, C_W), lambda i: (i, 0)),
        out_shape=jax.ShapeDtypeStruct((n_rows, C_W), BF16),
        scratch_shapes=[pltpu.VMEM((D_MODEL, 2 * C_W), BF16)],
        compiler_params=_cparams(("arbitrary",)),
        name="in_proj_gmlp",
    )(h, w_all, norm_w.reshape(1, C_W), ws.astype(BF16), bs_exp)


def _merge_kernel(ya_ref, yb_ref, atta_ref, attb_ref, sg_ref, gate_ref, xa_ref, xb_ref, g1_ref, sh_ref,
                  sc_ref, nw_ref, wf_ref, wna_ref, wc_ref, wo_ref, o_ref, h_ref, m_scr, *, n_a_tiles):
    i = pl.program_id(0)
    r = _mod_row(i, TM_RES)
    from_a = i < n_a_tiles
    y = jnp.where(from_a, ya_ref[...], yb_ref[...])
    att = jnp.where(from_a, atta_ref[...], attb_ref[...])
    a_f = jnp.dot(y, wf_ref[...], preferred_element_type=F32)
    a_na = jnp.dot(att, wna_ref[...], preferred_element_type=F32)
    a_c = jnp.dot(sg_ref[...], wc_ref[...], preferred_element_type=F32)
    g_f = gate_ref[:, :D_MODEL].astype(F32)
    g_na = gate_ref[:, D_MODEL:2 * D_MODEL].astype(F32)
    g_c = gate_ref[:, 2 * D_MODEL:].astype(F32)
    m_scr[...] = (g_f * a_f + g_na * a_na + g_c * a_c).astype(BF16)
    o = jnp.dot(m_scr[...], wo_ref[...], preferred_element_type=F32)
    x = jnp.where(from_a, xa_ref[...], xb_ref[...])
    xn = x + g1_ref[pl.ds(r, 1), :] * o
    o_ref[...] = xn
    ms = jnp.mean(xn * xn, axis=-1, keepdims=True)
    wmod = nw_ref[...] * (1.0 + sc_ref[pl.ds(r, 1), :])
    h_ref[...] = (xn * lax.rsqrt(ms + EPS) * wmod + sh_ref[pl.ds(r, 1), :]).astype(BF16)


def _merge(ya, yb, atta, attb, sg, gate, xa, xb, mod, norm2_w, wf, wna, wc, wo, layer, n_rows):
    tm = TM_RES
    na = N_LAT // tm
    nbt = max((n_rows - N_LAT) // tm, 1)
    a_spec = lambda w: pl.BlockSpec((tm, w), lambda i: (jnp.minimum(i, na - 1), 0))
    b_spec = lambda w: pl.BlockSpec((tm, w), lambda i: (jnp.clip(i - na, 0, nbt - 1), 0))
    return pl.pallas_call(
        functools.partial(_merge_kernel, n_a_tiles=na),
        grid=(n_rows // tm,),
        in_specs=[
            a_spec(F_W), b_spec(F_W), a_spec(NA_W), b_spec(NA_W),
            pl.BlockSpec((tm, C_W), lambda i: (i, 0)),
            pl.BlockSpec((tm, 3 * D_MODEL), lambda i: (i, 0)),
            a_spec(D_MODEL), b_spec(D_MODEL),
            pl.BlockSpec((MOD_ROWS, D_MODEL), lambda i: (0, 2)),
            pl.BlockSpec((MOD_ROWS, D_MODEL), lambda i: (0, 3)),
            pl.BlockSpec((MOD_ROWS, D_MODEL), lambda i: (0, 4)),
            pl.BlockSpec((1, D_MODEL), lambda i: (0, 0)),
            _resident((F_W, D_MODEL), layer),
            _resident((NA_W, D_MODEL), layer),
            _resident((C_W, D_MODEL), layer),
            _resident((D_MODEL, D_MODEL), layer),
        ],
        out_specs=[pl.BlockSpec((tm, D_MODEL), lambda i: (i, 0)),
                   pl.BlockSpec((tm, D_MODEL), lambda i: (i, 0))],
        out_shape=[jax.ShapeDtypeStruct((n_rows, D_MODEL), F32),
                   jax.ShapeDtypeStruct((n_rows, D_MODEL), BF16)],
        scratch_shapes=[pltpu.VMEM((tm, D_MODEL), BF16)],
        compiler_params=_cparams(("arbitrary",)),
        name="gated_merge",
    )(ya, yb, atta, attb, sg, gate, xa, xb, mod, mod, mod, norm2_w.reshape(1, D_MODEL), wf, wna, wc, wo)


_FF_TILES = D_FF // TN


def _ffn_up_kernel(h_ref, hp_ref, hn_ref, wa_ref, wg_ref, cw_ref, cb_ref, o_ref, wa_scr, wg_scr, a_scr):
    i = pl.program_id(1)
    tp = h_ref.shape[0]
    rc = tp // N_CHUNKS

    @pl.when(i == 0)
    def _():
        wa_scr[...] = wa_ref[...].astype(BF16)
        wg_scr[...] = wg_ref[...].astype(BF16)

    n_chunks = N_CHUNKS
    for c in range(n_chunks):
        parts = [h_ref[c * rc:(c + 1) * rc, :]]
        lo, hi = HALO + c * rc, HALO + (c + 1) * rc
        if c == 0:
            parts.insert(0, hp_ref[...])
            lo -= HALO
        if c == n_chunks - 1:
            parts.append(hn_ref[...])
            hi += HALO
        lhs = parts[0] if len(parts) == 1 else jnp.concatenate(parts, axis=0)
        a_scr[lo:hi, :] = jnp.dot(lhs, wa_scr[...], preferred_element_type=F32)

    w0, w1, w2, cb = cw_ref[0:1], cw_ref[1:2], cw_ref[2:3], cb_ref[...]
    gate_proj = lambda c: jnp.dot(h_ref[c * rc:(c + 1) * rc, :], wg_scr[...], preferred_element_type=F32)
    gv_nxt = gate_proj(0)
    for c in range(n_chunks):
        r0 = HALO + c * rc
        gv = gv_nxt
        if c + 1 < n_chunks:
            gv_nxt = gate_proj(c + 1)
        row = i * tp + c * rc + lax.broadcasted_iota(jnp.int32, (rc, TN), 0)
        seq_mask = jnp.where(row < N_LAT, SEQ - 1, CTX_LEN - 1)
        pos = row & seq_mask
        prev = jnp.where(pos != 0, a_scr[r0 - 1:r0 - 1 + rc, :], 0.0)
        nxt = jnp.where(pos != seq_mask, a_scr[r0 + 1:r0 + 1 + rc, :], 0.0)
        a = cb + w0 * prev + w1 * a_scr[r0:r0 + rc, :] + w2 * nxt
        o_ref[c * rc:(c + 1) * rc, :] = (a * jax.nn.sigmoid(a) * gv).astype(BF16)


def _ffn_up(h2, w_up_all, layer, conv_w, conv_b, n_rows):
    tp = _proj_tile(n_rows)
    hb = tp // HALO
    last = h2.shape[0] // HALO - 1

    def wspec(off):
        return pl.BlockSpec((pl.Squeezed(), D_MODEL, TN), lambda j, i: (layer, 0, off + j))

    return pl.pallas_call(
        _ffn_up_kernel,
        grid=(_FF_TILES, n_rows // tp),
        in_specs=[
            pl.BlockSpec((tp, D_MODEL), lambda j, i: (i, 0)),
            pl.BlockSpec((HALO, D_MODEL), lambda j, i: (jnp.maximum(i * hb - 1, 0), 0)),
            pl.BlockSpec((HALO, D_MODEL), lambda j, i: (jnp.minimum((i + 1) * hb, last), 0)),
            wspec(0),
            wspec(_FF_TILES),
            pl.BlockSpec((CONV_W, TN), lambda j, i: (0, j)),
            pl.BlockSpec((1, TN), lambda j, i: (0, j)),
        ],
        out_specs=pl.BlockSpec((tp, TN), lambda j, i: (i, j)),
        out_shape=jax.ShapeDtypeStruct((n_rows, D_FF), BF16),
        scratch_shapes=[pltpu.VMEM((D_MODEL, TN), BF16), pltpu.VMEM((D_MODEL, TN), BF16),
                        pltpu.VMEM((tp + 2 * HALO, TN), F32)],
        compiler_params=_cparams(("arbitrary", "arbitrary")),
        name="ffn_up_conv",
    )(h2, h2, h2, w_up_all, w_up_all, conv_w, conv_b.reshape(1, D_FF))


def _ffn_down_kernel(a_ref, w_ref, x_ref, g2_ref, sh_ref, sc_ref, nw_ref, *refs, final):
    tm = a_ref.shape[0]
    r = _mod_row(pl.program_id(0), tm)
    xs_ref = refs[1] if final else refs[0]
    act = a_ref[...]
    ss = jnp.zeros((tm, 1), F32)
    n_c = D_MODEL // TN
    down = lambda c: jnp.dot(act, w_ref[:, c * TN:(c + 1) * TN], preferred_element_type=F32)
    o_nxt = down(0)
    for c in range(n_c):
        sl = slice(c * TN, (c + 1) * TN)
        o = o_nxt
        if c + 1 < n_c:
            o_nxt = down(c + 1)
        xn = x_ref[:, sl] + g2_ref[pl.ds(r, 1), sl] * o
        xs_ref[:, sl] = xn
        ss = ss + jnp.sum(xn * xn, axis=-1, keepdims=True)
    inv = lax.rsqrt(ss * (1.0 / D_MODEL) + EPS)
    if final:
        refs[0][...] = xs_ref[...] * inv * nw_ref[...]
    else:
        wmod = nw_ref[...] * (1.0 + sc_ref[pl.ds(r, 1), :])
        refs[1][...] = (xs_ref[...] * inv * wmod + sh_ref[pl.ds(r, 1), :]).astype(BF16)


def _ffn_down(act, w_down, layer, x_all, mod, mod_next, norm_w, n_rows, final):
    tm = TM_DOWN
    assert n_rows % tm == 0 and N_LAT % tm == 0
    row = lambda i: (i, 0)
    if final:
        out_specs = pl.BlockSpec((tm, D_MODEL), row)
        out_shape = jax.ShapeDtypeStruct((n_rows, D_MODEL), F32)
        scratch = [pltpu.VMEM((tm, D_MODEL), F32)]
    else:
        out_specs = [pl.BlockSpec((tm, D_MODEL), row), pl.BlockSpec((tm, D_MODEL), row)]
        out_shape = [jax.ShapeDtypeStruct((n_rows, D_MODEL), F32),
                     jax.ShapeDtypeStruct((n_rows, D_MODEL), BF16)]
        scratch = []
    return pl.pallas_call(
        functools.partial(_ffn_down_kernel, final=final),
        grid=(n_rows // tm,),
        in_specs=[
            pl.BlockSpec((tm, D_FF), row),
            _resident((D_FF, D_MODEL), layer),
            pl.BlockSpec((tm, D_MODEL), row),
            pl.BlockSpec((MOD_ROWS, D_MODEL), lambda i: (0, 5)),
            pl.BlockSpec((MOD_ROWS, D_MODEL), lambda i: (0, 0)),
            pl.BlockSpec((MOD_ROWS, D_MODEL), lambda i: (0, 1)),
            pl.BlockSpec((1, D_MODEL), lambda i: (0, 0)),
        ],
        out_specs=out_specs,
        out_shape=out_shape,
        scratch_shapes=scratch,
        compiler_params=pltpu.CompilerParams(dimension_semantics=("arbitrary",),
                                             vmem_limit_bytes=60 * 1024 * 1024),
        name="ffn_down",
    )(act, w_down, x_all, mod, mod_next, mod_next, norm_w.reshape(1, D_MODEL))


def kernel(x, c, ctx, c_ctx, ada_w, ada_b, norm1_w, norm2_w, w_in, na_rpb, gmlp_norm_w, gmlp_ws,
           gmlp_bs, w_f_out, w_na_out, w_c_out, w_o, ffn_up, ffn_conv_w, ffn_conv_b, ffn_down,
           final_norm_w):
    assert x.shape == (BATCH, SEQ, D_MODEL) and ctx.shape == (BATCH, CTX_LEN, D_MODEL)
    cvec = jnp.concatenate(
        [c, c_ctx[None, :], jnp.zeros((MOD_ROWS - BATCH - 1, D_MODEL), F32)], axis=0)
    mod_all = _modulation(cvec, ada_w, ada_b)
    tables = _dft_tables()
    merge_w = tuple(w.astype(BF16) for w in (w_f_out, w_na_out, w_c_out, w_o))
    w_down = ffn_down.astype(BF16)
    xa = x.reshape(N_LAT, D_MODEL)
    xb = ctx.reshape(N_CTX, D_MODEL)
    h1 = _prenorm(xa, xb, mod_all[0], norm1_w[0])

    for l in range(DEPTH):
        last = l == DEPTH - 1
        n_rows = N_LAT if last else N_TOK
        mod = mod_all[l]
        fs = _projection_f(h1, w_in, l, n_rows, not last)
        qkv = _projection(h1, w_in, l, F_W, 3 * NA_W, 1024, _ident, "in_proj_qkv", N_TOK)
        sg = _projection_gating(h1, w_in, l, gmlp_norm_w[l], gmlp_ws[l], gmlp_bs[l], n_rows)
        gate = _projection(h1, w_in, l, F_W + 3 * NA_W + 2 * C_W, 3 * D_MODEL, 1024, jax.nn.sigmoid,
                           "in_proj_gate", n_rows)
        atts = _attention(qkv, _bias_table(na_rpb[l]), not last)
        ys = _fourier(fs[0], fs[-1], tables, not last)
        x_mid, h2 = _merge(ys[0], ys[-1], atts[0], atts[-1], sg, gate, xa, xb, mod, norm2_w[l],
                           *merge_w, l, n_rows)
        act = _ffn_up(h2, ffn_up, l, ffn_conv_w[l], ffn_conv_b[l], n_rows)
        if last:
            out = _ffn_down(act, w_down, l, x_mid, mod, mod, final_norm_w, n_rows, True)
        else:
            xa, h1 = _ffn_down(act, w_down, l, x_mid, mod, mod_all[l + 1], norm1_w[l + 1], n_rows, False)
            xb = xa
    return out.reshape(BATCH, SEQ, D_MODEL)
```

```python
import functools
import math

import numpy as np
import jax
import jax.numpy as jnp
from jax import lax
from jax.experimental import pallas as pl
from jax.experimental.pallas import tpu as pltpu

F32 = jnp.float32
BF16 = jnp.bfloat16

D_MODEL = 2048
BATCH = 4
SEQ = 4096
DEPTH = 2
GRID_W = 64
ROWS = SEQ // GRID_W
CTX_LEN = 256
HEAD_DIM = 128
F_GROUPS = 4
F_W = F_GROUPS * HEAD_DIM
NA_HEADS = 8
NA_W = NA_HEADS * HEAD_DIM
NA_KH = 8
NA_KW = 16
C_GROUPS = 4
C_W = C_GROUPS * HEAD_DIM
CHUNK = 128
D_FF = 5632
CONV_W = 3
EPS = 1e-6
NEG_INF = -1e30
IN_W = F_W + 3 * NA_W + 2 * C_W + 3 * D_MODEL

N_LAT = BATCH * SEQ
N_CTX = BATCH * CTX_LEN
N_TOK = N_LAT + N_CTX
MOD_ROWS = 8

TM = 1024
TN = 512
N_CHUNKS = 8
TM_RES = 256
TM_DOWN = 512
HALO = 16
QROWS = 4
KROWS = 12
ATTN_UNROLL = 4
DFT_UNROLL = 8
PITCH = 72
VMEM_LIMIT = 56 * 1024 * 1024


def _cparams(sem):
    return pltpu.CompilerParams(dimension_semantics=sem, vmem_limit_bytes=VMEM_LIMIT)


def _mod_row(i, tm):
    return jnp.minimum((i * tm) // SEQ, BATCH)


def _norm_mod(x, nw, sh, sc):
    ms = jnp.mean(x * x, axis=-1, keepdims=True)
    return (x * lax.rsqrt(ms + EPS) * nw) * (1.0 + sc) + sh


def _rms(x, nw):
    ms = jnp.mean(x * x, axis=-1, keepdims=True)
    return x * lax.rsqrt(ms + EPS) * nw


def _resident(shape, layer):
    return pl.BlockSpec((pl.Squeezed(),) + shape, lambda *_: (layer,) + (0,) * len(shape),
                        pipeline_mode=pl.Buffered(1))


def _mod_kernel(c_ref, w_ref, b_ref, o_ref):
    c = c_ref[...]
    s = (c * jax.nn.sigmoid(c)).astype(BF16)
    o_ref[0] = jnp.dot(s, w_ref[0].astype(BF16), preferred_element_type=F32) + b_ref[0]


def _modulation(cvec, ada_w, ada_b):
    tn = 1024
    return pl.pallas_call(
        _mod_kernel,
        grid=(DEPTH, 6 * D_MODEL // tn),
        in_specs=[
            pl.BlockSpec((MOD_ROWS, D_MODEL), lambda l, j: (0, 0)),
            pl.BlockSpec((1, D_MODEL, tn), lambda l, j: (l, 0, j)),
            pl.BlockSpec((1, 1, tn), lambda l, j: (l, 0, j)),
        ],
        out_specs=pl.BlockSpec((1, MOD_ROWS, tn), lambda l, j: (l, 0, j)),
        out_shape=jax.ShapeDtypeStruct((DEPTH, MOD_ROWS, 6 * D_MODEL), F32),
        compiler_params=_cparams(("arbitrary", "arbitrary")),
        name="modulation",
    )(cvec, ada_w, ada_b.reshape(DEPTH, 1, 6 * D_MODEL))


def _prenorm_kernel(xa_ref, xb_ref, sh_ref, sc_ref, nw_ref, o_ref, *, n_a_tiles):
    i = pl.program_id(0)
    r = _mod_row(i, TM)
    x = jnp.where(i < n_a_tiles, xa_ref[...], xb_ref[...])
    h = _norm_mod(x, nw_ref[...], sh_ref[pl.ds(r, 1), :], sc_ref[pl.ds(r, 1), :])
    o_ref[...] = h.astype(BF16)


def _prenorm(x2d, ctx2d, mod, norm_w):
    na = N_LAT // TM
    return pl.pallas_call(
        functools.partial(_prenorm_kernel, n_a_tiles=na),
        grid=(N_TOK // TM,),
        in_specs=[
            pl.BlockSpec((TM, D_MODEL), lambda i: (jnp.minimum(i, na - 1), 0)),
            pl.BlockSpec((TM, D_MODEL), lambda i: (jnp.maximum(i - na, 0), 0)),
            pl.BlockSpec((MOD_ROWS, D_MODEL), lambda i: (0, 0)),
            pl.BlockSpec((MOD_ROWS, D_MODEL), lambda i: (0, 1)),
            pl.BlockSpec((1, D_MODEL), lambda i: (0, 0)),
        ],
        out_specs=pl.BlockSpec((TM, D_MODEL), lambda i: (i, 0)),
        out_shape=jax.ShapeDtypeStruct((N_TOK, D_MODEL), BF16),
        compiler_params=_cparams(("arbitrary",)),
        name="prenorm",
    )(x2d, ctx2d, mod, mod, norm_w.reshape(1, D_MODEL))


def _ident(a):
    return a


def _gelu(a):
    return 0.5 * a * (1.0 + lax.erf(a * math.sqrt(0.5)))


def _proj_tile(n_rows):
    tp = n_rows // 8
    assert tp * 8 == n_rows and tp % (N_CHUNKS * HALO) == 0
    return tp


def _proj_kernel(h_ref, w_ref, o_ref, w_scr, *, epilogue):
    @pl.when(pl.program_id(1) == 0)
    def _():
        w_scr[...] = w_ref[...].astype(BF16)

    rc = h_ref.shape[0] // N_CHUNKS
    for c in range(N_CHUNKS):
        rows = slice(c * rc, (c + 1) * rc)
        acc = jnp.dot(h_ref[rows, :], w_scr[...], preferred_element_type=F32)
        o_ref[rows, :] = epilogue(acc).astype(BF16)


def _projection(h, w_all, layer, col0, n_cols, tn, epilogue, name, n_rows):
    tp = _proj_tile(n_rows)
    return pl.pallas_call(
        functools.partial(_proj_kernel, epilogue=epilogue),
        grid=(n_cols // tn, n_rows // tp),
        in_specs=[
            pl.BlockSpec((tp, D_MODEL), lambda j, i: (i, 0)),
            pl.BlockSpec((pl.Squeezed(), pl.Element(D_MODEL), pl.Element(tn)),
                         lambda j, i: (layer, 0, pl.multiple_of(col0 + j * tn, 128))),
        ],
        out_specs=pl.BlockSpec((tp, tn), lambda j, i: (i, j)),
        out_shape=jax.ShapeDtypeStruct((n_rows, n_cols), BF16),
        scratch_shapes=[pltpu.VMEM((D_MODEL, tn), BF16)],
        compiler_params=_cparams(("arbitrary", "arbitrary")),
        name=name,
    )(h, w_all)


def _proj_f_kernel(h_ref, w_ref, *rest, with_tok):
    if with_tok:
        og_ref, ot_ref, w_scr, t_scr = rest
    else:
        og_ref, w_scr, t_scr = rest

    @pl.when(pl.program_id(0) == 0)
    def _():
        w_scr[...] = w_ref[...].astype(BF16)

    n1 = GRID_W
    acc = jnp.dot(h_ref[...], w_scr[...], preferred_element_type=F32)
    if with_tok:
        ot_ref[...] = acc.astype(BF16)
    for a in range(TM // n1):
        for g in range(F_GROUPS):
            t_scr[g, a * PITCH:a * PITCH + n1, :] = acc[a * n1:(a + 1) * n1, g * HEAD_DIM:(g + 1) * HEAD_DIM]
    for b in range(n1):
        for g in range(F_GROUPS):
            col = b * F_W + g * HEAD_DIM
            og_ref[:, col:col + HEAD_DIM] = t_scr[g, pl.ds(b, TM // n1, stride=PITCH), :].astype(BF16)


def _projection_f(h, w_all, layer, n_rows, with_tok):
    n1 = GRID_W
    out_specs = [pl.BlockSpec((TM // n1, n1 * F_W), lambda i: (i, 0))]
    out_shape = [jax.ShapeDtypeStruct((n_rows // n1, n1 * F_W), BF16)]
    if with_tok:
        out_specs.append(pl.BlockSpec((TM, F_W), lambda i: (i, 0)))
        out_shape.append(jax.ShapeDtypeStruct((n_rows, F_W), BF16))
    return pl.pallas_call(
        functools.partial(_proj_f_kernel, with_tok=with_tok),
        grid=(n_rows // TM,),
        in_specs=[
            pl.BlockSpec((TM, D_MODEL), lambda i: (i, 0)),
            pl.BlockSpec((pl.Squeezed(), D_MODEL, F_W), lambda i: (layer, 0, 0)),
        ],
        out_specs=out_specs,
        out_shape=out_shape,
        scratch_shapes=[pltpu.VMEM((D_MODEL, F_W), BF16),
                        pltpu.VMEM((F_GROUPS, (TM // n1) * PITCH, HEAD_DIM), F32)],
        compiler_params=_cparams(("arbitrary",)),
        name="in_proj_f",
    )(h, w_all)


_N_QBLK = ROWS // QROWS
_BIAS_VARIANTS = (0, 1, _N_QBLK - 1)


def _key_row0(nb):
    return min(max(QROWS * nb - NA_KH // 2, 0), ROWS - KROWS)


def _bias_kernel(rpb_ref, o_ref, pair_scr):
    h = pl.program_id(0)
    kc = lax.broadcasted_iota(jnp.int32, (GRID_W, 2 * GRID_W), 0)
    lane = lax.broadcasted_iota(jnp.int32, (GRID_W, 2 * GRID_W), 1)
    qc = lane & (GRID_W - 1)
    low = lane < GRID_W
    dc = jnp.clip(kc - qc + NA_KW - 1, 0, 2 * NA_KW - 2)
    c_start = jnp.clip(qc - NA_KW // 2, 0, GRID_W - NA_KW)
    col_in = (kc >= c_start) & (kc < c_start + NA_KW)
    n_dr = 2 * NA_KH - 1
    for dr in range(n_dr + 1):
        acc = jnp.zeros((GRID_W, 2 * GRID_W), F32)
        for d in range(2 * NA_KW - 1):
            lo = rpb_ref[h, dr, d] if 0 <= dr < n_dr else 0.0
            hi = rpb_ref[h, dr - 1, d] if 0 <= dr - 1 < n_dr else 0.0
            acc = jnp.where(dc == d, jnp.where(low, lo, hi), acc)
        pair_scr[dr] = acc * _LOG2E
    for vi, nb in enumerate(_BIAS_VARIANTS):
        r0 = _key_row0(nb)
        for t in range(KROWS):
            kr = r0 + t
            for ip in range(QROWS // 2):
                ok = []
                for i in (2 * ip, 2 * ip + 1):
                    r_start = min(max(QROWS * nb + i - NA_KH // 2, 0), ROWS - NA_KH)
                    ok.append(r_start <= kr < r_start + NA_KH)
                dr = kr - (QROWS * nb + 2 * ip) + NA_KH - 1
                if not (ok[0] or ok[1]):
                    tile = jnp.full((GRID_W, 2 * GRID_W), NEG_INF, F32)
                else:
                    valid = col_in
                    if not ok[0]:
                        valid = valid & jnp.logical_not(low)
                    if not ok[1]:
                        valid = valid & low
                    tile = jnp.where(valid, pair_scr[dr], NEG_INF)
                o_ref[vi, 0, t * GRID_W:(t + 1) * GRID_W, ip * 2 * GRID_W:(ip + 1) * 2 * GRID_W] = tile


def _bias_table(rpb):
    nv = len(_BIAS_VARIANTS)
    return pl.pallas_call(
        _bias_kernel,
        grid=(NA_HEADS,),
        in_specs=[pl.BlockSpec(memory_space=pltpu.SMEM)],
        out_specs=pl.BlockSpec((nv, 1, KROWS * GRID_W, QROWS * GRID_W), lambda h: (0, h, 0, 0)),
        out_shape=jax.ShapeDtypeStruct((nv, NA_HEADS, KROWS * GRID_W, QROWS * GRID_W), F32),
        scratch_shapes=[pltpu.VMEM((2 * NA_KH, GRID_W, 2 * GRID_W), F32)],
        compiler_params=_cparams(("arbitrary",)),
        name="attn_bias_table",
    )(rpb)


_QK_DIMS = (((1,), (1,)), ((), ()))
_LOG2E = math.log2(math.e)
_SCALE = HEAD_DIM ** -0.5 * _LOG2E
_QB = QROWS * GRID_W


def _attn_kernel(q_ref, k_ref, v_ref, kc_ref, vc_ref, bias_ref, *rest, with_ctx):
    if with_ctx:
        qc_ref, o_ref, oc_ref, sw_scr, sc_scr, vt_scr, vct_scr = rest
    else:
        o_ref, sw_scr, sc_scr, vt_scr, vct_scr = rest
    kc = kc_ref[...]
    vc = vc_ref[...]
    pair = 2 * GRID_W

    def transposed(x):
        return x.astype(F32).T.astype(BF16)

    for j in range(SEQ // pair):
        vt_scr[j] = transposed(v_ref[j * pair:(j + 1) * pair, :])
    for j in range(CTX_LEN // pair):
        vct_scr[:, j * pair:(j + 1) * pair] = transposed(vc[j * pair:(j + 1) * pair, :])

    def key_row0(nb):
        return jnp.clip(QROWS * nb - NA_KH // 2, 0, ROWS - KROWS)

    def scores(nb, slot):
        qrow = pl.multiple_of(nb * _QB, _QB)
        variant = jnp.where(nb == 0, 0, jnp.where(nb == _N_QBLK - 1, 2, 1))
        q = (q_ref[pl.ds(qrow, _QB), :].astype(F32) * _SCALE).astype(BF16)
        kw = k_ref[pl.ds(pl.multiple_of(key_row0(nb) * GRID_W, GRID_W), KROWS * GRID_W), :]
        sw_scr[slot] = lax.dot_general(kw, q, _QK_DIMS, preferred_element_type=F32) + bias_ref[variant, 0]
        sc_scr[slot] = lax.dot_general(kc, q, _QK_DIMS, preferred_element_type=F32)

    def finish(nb, slot):
        qrow = pl.multiple_of(nb * _QB, _QB)
        j0 = key_row0(nb) // 2
        s_w = sw_scr[slot]
        s_c = sc_scr[slot]
        m = jnp.maximum(jnp.max(s_w, axis=0, keepdims=True), jnp.max(s_c, axis=0, keepdims=True))
        p_w = jnp.exp2(s_w - m)
        p_c = jnp.exp2(s_c - m)
        denom = jnp.sum(p_w, axis=0, keepdims=True) + jnp.sum(p_c, axis=0, keepdims=True)
        vt_win = jnp.concatenate([vt_scr[j0 + t] for t in range(KROWS // 2)], axis=1)
        o_t = (jnp.dot(vt_win, p_w.astype(BF16), preferred_element_type=F32)
               + jnp.dot(vct_scr[...], p_c.astype(BF16), preferred_element_type=F32))
        o_ref[pl.ds(qrow, _QB), :] = (o_t / denom).T.astype(BF16)

    if with_ctx:
        q = (qc_ref[...].astype(F32) * _SCALE).astype(BF16)
        s = lax.dot_general(q, kc, _QK_DIMS, preferred_element_type=F32)
        p = jnp.exp2(s - jnp.max(s, axis=-1, keepdims=True))
        o = jnp.dot(p.astype(BF16), vc, preferred_element_type=F32)
        oc_ref[...] = (o / jnp.sum(p, axis=-1, keepdims=True)).astype(BF16)

    scores(0, 0)

    def pair_of_blocks(t, carry):
        nb = 2 * t
        scores(nb + 1, 1)
        finish(nb, 0)
        scores(jnp.minimum(nb + 2, _N_QBLK - 1), 0)
        finish(nb + 1, 1)
        return carry

    lax.fori_loop(0, _N_QBLK // 2, pair_of_blocks, 0, unroll=ATTN_UNROLL)


def _attention(qkv, bias, with_ctx):
    ctx0 = N_LAT // CTX_LEN
    nv = len(_BIAS_VARIANTS)
    in_specs = [
        pl.BlockSpec((SEQ, HEAD_DIM), lambda b, h: (b, h)),
        pl.BlockSpec((SEQ, HEAD_DIM), lambda b, h: (b, NA_HEADS + h)),
        pl.BlockSpec((SEQ, HEAD_DIM), lambda b, h: (b, 2 * NA_HEADS + h)),
        pl.BlockSpec((CTX_LEN, HEAD_DIM), lambda b, h: (ctx0 + b, NA_HEADS + h)),
        pl.BlockSpec((CTX_LEN, HEAD_DIM), lambda b, h: (ctx0 + b, 2 * NA_HEADS + h)),
        pl.BlockSpec((nv, 1, KROWS * GRID_W, _QB), lambda b, h: (0, h, 0, 0)),
    ]
    out_specs = [pl.BlockSpec((SEQ, HEAD_DIM), lambda b, h: (b, h))]
    out_shape = [jax.ShapeDtypeStruct((N_LAT, NA_W), BF16)]
    operands = [qkv, qkv, qkv, qkv, qkv, bias]
    if with_ctx:
        in_specs.append(pl.BlockSpec((CTX_LEN, HEAD_DIM), lambda b, h: (ctx0 + b, h)))
        out_specs.append(pl.BlockSpec((CTX_LEN, HEAD_DIM), lambda b, h: (b, h)))
        out_shape.append(jax.ShapeDtypeStruct((N_CTX, NA_W), BF16))
        operands.append(qkv)
    return pl.pallas_call(
        functools.partial(_attn_kernel, with_ctx=with_ctx),
        grid=(BATCH, NA_HEADS),
        in_specs=in_specs,
        out_specs=out_specs,
        out_shape=out_shape,
        scratch_shapes=[pltpu.VMEM((2, KROWS * GRID_W, _QB), F32), pltpu.VMEM((2, CTX_LEN, _QB), F32),
                        pltpu.VMEM((SEQ // (2 * GRID_W), HEAD_DIM, 2 * GRID_W), BF16),
                        pltpu.VMEM((HEAD_DIM, CTX_LEN), BF16)],
        compiler_params=_cparams(("arbitrary", "arbitrary")),
        name="neighborhood_attention",
    )(*operands)


def _dft_tables():
    n1 = GRID_W
    idx = np.arange(n1)
    ang1 = 2 * np.pi * np.outer(idx, idx) / n1
    w1 = np.concatenate([np.cos(ang1), -np.sin(ang1)], axis=0)
    q = idx[:, None, None]
    p = idx[None, :, None]
    b = idx[None, None, :]
    ang2 = 2 * np.pi * b * (n1 * p + q) / SEQ
    wc, ws = np.cos(ang2), np.sin(ang2)
    m2 = np.concatenate([np.concatenate([wc, ws], axis=2),
                         np.concatenate([-ws, wc], axis=2)], axis=1)
    d = np.arange(HEAD_DIM)
    ang3 = 2 * np.pi * np.outer(d, d) / HEAD_DIM
    cs = np.concatenate([np.cos(ang3), np.sin(ang3)], axis=0) / math.sqrt(SEQ * HEAD_DIM)
    n = np.arange(CTX_LEN)
    angc = 2 * np.pi * np.outer(n, n) / CTX_LEN
    t1 = np.concatenate([np.cos(angc), np.sin(angc)], axis=0)
    cs_ctx = np.concatenate([np.cos(ang3), -np.sin(ang3)], axis=0) / math.sqrt(CTX_LEN * HEAD_DIM)
    as_bf16 = lambda a: jnp.asarray(a, dtype=F32).astype(BF16)
    return as_bf16(w1), as_bf16(m2), as_bf16(cs), as_bf16(t1), as_bf16(cs_ctx)


def _fourier_kernel(x_ref, w1_ref, m2_ref, cs_ref, *rest, with_ctx):
    if with_ctx:
        xc_ref, t1_ref, csc_ref, o_ref, oc_ref, re_scr, im_scr, y_scr = rest
        t = jnp.dot(t1_ref[...], xc_ref[...], preferred_element_type=F32)
        csc = csc_ref[...]
        for g in range(F_GROUPS):
            sl = slice(g * HEAD_DIM, (g + 1) * HEAD_DIM)
            lhs = jnp.concatenate([t[:CTX_LEN, sl], t[CTX_LEN:, sl]], axis=1).astype(BF16)
            oc_ref[:, sl] = jnp.dot(lhs, csc, preferred_element_type=F32).astype(BF16)
    else:
        o_ref, re_scr, im_scr, y_scr = rest
    n1 = GRID_W
    w1 = w1_ref[...]
    for b in range(n1):
        a = jnp.dot(w1, x_ref[:, b * F_W:(b + 1) * F_W], preferred_element_type=F32)
        for g in range(F_GROUPS):
            sl = slice(g * HEAD_DIM, (g + 1) * HEAD_DIM)
            re_scr[g, pl.ds(b, n1, stride=PITCH), :] = a[:n1, sl]
            im_scr[g, pl.ds(b, n1, stride=PITCH), :] = a[n1:, sl]

    cs = cs_ref[...]

    def column_dft(q):
        row = pl.multiple_of(q * PITCH, 8)
        a = jnp.concatenate(
            [jnp.concatenate([re_scr[g, pl.ds(row, n1), :] for g in range(F_GROUPS)], axis=1),
             jnp.concatenate([im_scr[g, pl.ds(row, n1), :] for g in range(F_GROUPS)], axis=1)], axis=0)
        return jnp.dot(m2_ref[q], a.astype(BF16), preferred_element_type=F32)

    def channel_dft(q, z):
        lhs = jnp.concatenate(
            [jnp.concatenate([z[:n1, g * HEAD_DIM:(g + 1) * HEAD_DIM],
                              z[n1:, g * HEAD_DIM:(g + 1) * HEAD_DIM]], axis=1)
             for g in range(F_GROUPS)], axis=0).astype(BF16)
        y = jnp.dot(lhs, cs, preferred_element_type=F32)
        for g in range(F_GROUPS):
            y_scr[g, pl.ds(q, n1, stride=PITCH), :] = y[g * n1:(g + 1) * n1]

    def per_group(t, carry):
        qs = [t * DFT_UNROLL + k for k in range(DFT_UNROLL)]
        zs = [column_dft(q) for q in qs]
        for q, z in zip(qs, zs):
            channel_dft(q, z)
        return carry

    lax.fori_loop(0, n1 // DFT_UNROLL, per_group, 0)
    for p in range(n1):
        for g in range(F_GROUPS):
            o_ref[p * n1:(p + 1) * n1, g * HEAD_DIM:(g + 1) * HEAD_DIM] = (
                y_scr[g, p * PITCH:p * PITCH + n1, :].astype(BF16))


def _fourier(fx, f_ctx, tables, with_ctx):
    w1, m2, cs, t1, cs_ctx = tables
    n1 = GRID_W
    const = lambda shape: pl.BlockSpec(shape, lambda b: (0,) * len(shape))
    in_specs = [pl.BlockSpec((n1, n1 * F_W), lambda b: (b, 0)), const((2 * n1, n1)),
                const((n1, 2 * n1, 2 * n1)), const((2 * HEAD_DIM, HEAD_DIM))]
    out_specs = [pl.BlockSpec((SEQ, F_W), lambda b: (b, 0))]
    out_shape = [jax.ShapeDtypeStruct((N_LAT, F_W), BF16)]
    operands = [fx, w1, m2, cs]
    if with_ctx:
        ctx0 = N_LAT // CTX_LEN
        in_specs += [pl.BlockSpec((CTX_LEN, F_W), lambda b: (ctx0 + b, 0)), const((2 * CTX_LEN, CTX_LEN)),
                     const((2 * HEAD_DIM, HEAD_DIM))]
        out_specs.append(pl.BlockSpec((CTX_LEN, F_W), lambda b: (b, 0)))
        out_shape.append(jax.ShapeDtypeStruct((N_CTX, F_W), BF16))
        operands += [f_ctx, t1, cs_ctx]
    return pl.pallas_call(
        functools.partial(_fourier_kernel, with_ctx=with_ctx),
        grid=(BATCH,),
        in_specs=in_specs,
        out_specs=out_specs,
        out_shape=out_shape,
        scratch_shapes=[pltpu.VMEM((F_GROUPS, n1 * PITCH, HEAD_DIM), F32)] * 3,
        compiler_params=_cparams(("arbitrary",)),
        name="fourier_mix",
    )(*operands)


_Z_ROWS = 2 * CHUNK


def _proj_gating_kernel(h_ref, w_ref, nw_ref, ws_ref, bs_ref, o_ref, w_scr):
    @pl.when(pl.program_id(0) == 0)
    def _():
        w_scr[...] = w_ref[...].astype(BF16)

    nw = nw_ref[...]

    def project(r):
        rows = slice(r * _Z_ROWS, (r + 1) * _Z_ROWS)
        return _gelu(jnp.dot(h_ref[rows, :], w_scr[...], preferred_element_type=F32))

    n_r = TM // _Z_ROWS
    z_next = project(0)
    for r in range(n_r):
        z = z_next
        if r + 1 < n_r:
            z_next = project(r + 1)
        for g in range(C_GROUPS):
            sl = slice(g * HEAD_DIM, (g + 1) * HEAD_DIM)
            v = z[:, C_W + g * HEAD_DIM:C_W + (g + 1) * HEAD_DIM]
            mu = jnp.mean(v, axis=-1, keepdims=True)
            vc = v - mu
            var = jnp.mean(vc * vc, axis=-1, keepdims=True)
            vn = (vc * lax.rsqrt(var + EPS) * nw[:, sl]).astype(BF16)
            n_ch = _Z_ROWS // CHUNK
            vn2 = jnp.concatenate([vn[c * CHUNK:(c + 1) * CHUNK] for c in range(n_ch)], axis=1)
            sp2 = jnp.dot(ws_ref[g], vn2, preferred_element_type=F32)
            for c in range(n_ch):
                cr = slice(c * CHUNK, (c + 1) * CHUNK)
                sp = sp2[:, c * HEAD_DIM:(c + 1) * HEAD_DIM] + bs_ref[g]
                o_ref[r * _Z_ROWS + c * CHUNK:r * _Z_ROWS + (c + 1) * CHUNK, sl] = (
                    z[cr, sl] * sp).astype(BF16)


def _projection_gating(h, w_all, layer, norm_w, ws, bs, n_rows):
    bs_exp = jnp.broadcast_to(bs[:, :, None], (C_GROUPS, CHUNK, HEAD_DIM))
    col0 = F_W + 3 * NA_W
    return pl.pallas_call(
        _proj_gating_kernel,
        grid=(n_rows // TM,),
        in_specs=[
            pl.BlockSpec((TM, D_MODEL), lambda i: (i, 0)),
            pl.BlockSpec((pl.Squeezed(), pl.Element(D_MODEL), pl.Element(2 * C_W)),
                         lambda i: (layer, 0, col0)),
            pl.BlockSpec((1, C_W), lambda i: (0, 0)),
            pl.BlockSpec((C_GROUPS, CHUNK, CHUNK), lambda i: (0, 0, 0)),
            pl.BlockSpec((C_GROUPS, CHUNK, HEAD_DIM), lambda i: (0, 0, 0)),
        ],
        out_specs=pl.BlockSpec((TM, C_W), lambda i: (i, 0)),
        out_shape=jax.ShapeDtypeStruct((n_rows, C_W), BF16),
        scratch_shapes=[pltpu.VMEM((D_MODEL, 2 * C_W), BF16)],
        compiler_params=_cparams(("arbitrary",)),
        name="in_proj_gmlp",
    )(h, w_all, norm_w.reshape(1, C_W), ws.astype(BF16), bs_exp)


def _merge_kernel(ya_ref, yb_ref, atta_ref, attb_ref, sg_ref, gate_ref, xa_ref, xb_ref, g1_ref, sh_ref,
                  sc_ref, nw_ref, wf_ref, wna_ref, wc_ref, wo_ref, o_ref, h_ref, m_scr, *, n_a_tiles):
    i = pl.program_id(0)
    r = _mod_row(i, TM_RES)
    from_a = i < n_a_tiles
    y = jnp.where(from_a, ya_ref[...], yb_ref[...])
    att = jnp.where(from_a, atta_ref[...], attb_ref[...])
    a_f = jnp.dot(y, wf_ref[...], preferred_element_type=F32)
    a_na = jnp.dot(att, wna_ref[...], preferred_element_type=F32)
    a_c = jnp.dot(sg_ref[...], wc_ref[...], preferred_element_type=F32)
    g_f = gate_ref[:, :D_MODEL].astype(F32)
    g_na = gate_ref[:, D_MODEL:2 * D_MODEL].astype(F32)
    g_c = gate_ref[:, 2 * D_MODEL:].astype(F32)
    m_scr[...] = (g_f * a_f + g_na * a_na + g_c * a_c).astype(BF16)
    o = jnp.dot(m_scr[...], wo_ref[...], preferred_element_type=F32)
    x = jnp.where(from_a, xa_ref[...], xb_ref[...])
    xn = x + g1_ref[pl.ds(r, 1), :] * o
    o_ref[...] = xn
    ms = jnp.mean(xn * xn, axis=-1, keepdims=True)
    wmod = nw_ref[...] * (1.0 + sc_ref[pl.ds(r, 1), :])
    h_ref[...] = (xn * lax.rsqrt(ms + EPS) * wmod + sh_ref[pl.ds(r, 1), :]).astype(BF16)


def _merge(ya, yb, atta, attb, sg, gate, xa, xb, mod, norm2_w, wf, wna, wc, wo, layer, n_rows):
    tm = TM_RES
    na = N_LAT // tm
    nbt = max((n_rows - N_LAT) // tm, 1)
    a_spec = lambda w: pl.BlockSpec((tm, w), lambda i: (jnp.minimum(i, na - 1), 0))
    b_spec = lambda w: pl.BlockSpec((tm, w), lambda i: (jnp.clip(i - na, 0, nbt - 1), 0))
    return pl.pallas_call(
        functools.partial(_merge_kernel, n_a_tiles=na),
        grid=(n_rows // tm,),
        in_specs=[
            a_spec(F_W), b_spec(F_W), a_spec(NA_W), b_spec(NA_W),
            pl.BlockSpec((tm, C_W), lambda i: (i, 0)),
            pl.BlockSpec((tm, 3 * D_MODEL), lambda i: (i, 0)),
            a_spec(D_MODEL), b_spec(D_MODEL),
            pl.BlockSpec((MOD_ROWS, D_MODEL), lambda i: (0, 2)),
            pl.BlockSpec((MOD_ROWS, D_MODEL), lambda i: (0, 3)),
            pl.BlockSpec((MOD_ROWS, D_MODEL), lambda i: (0, 4)),
            pl.BlockSpec((1, D_MODEL), lambda i: (0, 0)),
            _resident((F_W, D_MODEL), layer),
            _resident((NA_W, D_MODEL), layer),
            _resident((C_W, D_MODEL), layer),
            _resident((D_MODEL, D_MODEL), layer),
        ],
        out_specs=[pl.BlockSpec((tm, D_MODEL), lambda i: (i, 0)),
                   pl.BlockSpec((tm, D_MODEL), lambda i: (i, 0))],
        out_shape=[jax.ShapeDtypeStruct((n_rows, D_MODEL), F32),
                   jax.ShapeDtypeStruct((n_rows, D_MODEL), BF16)],
        scratch_shapes=[pltpu.VMEM((tm, D_MODEL), BF16)],
        compiler_params=_cparams(("arbitrary",)),
        name="gated_merge",
    )(ya, yb, atta, attb, sg, gate, xa, xb, mod, mod, mod, norm2_w.reshape(1, D_MODEL), wf, wna, wc, wo)


_FF_TILES = D_FF // TN


def _ffn_up_kernel(h_ref, hp_ref, hn_ref, wa_ref, wg_ref, cw_ref, cb_ref, o_ref, wa_scr, wg_scr, a_scr):
    i = pl.program_id(1)
    tp = h_ref.shape[0]
    rc = tp // N_CHUNKS

    @pl.when(i == 0)
    def _():
        wa_scr[...] = wa_ref[...].astype(BF16)
        wg_scr[...] = wg_ref[...].astype(BF16)

    n_chunks = N_CHUNKS
    for c in range(n_chunks):
        parts = [h_ref[c * rc:(c + 1) * rc, :]]
        lo, hi = HALO + c * rc, HALO + (c + 1) * rc
        if c == 0:
            parts.insert(0, hp_ref[...])
            lo -= HALO
        if c == n_chunks - 1:
            parts.append(hn_ref[...])
            hi += HALO
        lhs = parts[0] if len(parts) == 1 else jnp.concatenate(parts, axis=0)
        a_scr[lo:hi, :] = jnp.dot(lhs, wa_scr[...], preferred_element_type=F32)

    w0, w1, w2, cb = cw_ref[0:1], cw_ref[1:2], cw_ref[2:3], cb_ref[...]
    gate_proj = lambda c: jnp.dot(h_ref[c * rc:(c + 1) * rc, :], wg_scr[...], preferred_element_type=F32)
    gv_nxt = gate_proj(0)
    for c in range(n_chunks):
        r0 = HALO + c * rc
        gv = gv_nxt
        if c + 1 < n_chunks:
            gv_nxt = gate_proj(c + 1)
        row = i * tp + c * rc + lax.broadcasted_iota(jnp.int32, (rc, TN), 0)
        seq_mask = jnp.where(row < N_LAT, SEQ - 1, CTX_LEN - 1)
        pos = row & seq_mask
        prev = jnp.where(pos != 0, a_scr[r0 - 1:r0 - 1 + rc, :], 0.0)
        nxt = jnp.where(pos != seq_mask, a_scr[r0 + 1:r0 + 1 + rc, :], 0.0)
        a = cb + w0 * prev + w1 * a_scr[r0:r0 + rc, :] + w2 * nxt
        o_ref[c * rc:(c + 1) * rc, :] = (a * jax.nn.sigmoid(a) * gv).astype(BF16)


def _ffn_up(h2, w_up_all, layer, conv_w, conv_b, n_rows):
    tp = _proj_tile(n_rows)
    hb = tp // HALO
    last = h2.shape[0] // HALO - 1

    def wspec(off):
        return pl.BlockSpec((pl.Squeezed(), D_MODEL, TN), lambda j, i: (layer, 0, off + j))

    return pl.pallas_call(
        _ffn_up_kernel,
        grid=(_FF_TILES, n_rows // tp),
        in_specs=[
            pl.BlockSpec((tp, D_MODEL), lambda j, i: (i, 0)),
            pl.BlockSpec((HALO, D_MODEL), lambda j, i: (jnp.maximum(i * hb - 1, 0), 0)),
            pl.BlockSpec((HALO, D_MODEL), lambda j, i: (jnp.minimum((i + 1) * hb, last), 0)),
            wspec(0),
            wspec(_FF_TILES),
            pl.BlockSpec((CONV_W, TN), lambda j, i: (0, j)),
            pl.BlockSpec((1, TN), lambda j, i: (0, j)),
        ],
        out_specs=pl.BlockSpec((tp, TN), lambda j, i: (i, j)),
        out_shape=jax.ShapeDtypeStruct((n_rows, D_FF), BF16),
        scratch_shapes=[pltpu.VMEM((D_MODEL, TN), BF16), pltpu.VMEM((D_MODEL, TN), BF16),
                        pltpu.VMEM((tp + 2 * HALO, TN), F32)],
        compiler_params=_cparams(("arbitrary", "arbitrary")),
        name="ffn_up_conv",
    )(h2, h2, h2, w_up_all, w_up_all, conv_w, conv_b.reshape(1, D_FF))


def _ffn_down_kernel(a_ref, w_ref, x_ref, g2_ref, sh_ref, sc_ref, nw_ref, *refs, final):
    tm = a_ref.shape[0]
    r = _mod_row(pl.program_id(0), tm)
    xs_ref = refs[1] if final else refs[0]
    act = a_ref[...]
    ss = jnp.zeros((tm, 1), F32)
    n_c = D_MODEL // TN
    down = lambda c: jnp.dot(act, w_ref[:, c * TN:(c + 1) * TN], preferred_element_type=F32)
    o_nxt = down(0)
    for c in range(n_c):
        sl = slice(c * TN, (c + 1) * TN)
        o = o_nxt
        if c + 1 < n_c:
            o_nxt = down(c + 1)
        xn = x_ref[:, sl] + g2_ref[pl.ds(r, 1), sl] * o
        xs_ref[:, sl] = xn
        ss = ss + jnp.sum(xn * xn, axis=-1, keepdims=True)
    inv = lax.rsqrt(ss * (1.0 / D_MODEL) + EPS)
    if final:
        refs[0][...] = xs_ref[...] * inv * nw_ref[...]
    else:
        wmod = nw_ref[...] * (1.0 + sc_ref[pl.ds(r, 1), :])
        refs[1][...] = (xs_ref[...] * inv * wmod + sh_ref[pl.ds(r, 1), :]).astype(BF16)


def _ffn_down(act, w_down, layer, x_all, mod, mod_next, norm_w, n_rows, final):
    tm = TM_DOWN
    assert n_rows % tm == 0 and N_LAT % tm == 0
    row = lambda i: (i, 0)
    if final:
        out_specs = pl.BlockSpec((tm, D_MODEL), row)
        out_shape = jax.ShapeDtypeStruct((n_rows, D_MODEL), F32)
        scratch = [pltpu.VMEM((tm, D_MODEL), F32)]
    else:
        out_specs = [pl.BlockSpec((tm, D_MODEL), row), pl.BlockSpec((tm, D_MODEL), row)]
        out_shape = [jax.ShapeDtypeStruct((n_rows, D_MODEL), F32),
                     jax.ShapeDtypeStruct((n_rows, D_MODEL), BF16)]
        scratch = []
    return pl.pallas_call(
        functools.partial(_ffn_down_kernel, final=final),
        grid=(n_rows // tm,),
        in_specs=[
            pl.BlockSpec((tm, D_FF), row),
            _resident((D_FF, D_MODEL), layer),
            pl.BlockSpec((tm, D_MODEL), row),
            pl.BlockSpec((MOD_ROWS, D_MODEL), lambda i: (0, 5)),
            pl.BlockSpec((MOD_ROWS, D_MODEL), lambda i: (0, 0)),
            pl.BlockSpec((MOD_ROWS, D_MODEL), lambda i: (0, 1)),
            pl.BlockSpec((1, D_MODEL), lambda i: (0, 0)),
        ],
        out_specs=out_specs,
        out_shape=out_shape,
        scratch_shapes=scratch,
        compiler_params=pltpu.CompilerParams(dimension_semantics=("arbitrary",),
                                             vmem_limit_bytes=60 * 1024 * 1024),
        name="ffn_down",
    )(act, w_down, x_all, mod, mod_next, mod_next, norm_w.reshape(1, D_MODEL))


def kernel(x, c, ctx, c_ctx, ada_w, ada_b, norm1_w, norm2_w, w_in, na_rpb, gmlp_norm_w, gmlp_ws,
           gmlp_bs, w_f_out, w_na_out, w_c_out, w_o, ffn_up, ffn_conv_w, ffn_conv_b, ffn_down,
           final_norm_w):
    assert x.shape == (BATCH, SEQ, D_MODEL) and ctx.shape == (BATCH, CTX_LEN, D_MODEL)
    cvec = jnp.concatenate(
        [c, c_ctx[None, :], jnp.zeros((MOD_ROWS - BATCH - 1, D_MODEL), F32)], axis=0)
    mod_all = _modulation(cvec, ada_w, ada_b)
    tables = _dft_tables()
    merge_w = tuple(w.astype(BF16) for w in (w_f_out, w_na_out, w_c_out, w_o))
    w_down = ffn_down.astype(BF16)
    xa = x.reshape(N_LAT, D_MODEL)
    xb = ctx.reshape(N_CTX, D_MODEL)
    h1 = _prenorm(xa, xb, mod_all[0], norm1_w[0])

    for l in range(DEPTH):
        last = l == DEPTH - 1
        n_rows = N_LAT if last else N_TOK
        mod = mod_all[l]
        fs = _projection_f(h1, w_in, l, n_rows, not last)
        qkv = _projection(h1, w_in, l, F_W, 3 * NA_W, 1024, _ident, "in_proj_qkv", N_TOK)
        sg = _projection_gating(h1, w_in, l, gmlp_norm_w[l], gmlp_ws[l], gmlp_bs[l], n_rows)
        gate = _projection(h1, w_in, l, F_W + 3 * NA_W + 2 * C_W, 3 * D_MODEL, 1024, jax.nn.sigmoid,
                           "in_proj_gate", n_rows)
        atts = _attention(qkv, _bias_table(na_rpb[l]), not last)
        ys = _fourier(fs[0], fs[-1], tables, not last)
        x_mid, h2 = _merge(ys[0], ys[-1], atts[0], atts[-1], sg, gate, xa, xb, mod, norm2_w[l],
                           *merge_w, l, n_rows)
        act = _ffn_up(h2, ffn_up, l, ffn_conv_w[l], ffn_conv_b[l], n_rows)
        if last:
            out = _ffn_down(act, w_down, l, x_mid, mod, mod, final_norm_w, n_rows, True)
        else:
            xa, h1 = _ffn_down(act, w_down, l, x_mid, mod, mod_all[l + 1], norm1_w[l + 1], n_rows, False)
            xb = xa
    return out.reshape(BATCH, SEQ, D_MODEL)
```

```python
import functools
import math

import numpy as np
import jax
import jax.numpy as jnp
from jax import lax
from jax.experimental import pallas as pl
from jax.experimental.pallas import tpu as pltpu

F32 = jnp.float32
BF16 = jnp.bfloat16

D_MODEL = 2048
BATCH = 4
SEQ = 4096
DEPTH = 2
GRID_W = 64
ROWS = SEQ // GRID_W
CTX_LEN = 256
HEAD_DIM = 128
F_GROUPS = 4
F_W = F_GROUPS * HEAD_DIM
NA_HEADS = 8
NA_W = NA_HEADS * HEAD_DIM
NA_KH = 8
NA_KW = 16
C_GROUPS = 4
C_W = C_GROUPS * HEAD_DIM
CHUNK = 128
D_FF = 5632
CONV_W = 3
EPS = 1e-6
NEG_INF = -1e30
IN_W = F_W + 3 * NA_W + 2 * C_W + 3 * D_MODEL

N_LAT = BATCH * SEQ
N_CTX = BATCH * CTX_LEN
N_TOK = N_LAT + N_CTX
MOD_ROWS = 8

TM = 1024
TN = 512
N_CHUNKS = 8
TM_RES = 256
TM_DOWN = 512
HALO = 16
QROWS = 4
KROWS = 12
ATTN_UNROLL = 4
DFT_UNROLL = 8
PITCH = 72
VMEM_LIMIT = 56 * 1024 * 1024


def _cparams(sem):
    return pltpu.CompilerParams(dimension_semantics=sem, vmem_limit_bytes=VMEM_LIMIT)


def _mod_row(i, tm):
    return jnp.minimum((i * tm) // SEQ, BATCH)


def _norm_mod(x, nw, sh, sc):
    ms = jnp.mean(x * x, axis=-1, keepdims=True)
    return (x * lax.rsqrt(ms + EPS) * nw) * (1.0 + sc) + sh


def _rms(x, nw):
    ms = jnp.mean(x * x, axis=-1, keepdims=True)
    return x * lax.rsqrt(ms + EPS) * nw


def _resident(shape, layer):
    return pl.BlockSpec((pl.Squeezed(),) + shape, lambda *_: (layer,) + (0,) * len(shape),
                        pipeline_mode=pl.Buffered(1))


def _mod_kernel(c_ref, w_ref, b_ref, o_ref):
    c = c_ref[...]
    s = (c * jax.nn.sigmoid(c)).astype(BF16)
    o_ref[0] = jnp.dot(s, w_ref[0].astype(BF16), preferred_element_type=F32) + b_ref[0]


def _modulation(cvec, ada_w, ada_b):
    tn = 1024
    return pl.pallas_call(
        _mod_kernel,
        grid=(DEPTH, 6 * D_MODEL // tn),
        in_specs=[
            pl.BlockSpec((MOD_ROWS, D_MODEL), lambda l, j: (0, 0)),
            pl.BlockSpec((1, D_MODEL, tn), lambda l, j: (l, 0, j)),
            pl.BlockSpec((1, 1, tn), lambda l, j: (l, 0, j)),
        ],
        out_specs=pl.BlockSpec((1, MOD_ROWS, tn), lambda l, j: (l, 0, j)),
        out_shape=jax.ShapeDtypeStruct((DEPTH, MOD_ROWS, 6 * D_MODEL), F32),
        compiler_params=_cparams(("arbitrary", "arbitrary")),
        name="modulation",
    )(cvec, ada_w, ada_b.reshape(DEPTH, 1, 6 * D_MODEL))


def _prenorm_kernel(xa_ref, xb_ref, sh_ref, sc_ref, nw_ref, o_ref, *, n_a_tiles):
    i = pl.program_id(0)
    r = _mod_row(i, TM)
    x = jnp.where(i < n_a_tiles, xa_ref[...], xb_ref[...])
    h = _norm_mod(x, nw_ref[...], sh_ref[pl.ds(r, 1), :], sc_ref[pl.ds(r, 1), :])
    o_ref[...] = h.astype(BF16)


def _prenorm(x2d, ctx2d, mod, norm_w):
    na = N_LAT // TM
    return pl.pallas_call(
        functools.partial(_prenorm_kernel, n_a_tiles=na),
        grid=(N_TOK // TM,),
        in_specs=[
            pl.BlockSpec((TM, D_MODEL), lambda i: (jnp.minimum(i, na - 1), 0)),
            pl.BlockSpec((TM, D_MODEL), lambda i: (jnp.maximum(i - na, 0), 0)),
            pl.BlockSpec((MOD_ROWS, D_MODEL), lambda i: (0, 0)),
            pl.BlockSpec((MOD_ROWS, D_MODEL), lambda i: (0, 1)),
            pl.BlockSpec((1, D_MODEL), lambda i: (0, 0)),
        ],
        out_specs=pl.BlockSpec((TM, D_MODEL), lambda i: (i, 0)),
        out_shape=jax.ShapeDtypeStruct((N_TOK, D_MODEL), BF16),
        compiler_params=_cparams(("arbitrary",)),
        name="prenorm",
    )(x2d, ctx2d, mod, mod, norm_w.reshape(1, D_MODEL))


def _ident(a):
    return a


def _gelu(a):
    return 0.5 * a * (1.0 + lax.erf(a * math.sqrt(0.5)))


def _proj_tile(n_rows):
    tp = n_rows // 8
    assert tp * 8 == n_rows and tp % (N_CHUNKS * HALO) == 0
    return tp


def _proj_kernel(h_ref, w_ref, *rest, epilogue, side):
    if side:
        side_in_ref, o_ref, side_out_ref, w_scr = rest
        side_out_ref[...] = side_in_ref[...].astype(BF16)
    else:
        o_ref, w_scr = rest

    @pl.when(pl.program_id(1) == 0)
    def _():
        w_scr[...] = w_ref[...].astype(BF16)

    rc = h_ref.shape[0] // N_CHUNKS
    for c in range(N_CHUNKS):
        rows = slice(c * rc, (c + 1) * rc)
        acc = jnp.dot(h_ref[rows, :], w_scr[...], preferred_element_type=F32)
        o_ref[rows, :] = epilogue(acc).astype(BF16)


def _projection(h, w_all, layer, col0, n_cols, tn, epilogue, name, n_rows, side=None):
    tp = _proj_tile(n_rows)
    n_i = n_rows // tp
    in_specs = [
        pl.BlockSpec((tp, D_MODEL), lambda j, i: (i, 0)),
        pl.BlockSpec((pl.Squeezed(), pl.Element(D_MODEL), pl.Element(tn)),
                     lambda j, i: (layer, 0, pl.multiple_of(col0 + j * tn, 128))),
    ]
    out_specs = [pl.BlockSpec((tp, tn), lambda j, i: (i, j))]
    out_shape = [jax.ShapeDtypeStruct((n_rows, n_cols), BF16)]
    operands = [h, w_all]
    if side is not None:
        w_side, slab = side
        _, rows, cols = w_side.shape
        n_slabs = rows // slab
        assert n_slabs * slab == rows and n_slabs <= (n_cols // tn) * n_i
        slab_of = lambda j, i: jnp.minimum(j * n_i + i, n_slabs - 1)
        in_specs.append(pl.BlockSpec((pl.Squeezed(), slab, cols), lambda j, i: (layer, slab_of(j, i), 0)))
        out_specs.append(pl.BlockSpec((slab, cols), lambda j, i: (slab_of(j, i), 0)))
        out_shape.append(jax.ShapeDtypeStruct((rows, cols), BF16))
        operands.append(w_side)
    outs = pl.pallas_call(
        functools.partial(_proj_kernel, epilogue=epilogue, side=side is not None),
        grid=(n_cols // tn, n_i),
        in_specs=in_specs,
        out_specs=out_specs,
        out_shape=out_shape,
        scratch_shapes=[pltpu.VMEM((D_MODEL, tn), BF16)],
        compiler_params=_cparams(("arbitrary", "arbitrary")),
        name=name,
    )(*operands)
    return outs if side is not None else outs[0]


def _proj_f_kernel(h_ref, w_ref, *rest, with_tok):
    if with_tok:
        og_ref, ot_ref, w_scr, t_scr = rest
    else:
        og_ref, w_scr, t_scr = rest

    @pl.when(pl.program_id(0) == 0)
    def _():
        w_scr[...] = w_ref[...].astype(BF16)

    n1 = GRID_W
    acc = jnp.dot(h_ref[...], w_scr[...], preferred_element_type=F32)
    if with_tok:
        ot_ref[...] = acc.astype(BF16)
    for a in range(TM // n1):
        for g in range(F_GROUPS):
            t_scr[g, a * PITCH:a * PITCH + n1, :] = acc[a * n1:(a + 1) * n1, g * HEAD_DIM:(g + 1) * HEAD_DIM]
    for b in range(n1):
        for g in range(F_GROUPS):
            col = b * F_W + g * HEAD_DIM
            og_ref[:, col:col + HEAD_DIM] = t_scr[g, pl.ds(b, TM // n1, stride=PITCH), :].astype(BF16)


def _projection_f(h, w_all, layer, n_rows, with_tok):
    n1 = GRID_W
    out_specs = [pl.BlockSpec((TM // n1, n1 * F_W), lambda i: (i, 0))]
    out_shape = [jax.ShapeDtypeStruct((n_rows // n1, n1 * F_W), BF16)]
    if with_tok:
        out_specs.append(pl.BlockSpec((TM, F_W), lambda i: (i, 0)))
        out_shape.append(jax.ShapeDtypeStruct((n_rows, F_W), BF16))
    return pl.pallas_call(
        functools.partial(_proj_f_kernel, with_tok=with_tok),
        grid=(n_rows // TM,),
        in_specs=[
            pl.BlockSpec((TM, D_MODEL), lambda i: (i, 0)),
            pl.BlockSpec((pl.Squeezed(), D_MODEL, F_W), lambda i: (layer, 0, 0)),
        ],
        out_specs=out_specs,
        out_shape=out_shape,
        scratch_shapes=[pltpu.VMEM((D_MODEL, F_W), BF16),
                        pltpu.VMEM((F_GROUPS, (TM // n1) * PITCH, HEAD_DIM), F32)],
        compiler_params=_cparams(("arbitrary",)),
        name="in_proj_f",
    )(h, w_all)


_N_QBLK = ROWS // QROWS
_BIAS_VARIANTS = (0, 1, _N_QBLK - 1)


def _key_row0(nb):
    return min(max(QROWS * nb - NA_KH // 2, 0), ROWS - KROWS)


def _bias_kernel(rpb_ref, o_ref, pair_scr):
    h = pl.program_id(0)
    kc = lax.broadcasted_iota(jnp.int32, (GRID_W, 2 * GRID_W), 0)
    lane = lax.broadcasted_iota(jnp.int32, (GRID_W, 2 * GRID_W), 1)
    qc = lane & (GRID_W - 1)
    low = lane < GRID_W
    dc = jnp.clip(kc - qc + NA_KW - 1, 0, 2 * NA_KW - 2)
    c_start = jnp.clip(qc - NA_KW // 2, 0, GRID_W - NA_KW)
    col_in = (kc >= c_start) & (kc < c_start + NA_KW)
    n_dr = 2 * NA_KH - 1
    for dr in range(n_dr + 1):
        acc = jnp.zeros((GRID_W, 2 * GRID_W), F32)
        for d in range(2 * NA_KW - 1):
            lo = rpb_ref[h, dr, d] if 0 <= dr < n_dr else 0.0
            hi = rpb_ref[h, dr - 1, d] if 0 <= dr - 1 < n_dr else 0.0
            acc = jnp.where(dc == d, jnp.where(low, lo, hi), acc)
        pair_scr[dr] = acc * _LOG2E
    for vi, nb in enumerate(_BIAS_VARIANTS):
        r0 = _key_row0(nb)
        for t in range(KROWS):
            kr = r0 + t
            for ip in range(QROWS // 2):
                ok = []
                for i in (2 * ip, 2 * ip + 1):
                    r_start = min(max(QROWS * nb + i - NA_KH // 2, 0), ROWS - NA_KH)
                    ok.append(r_start <= kr < r_start + NA_KH)
                dr = kr - (QROWS * nb + 2 * ip) + NA_KH - 1
                if not (ok[0] or ok[1]):
                    tile = jnp.full((GRID_W, 2 * GRID_W), NEG_INF, F32)
                else:
                    valid = col_in
                    if not ok[0]:
                        valid = valid & jnp.logical_not(low)
                    if not ok[1]:
                        valid = valid & low
                    tile = jnp.where(valid, pair_scr[dr], NEG_INF)
                o_ref[vi, 0, t * GRID_W:(t + 1) * GRID_W, ip * 2 * GRID_W:(ip + 1) * 2 * GRID_W] = tile


def _bias_table(rpb):
    nv = len(_BIAS_VARIANTS)
    return pl.pallas_call(
        _bias_kernel,
        grid=(NA_HEADS,),
        in_specs=[pl.BlockSpec(memory_space=pltpu.SMEM)],
        out_specs=pl.BlockSpec((nv, 1, KROWS * GRID_W, QROWS * GRID_W), lambda h: (0, h, 0, 0)),
        out_shape=jax.ShapeDtypeStruct((nv, NA_HEADS, KROWS * GRID_W, QROWS * GRID_W), F32),
        scratch_shapes=[pltpu.VMEM((2 * NA_KH, GRID_W, 2 * GRID_W), F32)],
        compiler_params=_cparams(("arbitrary",)),
        name="attn_bias_table",
    )(rpb)


_QK_DIMS = (((1,), (1,)), ((), ()))
_LOG2E = math.log2(math.e)
_SCALE = HEAD_DIM ** -0.5 * _LOG2E
_QB = QROWS * GRID_W


def _attn_kernel(q_ref, k_ref, v_ref, kc_ref, vc_ref, bias_ref, *rest, with_ctx):
    if with_ctx:
        qc_ref, o_ref, oc_ref, sw_scr, sc_scr, vt_scr, vct_scr = rest
    else:
        o_ref, sw_scr, sc_scr, vt_scr, vct_scr = rest
    kc = kc_ref[...]
    vc = vc_ref[...]
    pair = 2 * GRID_W

    def transposed(x):
        return x.astype(F32).T.astype(BF16)

    for j in range(SEQ // pair):
        vt_scr[j] = transposed(v_ref[j * pair:(j + 1) * pair, :])
    for j in range(CTX_LEN // pair):
        vct_scr[:, j * pair:(j + 1) * pair] = transposed(vc[j * pair:(j + 1) * pair, :])

    def key_row0(nb):
        return jnp.clip(QROWS * nb - NA_KH // 2, 0, ROWS - KROWS)

    def scores(nb, slot):
        qrow = pl.multiple_of(nb * _QB, _QB)
        variant = jnp.where(nb == 0, 0, jnp.where(nb == _N_QBLK - 1, 2, 1))
        q = (q_ref[pl.ds(qrow, _QB), :].astype(F32) * _SCALE).astype(BF16)
        kw = k_ref[pl.ds(pl.multiple_of(key_row0(nb) * GRID_W, GRID_W), KROWS * GRID_W), :]
        sw_scr[slot] = lax.dot_general(kw, q, _QK_DIMS, preferred_element_type=F32) + bias_ref[variant, 0]
        sc_scr[slot] = lax.dot_general(kc, q, _QK_DIMS, preferred_element_type=F32)

    def finish(nb, slot):
        qrow = pl.multiple_of(nb * _QB, _QB)
        j0 = key_row0(nb) // 2
        s_w = sw_scr[slot]
        s_c = sc_scr[slot]
        m = jnp.maximum(jnp.max(s_w, axis=0, keepdims=True), jnp.max(s_c, axis=0, keepdims=True))
        p_w = jnp.exp2(s_w - m)
        p_c = jnp.exp2(s_c - m)
        denom = jnp.sum(p_w, axis=0, keepdims=True) + jnp.sum(p_c, axis=0, keepdims=True)
        vt_win = jnp.concatenate([vt_scr[j0 + t] for t in range(KROWS // 2)], axis=1)
        o_t = (jnp.dot(vt_win, p_w.astype(BF16), preferred_element_type=F32)
               + jnp.dot(vct_scr[...], p_c.astype(BF16), preferred_element_type=F32))
        o_ref[pl.ds(qrow, _QB), :] = (o_t / denom).T.astype(BF16)

    if with_ctx:
        q = (qc_ref[...].astype(F32) * _SCALE).astype(BF16)
        s = lax.dot_general(q, kc, _QK_DIMS, preferred_element_type=F32)
        p = jnp.exp2(s - jnp.max(s, axis=-1, keepdims=True))
        o = jnp.dot(p.astype(BF16), vc, preferred_element_type=F32)
        oc_ref[...] = (o / jnp.sum(p, axis=-1, keepdims=True)).astype(BF16)

    scores(0, 0)

    def pair_of_blocks(t, carry):
        nb = 2 * t
        scores(nb + 1, 1)
        finish(nb, 0)
        scores(jnp.minimum(nb + 2, _N_QBLK - 1), 0)
        finish(nb + 1, 1)
        return carry

    lax.fori_loop(0, _N_QBLK // 2, pair_of_blocks, 0, unroll=ATTN_UNROLL)


def _attention(qkv, bias, with_ctx):
    ctx0 = N_LAT // CTX_LEN
    nv = len(_BIAS_VARIANTS)
    in_specs = [
        pl.BlockSpec((SEQ, HEAD_DIM), lambda b, h: (b, h)),
        pl.BlockSpec((SEQ, HEAD_DIM), lambda b, h: (b, NA_HEADS + h)),
        pl.BlockSpec((SEQ, HEAD_DIM), lambda b, h: (b, 2 * NA_HEADS + h)),
        pl.BlockSpec((CTX_LEN, HEAD_DIM), lambda b, h: (ctx0 + b, NA_HEADS + h)),
        pl.BlockSpec((CTX_LEN, HEAD_DIM), lambda b, h: (ctx0 + b, 2 * NA_HEADS + h)),
        pl.BlockSpec((nv, 1, KROWS * GRID_W, _QB), lambda b, h: (0, h, 0, 0)),
    ]
    out_specs = [pl.BlockSpec((SEQ, HEAD_DIM), lambda b, h: (b, h))]
    out_shape = [jax.ShapeDtypeStruct((N_LAT, NA_W), BF16)]
    operands = [qkv, qkv, qkv, qkv, qkv, bias]
    if with_ctx:
        in_specs.append(pl.BlockSpec((CTX_LEN, HEAD_DIM), lambda b, h: (ctx0 + b, h)))
        out_specs.append(pl.BlockSpec((CTX_LEN, HEAD_DIM), lambda b, h: (b, h)))
        out_shape.append(jax.ShapeDtypeStruct((N_CTX, NA_W), BF16))
        operands.append(qkv)
    return pl.pallas_call(
        functools.partial(_attn_kernel, with_ctx=with_ctx),
        grid=(BATCH, NA_HEADS),
        in_specs=in_specs,
        out_specs=out_specs,
        out_shape=out_shape,
        scratch_shapes=[pltpu.VMEM((2, KROWS * GRID_W, _QB), F32), pltpu.VMEM((2, CTX_LEN, _QB), F32),
                        pltpu.VMEM((SEQ // (2 * GRID_W), HEAD_DIM, 2 * GRID_W), BF16),
                        pltpu.VMEM((HEAD_DIM, CTX_LEN), BF16)],
        compiler_params=_cparams(("arbitrary", "arbitrary")),
        name="neighborhood_attention",
    )(*operands)


def _dft_tables():
    n1 = GRID_W
    idx = np.arange(n1)
    ang1 = 2 * np.pi * np.outer(idx, idx) / n1
    w1 = np.concatenate([np.cos(ang1), -np.sin(ang1)], axis=0)
    q = idx[:, None, None]
    p = idx[None, :, None]
    b = idx[None, None, :]
    ang2 = 2 * np.pi * b * (n1 * p + q) / SEQ
    wc, ws = np.cos(ang2), np.sin(ang2)
    m2 = np.concatenate([np.concatenate([wc, ws], axis=2),
                         np.concatenate([-ws, wc], axis=2)], axis=1)
    d = np.arange(HEAD_DIM)
    ang3 = 2 * np.pi * np.outer(d, d) / HEAD_DIM
    cs = np.concatenate([np.cos(ang3), np.sin(ang3)], axis=0) / math.sqrt(SEQ * HEAD_DIM)
    n = np.arange(CTX_LEN)
    angc = 2 * np.pi * np.outer(n, n) / CTX_LEN
    t1 = np.concatenate([np.cos(angc), np.sin(angc)], axis=0)
    cs_ctx = np.concatenate([np.cos(ang3), -np.sin(ang3)], axis=0) / math.sqrt(CTX_LEN * HEAD_DIM)
    as_bf16 = lambda a: jnp.asarray(a, dtype=F32).astype(BF16)
    return as_bf16(w1), as_bf16(m2), as_bf16(cs), as_bf16(t1), as_bf16(cs_ctx)


def _fourier_kernel(x_ref, w1_ref, m2_ref, cs_ref, *rest, with_ctx):
    if with_ctx:
        xc_ref, t1_ref, csc_ref, o_ref, oc_ref, re_scr, im_scr, y_scr = rest
        t = jnp.dot(t1_ref[...], xc_ref[...], preferred_element_type=F32)
        csc = csc_ref[...]
        for g in range(F_GROUPS):
            sl = slice(g * HEAD_DIM, (g + 1) * HEAD_DIM)
            lhs = jnp.concatenate([t[:CTX_LEN, sl], t[CTX_LEN:, sl]], axis=1).astype(BF16)
            oc_ref[:, sl] = jnp.dot(lhs, csc, preferred_element_type=F32).astype(BF16)
    else:
        o_ref, re_scr, im_scr, y_scr = rest
    n1 = GRID_W
    w1 = w1_ref[...]
    for b in range(n1):
        a = jnp.dot(w1, x_ref[:, b * F_W:(b + 1) * F_W], preferred_element_type=F32)
        for g in range(F_GROUPS):
            sl = slice(g * HEAD_DIM, (g + 1) * HEAD_DIM)
            re_scr[g, pl.ds(b, n1, stride=PITCH), :] = a[:n1, sl]
            im_scr[g, pl.ds(b, n1, stride=PITCH), :] = a[n1:, sl]

    cs = cs_ref[...]

    def column_dft(q):
        row = pl.multiple_of(q * PITCH, 8)
        a = jnp.concatenate(
            [jnp.concatenate([re_scr[g, pl.ds(row, n1), :] for g in range(F_GROUPS)], axis=1),
             jnp.concatenate([im_scr[g, pl.ds(row, n1), :] for g in range(F_GROUPS)], axis=1)], axis=0)
        return jnp.dot(m2_ref[q], a.astype(BF16), preferred_element_type=F32)

    def channel_dft(q, z):
        lhs = jnp.concatenate(
            [jnp.concatenate([z[:n1, g * HEAD_DIM:(g + 1) * HEAD_DIM],
                              z[n1:, g * HEAD_DIM:(g + 1) * HEAD_DIM]], axis=1)
             for g in range(F_GROUPS)], axis=0).astype(BF16)
        y = jnp.dot(lhs, cs, preferred_element_type=F32)
        for g in range(F_GROUPS):
            y_scr[g, pl.ds(q, n1, stride=PITCH), :] = y[g * n1:(g + 1) * n1]

    def per_group(t, carry):
        qs = [t * DFT_UNROLL + k for k in range(DFT_UNROLL)]
        zs = [column_dft(q) for q in qs]
        for q, z in zip(qs, zs):
            channel_dft(q, z)
        return carry

    lax.fori_loop(0, n1 // DFT_UNROLL, per_group, 0)
    for p in range(n1):
        for g in range(F_GROUPS):
            o_ref[p * n1:(p + 1) * n1, g * HEAD_DIM:(g + 1) * HEAD_DIM] = (
                y_scr[g, p * PITCH:p * PITCH + n1, :].astype(BF16))


def _fourier(fx, f_ctx, tables, with_ctx):
    w1, m2, cs, t1, cs_ctx = tables
    n1 = GRID_W
    const = lambda shape: pl.BlockSpec(shape, lambda b: (0,) * len(shape))
    in_specs = [pl.BlockSpec((n1, n1 * F_W), lambda b: (b, 0)), const((2 * n1, n1)),
                const((n1, 2 * n1, 2 * n1)), const((2 * HEAD_DIM, HEAD_DIM))]
    out_specs = [pl.BlockSpec((SEQ, F_W), lambda b: (b, 0))]
    out_shape = [jax.ShapeDtypeStruct((N_LAT, F_W), BF16)]
    operands = [fx, w1, m2, cs]
    if with_ctx:
        ctx0 = N_LAT // CTX_LEN
        in_specs += [pl.BlockSpec((CTX_LEN, F_W), lambda b: (ctx0 + b, 0)), const((2 * CTX_LEN, CTX_LEN)),
                     const((2 * HEAD_DIM, HEAD_DIM))]
        out_specs.append(pl.BlockSpec((CTX_LEN, F_W), lambda b: (b, 0)))
        out_shape.append(jax.ShapeDtypeStruct((N_CTX, F_W), BF16))
        operands += [f_ctx, t1, cs_ctx]
    return pl.pallas_call(
        functools.partial(_fourier_kernel, with_ctx=with_ctx),
        grid=(BATCH,),
        in_specs=in_specs,
        out_specs=out_specs,
        out_shape=out_shape,
        scratch_shapes=[pltpu.VMEM((F_GROUPS, n1 * PITCH, HEAD_DIM), F32)] * 3,
        compiler_params=_cparams(("arbitrary",)),
        name="fourier_mix",
    )(*operands)


_Z_ROWS = 2 * CHUNK


def _proj_gating_kernel(h_ref, w_ref, nw_ref, ws_ref, bs_ref, o_ref, w_scr):
    @pl.when(pl.program_id(0) == 0)
    def _():
        w_scr[...] = w_ref[...].astype(BF16)

    nw = nw_ref[...]

    def project(r):
        rows = slice(r * _Z_ROWS, (r + 1) * _Z_ROWS)
        return _gelu(jnp.dot(h_ref[rows, :], w_scr[...], preferred_element_type=F32))

    n_r = TM // _Z_ROWS
    z_next = project(0)
    for r in range(n_r):
        z = z_next
        if r + 1 < n_r:
            z_next = project(r + 1)
        for g in range(C_GROUPS):
            sl = slice(g * HEAD_DIM, (g + 1) * HEAD_DIM)
            v = z[:, C_W + g * HEAD_DIM:C_W + (g + 1) * HEAD_DIM]
            mu = jnp.mean(v, axis=-1, keepdims=True)
            vc = v - mu
            var = jnp.mean(vc * vc, axis=-1, keepdims=True)
            vn = (vc * lax.rsqrt(var + EPS) * nw[:, sl]).astype(BF16)
            n_ch = _Z_ROWS // CHUNK
            vn2 = jnp.concatenate([vn[c * CHUNK:(c + 1) * CHUNK] for c in range(n_ch)], axis=1)
            sp2 = jnp.dot(ws_ref[g], vn2, preferred_element_type=F32)
            for c in range(n_ch):
                cr = slice(c * CHUNK, (c + 1) * CHUNK)
                sp = sp2[:, c * HEAD_DIM:(c + 1) * HEAD_DIM] + bs_ref[g]
                o_ref[r * _Z_ROWS + c * CHUNK:r * _Z_ROWS + (c + 1) * CHUNK, sl] = (
                    z[cr, sl] * sp).astype(BF16)


def _projection_gating(h, w_all, layer, norm_w, ws, bs, n_rows):
    bs_exp = jnp.broadcast_to(bs[:, :, None], (C_GROUPS, CHUNK, HEAD_DIM))
    col0 = F_W + 3 * NA_W
    return pl.pallas_call(
        _proj_gating_kernel,
        grid=(n_rows // TM,),
        in_specs=[
            pl.BlockSpec((TM, D_MODEL), lambda i: (i, 0)),
            pl.BlockSpec((pl.Squeezed(), pl.Element(D_MODEL), pl.Element(2 * C_W)),
                         lambda i: (layer, 0, col0)),
            pl.BlockSpec((1, C_W), lambda i: (0, 0)),
            pl.BlockSpec((C_GROUPS, CHUNK, CHUNK), lambda i: (0, 0, 0)),
            pl.BlockSpec((C_GROUPS, CHUNK, HEAD_DIM), lambda i: (0, 0, 0)),
        ],
        out_specs=pl.BlockSpec((TM, C_W), lambda i: (i, 0)),
        out_shape=jax.ShapeDtypeStruct((n_rows, C_W), BF16),
        scratch_shapes=[pltpu.VMEM((D_MODEL, 2 * C_W), BF16)],
        compiler_params=_cparams(("arbitrary",)),
        name="in_proj_gmlp",
    )(h, w_all, norm_w.reshape(1, C_W), ws.astype(BF16), bs_exp)


def _merge_kernel(ya_ref, yb_ref, atta_ref, attb_ref, sg_ref, gate_ref, xa_ref, xb_ref, g1_ref, sh_ref,
                  sc_ref, nw_ref, wf_ref, wna_ref, wc_ref, wo_ref, o_ref, h_ref, m_scr, *, n_a_tiles):
    i = pl.program_id(0)
    r = _mod_row(i, TM_RES)
    from_a = i < n_a_tiles
    y = jnp.where(from_a, ya_ref[...], yb_ref[...])
    att = jnp.where(from_a, atta_ref[...], attb_ref[...])
    a_f = jnp.dot(y, wf_ref[...], preferred_element_type=F32)
    a_na = jnp.dot(att, wna_ref[...], preferred_element_type=F32)
    a_c = jnp.dot(sg_ref[...], wc_ref[...], preferred_element_type=F32)
    g_f = gate_ref[:, :D_MODEL].astype(F32)
    g_na = gate_ref[:, D_MODEL:2 * D_MODEL].astype(F32)
    g_c = gate_ref[:, 2 * D_MODEL:].astype(F32)
    m_scr[...] = (g_f * a_f + g_na * a_na + g_c * a_c).astype(BF16)
    o = jnp.dot(m_scr[...], wo_ref[...], preferred_element_type=F32)
    x = jnp.where(from_a, xa_ref[...], xb_ref[...])
    xn = x + g1_ref[pl.ds(r, 1), :] * o
    o_ref[...] = xn
    ms = jnp.mean(xn * xn, axis=-1, keepdims=True)
    wmod = nw_ref[...] * (1.0 + sc_ref[pl.ds(r, 1), :])
    h_ref[...] = (xn * lax.rsqrt(ms + EPS) * wmod + sh_ref[pl.ds(r, 1), :]).astype(BF16)


def _merge(ya, yb, atta, attb, sg, gate, xa, xb, mod, norm2_w, wf, wna, wc, wo, layer, n_rows):
    tm = TM_RES
    na = N_LAT // tm
    nbt = max((n_rows - N_LAT) // tm, 1)
    a_spec = lambda w: pl.BlockSpec((tm, w), lambda i: (jnp.minimum(i, na - 1), 0))
    b_spec = lambda w: pl.BlockSpec((tm, w), lambda i: (jnp.clip(i - na, 0, nbt - 1), 0))
    return pl.pallas_call(
        functools.partial(_merge_kernel, n_a_tiles=na),
        grid=(n_rows // tm,),
        in_specs=[
            a_spec(F_W), b_spec(F_W), a_spec(NA_W), b_spec(NA_W),
            pl.BlockSpec((tm, C_W), lambda i: (i, 0)),
            pl.BlockSpec((tm, 3 * D_MODEL), lambda i: (i, 0)),
            a_spec(D_MODEL), b_spec(D_MODEL),
            pl.BlockSpec((MOD_ROWS, D_MODEL), lambda i: (0, 2)),
            pl.BlockSpec((MOD_ROWS, D_MODEL), lambda i: (0, 3)),
            pl.BlockSpec((MOD_ROWS, D_MODEL), lambda i: (0, 4)),
            pl.BlockSpec((1, D_MODEL), lambda i: (0, 0)),
            _resident((F_W, D_MODEL), layer),
            _resident((NA_W, D_MODEL), layer),
            _resident((C_W, D_MODEL), layer),
            _resident((D_MODEL, D_MODEL), layer),
        ],
        out_specs=[pl.BlockSpec((tm, D_MODEL), lambda i: (i, 0)),
                   pl.BlockSpec((tm, D_MODEL), lambda i: (i, 0))],
        out_shape=[jax.ShapeDtypeStruct((n_rows, D_MODEL), F32),
                   jax.ShapeDtypeStruct((n_rows, D_MODEL), BF16)],
        scratch_shapes=[pltpu.VMEM((tm, D_MODEL), BF16)],
        compiler_params=_cparams(("arbitrary",)),
        name="gated_merge",
    )(ya, yb, atta, attb, sg, gate, xa, xb, mod, mod, mod, norm2_w.reshape(1, D_MODEL), wf, wna, wc, wo)


_FF_TILES = D_FF // TN


def _ffn_up_kernel(h_ref, hp_ref, hn_ref, wa_ref, wg_ref, cw_ref, cb_ref, o_ref, wa_scr, wg_scr, a_scr):
    i = pl.program_id(1)
    tp = h_ref.shape[0]
    rc = tp // N_CHUNKS

    @pl.when(i == 0)
    def _():
        wa_scr[...] = wa_ref[...].astype(BF16)
        wg_scr[...] = wg_ref[...].astype(BF16)

    n_chunks = N_CHUNKS
    for c in range(n_chunks):
        parts = [h_ref[c * rc:(c + 1) * rc, :]]
        lo, hi = HALO + c * rc, HALO + (c + 1) * rc
        if c == 0:
            parts.insert(0, hp_ref[...])
            lo -= HALO
        if c == n_chunks - 1:
            parts.append(hn_ref[...])
            hi += HALO
        lhs = parts[0] if len(parts) == 1 else jnp.concatenate(parts, axis=0)
        a_scr[lo:hi, :] = jnp.dot(lhs, wa_scr[...], preferred_element_type=F32)

    w0, w1, w2, cb = cw_ref[0:1], cw_ref[1:2], cw_ref[2:3], cb_ref[...]
    gate_proj = lambda c: jnp.dot(h_ref[c * rc:(c + 1) * rc, :], wg_scr[...], preferred_element_type=F32)
    gv_nxt = gate_proj(0)
    for c in range(n_chunks):
        r0 = HALO + c * rc
        gv = gv_nxt
        if c + 1 < n_chunks:
            gv_nxt = gate_proj(c + 1)
        row = i * tp + c * rc + lax.broadcasted_iota(jnp.int32, (rc, TN), 0)
        seq_mask = jnp.where(row < N_LAT, SEQ - 1, CTX_LEN - 1)
        pos = row & seq_mask
        prev = jnp.where(pos != 0, a_scr[r0 - 1:r0 - 1 + rc, :], 0.0)
        nxt = jnp.where(pos != seq_mask, a_scr[r0 + 1:r0 + 1 + rc, :], 0.0)
        a = cb + w0 * prev + w1 * a_scr[r0:r0 + rc, :] + w2 * nxt
        o_ref[c * rc:(c + 1) * rc, :] = (a * jax.nn.sigmoid(a) * gv).astype(BF16)


def _ffn_up(h2, w_up_all, layer, conv_w, conv_b, n_rows):
    tp = _proj_tile(n_rows)
    hb = tp // HALO
    last = h2.shape[0] // HALO - 1

    def wspec(off):
        return pl.BlockSpec((pl.Squeezed(), D_MODEL, TN), lambda j, i: (layer, 0, off + j))

    return pl.pallas_call(
        _ffn_up_kernel,
        grid=(_FF_TILES, n_rows // tp),
        in_specs=[
            pl.BlockSpec((tp, D_MODEL), lambda j, i: (i, 0)),
            pl.BlockSpec((HALO, D_MODEL), lambda j, i: (jnp.maximum(i * hb - 1, 0), 0)),
            pl.BlockSpec((HALO, D_MODEL), lambda j, i: (jnp.minimum((i + 1) * hb, last), 0)),
            wspec(0),
            wspec(_FF_TILES),
            pl.BlockSpec((CONV_W, TN), lambda j, i: (0, j)),
            pl.BlockSpec((1, TN), lambda j, i: (0, j)),
        ],
        out_specs=pl.BlockSpec((tp, TN), lambda j, i: (i, j)),
        out_shape=jax.ShapeDtypeStruct((n_rows, D_FF), BF16),
        scratch_shapes=[pltpu.VMEM((D_MODEL, TN), BF16), pltpu.VMEM((D_MODEL, TN), BF16),
                        pltpu.VMEM((tp + 2 * HALO, TN), F32)],
        compiler_params=_cparams(("arbitrary", "arbitrary")),
        name="ffn_up_conv",
    )(h2, h2, h2, w_up_all, w_up_all, conv_w, conv_b.reshape(1, D_FF))


def _ffn_down_kernel(a_ref, w_ref, x_ref, g2_ref, sh_ref, sc_ref, nw_ref, *refs, final):
    tm = a_ref.shape[0]
    r = _mod_row(pl.program_id(0), tm)
    xs_ref = refs[1] if final else refs[0]
    act = a_ref[...]
    ss = jnp.zeros((tm, 1), F32)
    n_c = D_MODEL // TN
    down = lambda c: jnp.dot(act, w_ref[:, c * TN:(c + 1) * TN], preferred_element_type=F32)
    o_nxt = down(0)
    for c in range(n_c):
        sl = slice(c * TN, (c + 1) * TN)
        o = o_nxt
        if c + 1 < n_c:
            o_nxt = down(c + 1)
        xn = x_ref[:, sl] + g2_ref[pl.ds(r, 1), sl] * o
        xs_ref[:, sl] = xn
        ss = ss + jnp.sum(xn * xn, axis=-1, keepdims=True)
    inv = lax.rsqrt(ss * (1.0 / D_MODEL) + EPS)
    if final:
        refs[0][...] = xs_ref[...] * inv * nw_ref[...]
    else:
        wmod = nw_ref[...] * (1.0 + sc_ref[pl.ds(r, 1), :])
        refs[1][...] = (xs_ref[...] * inv * wmod + sh_ref[pl.ds(r, 1), :]).astype(BF16)


def _ffn_down(act, w_down, x_all, mod, mod_next, norm_w, n_rows, final):
    tm = TM_DOWN
    assert n_rows % tm == 0 and N_LAT % tm == 0
    row = lambda i: (i, 0)
    if final:
        out_specs = pl.BlockSpec((tm, D_MODEL), row)
        out_shape = jax.ShapeDtypeStruct((n_rows, D_MODEL), F32)
        scratch = [pltpu.VMEM((tm, D_MODEL), F32)]
    else:
        out_specs = [pl.BlockSpec((tm, D_MODEL), row), pl.BlockSpec((tm, D_MODEL), row)]
        out_shape = [jax.ShapeDtypeStruct((n_rows, D_MODEL), F32),
                     jax.ShapeDtypeStruct((n_rows, D_MODEL), BF16)]
        scratch = []
    return pl.pallas_call(
        functools.partial(_ffn_down_kernel, final=final),
        grid=(n_rows // tm,),
        in_specs=[
            pl.BlockSpec((tm, D_FF), row),
            pl.BlockSpec((D_FF, D_MODEL), lambda i: (0, 0), pipeline_mode=pl.Buffered(1)),
            pl.BlockSpec((tm, D_MODEL), row),
            pl.BlockSpec((MOD_ROWS, D_MODEL), lambda i: (0, 5)),
            pl.BlockSpec((MOD_ROWS, D_MODEL), lambda i: (0, 0)),
            pl.BlockSpec((MOD_ROWS, D_MODEL), lambda i: (0, 1)),
            pl.BlockSpec((1, D_MODEL), lambda i: (0, 0)),
        ],
        out_specs=out_specs,
        out_shape=out_shape,
        scratch_shapes=scratch,
        compiler_params=pltpu.CompilerParams(dimension_semantics=("arbitrary",),
                                             vmem_limit_bytes=60 * 1024 * 1024),
        name="ffn_down",
    )(act, w_down, x_all, mod, mod_next, mod_next, norm_w.reshape(1, D_MODEL))


def kernel(x, c, ctx, c_ctx, ada_w, ada_b, norm1_w, norm2_w, w_in, na_rpb, gmlp_norm_w, gmlp_ws,
           gmlp_bs, w_f_out, w_na_out, w_c_out, w_o, ffn_up, ffn_conv_w, ffn_conv_b, ffn_down,
           final_norm_w):
    assert x.shape == (BATCH, SEQ, D_MODEL) and ctx.shape == (BATCH, CTX_LEN, D_MODEL)
    cvec = jnp.concatenate(
        [c, c_ctx[None, :], jnp.zeros((MOD_ROWS - BATCH - 1, D_MODEL), F32)], axis=0)
    mod_all = _modulation(cvec, ada_w, ada_b)
    tables = _dft_tables()
    merge_w = tuple(w.astype(BF16) for w in (w_f_out, w_na_out, w_c_out, w_o))
    xa = x.reshape(N_LAT, D_MODEL)
    xb = ctx.reshape(N_CTX, D_MODEL)
    h1 = _prenorm(xa, xb, mod_all[0], norm1_w[0])

    for l in range(DEPTH):
        last = l == DEPTH - 1
        n_rows = N_LAT if last else N_TOK
        mod = mod_all[l]
        fs = _projection_f(h1, w_in, l, n_rows, not last)
        qkv = _projection(h1, w_in, l, F_W, 3 * NA_W, 1024, _ident, "in_proj_qkv", N_TOK)
        sg = _projection_gating(h1, w_in, l, gmlp_norm_w[l], gmlp_ws[l], gmlp_bs[l], n_rows)
        gate, w_down = _projection(h1, w_in, l, F_W + 3 * NA_W + 2 * C_W, 3 * D_MODEL, 1024, jax.nn.sigmoid,
                                   "in_proj_gate", n_rows, side=(ffn_down, CHUNK))
        atts = _attention(qkv, _bias_table(na_rpb[l]), not last)
        ys = _fourier(fs[0], fs[-1], tables, not last)
        x_mid, h2 = _merge(ys[0], ys[-1], atts[0], atts[-1], sg, gate, xa, xb, mod, norm2_w[l],
                           *merge_w, l, n_rows)
        act = _ffn_up(h2, ffn_up, l, ffn_conv_w[l], ffn_conv_b[l], n_rows)
        if last:
            out = _ffn_down(act, w_down, x_mid, mod, mod, final_norm_w, n_rows, True)
        else:
            xa, h1 = _ffn_down(act, w_down, x_mid, mod, mod_all[l + 1], norm1_w[l + 1], n_rows, False)
            xb = xa
    return out.reshape(BATCH, SEQ, D_MODEL)
```

```python
import functools
import math

import numpy as np
import jax
import jax.numpy as jnp
from jax import lax
from jax.experimental import pallas as pl
from jax.experimental.pallas import tpu as pltpu

F32 = jnp.float32
BF16 = jnp.bfloat16

D_MODEL = 2048
BATCH = 4
SEQ = 4096
DEPTH = 2
GRID_W = 64
ROWS = SEQ // GRID_W
CTX_LEN = 256
HEAD_DIM = 128
F_GROUPS = 4
F_W = F_GROUPS * HEAD_DIM
NA_HEADS = 8
NA_W = NA_HEADS * HEAD_DIM
NA_KH = 8
NA_KW = 16
C_GROUPS = 4
C_W = C_GROUPS * HEAD_DIM
CHUNK = 128
D_FF = 5632
CONV_W = 3
EPS = 1e-6
NEG_INF = -1e30
IN_W = F_W + 3 * NA_W + 2 * C_W + 3 * D_MODEL

N_LAT = BATCH * SEQ
N_CTX = BATCH * CTX_LEN
N_TOK = N_LAT + N_CTX
MOD_ROWS = 8

TM = 1024
TN = 512
N_CHUNKS = 8
TM_RES = 256
TM_DOWN = 512
HALO = 16
QROWS = 4
KROWS = 12
ATTN_UNROLL = 4
DFT_UNROLL = 8
PITCH = 72
VMEM_LIMIT = 56 * 1024 * 1024


def _cparams(sem):
    return pltpu.CompilerParams(dimension_semantics=sem, vmem_limit_bytes=VMEM_LIMIT)


def _mod_row(i, tm):
    return jnp.minimum((i * tm) // SEQ, BATCH)


def _norm_mod(x, nw, sh, sc):
    ms = jnp.mean(x * x, axis=-1, keepdims=True)
    return (x * lax.rsqrt(ms + EPS) * nw) * (1.0 + sc) + sh


def _rms(x, nw):
    ms = jnp.mean(x * x, axis=-1, keepdims=True)
    return x * lax.rsqrt(ms + EPS) * nw


def _resident2d(shape):
    return pl.BlockSpec(shape, lambda *_: (0, 0), pipeline_mode=pl.Buffered(1))


def _mod_kernel(c_ref, w_ref, b_ref, o_ref):
    c = c_ref[...]
    s = (c * jax.nn.sigmoid(c)).astype(BF16)
    o_ref[0] = jnp.dot(s, w_ref[0].astype(BF16), preferred_element_type=F32) + b_ref[0]


def _modulation(cvec, ada_w, ada_b):
    tn = 1024
    return pl.pallas_call(
        _mod_kernel,
        grid=(DEPTH, 6 * D_MODEL // tn),
        in_specs=[
            pl.BlockSpec((MOD_ROWS, D_MODEL), lambda l, j: (0, 0)),
            pl.BlockSpec((1, D_MODEL, tn), lambda l, j: (l, 0, j)),
            pl.BlockSpec((1, 1, tn), lambda l, j: (l, 0, j)),
        ],
        out_specs=pl.BlockSpec((1, MOD_ROWS, tn), lambda l, j: (l, 0, j)),
        out_shape=jax.ShapeDtypeStruct((DEPTH, MOD_ROWS, 6 * D_MODEL), F32),
        compiler_params=_cparams(("arbitrary", "arbitrary")),
        name="modulation",
    )(cvec, ada_w, ada_b.reshape(DEPTH, 1, 6 * D_MODEL))


def _prenorm_kernel(xa_ref, xb_ref, sh_ref, sc_ref, nw_ref, o_ref, *, n_a_tiles):
    i = pl.program_id(0)
    r = _mod_row(i, TM)
    x = jnp.where(i < n_a_tiles, xa_ref[...], xb_ref[...])
    h = _norm_mod(x, nw_ref[...], sh_ref[pl.ds(r, 1), :], sc_ref[pl.ds(r, 1), :])
    o_ref[...] = h.astype(BF16)


def _prenorm(x2d, ctx2d, mod, norm_w):
    na = N_LAT // TM
    return pl.pallas_call(
        functools.partial(_prenorm_kernel, n_a_tiles=na),
        grid=(N_TOK // TM,),
        in_specs=[
            pl.BlockSpec((TM, D_MODEL), lambda i: (jnp.minimum(i, na - 1), 0)),
            pl.BlockSpec((TM, D_MODEL), lambda i: (jnp.maximum(i - na, 0), 0)),
            pl.BlockSpec((MOD_ROWS, D_MODEL), lambda i: (0, 0)),
            pl.BlockSpec((MOD_ROWS, D_MODEL), lambda i: (0, 1)),
            pl.BlockSpec((1, D_MODEL), lambda i: (0, 0)),
        ],
        out_specs=pl.BlockSpec((TM, D_MODEL), lambda i: (i, 0)),
        out_shape=jax.ShapeDtypeStruct((N_TOK, D_MODEL), BF16),
        compiler_params=_cparams(("arbitrary",)),
        name="prenorm",
    )(x2d, ctx2d, mod, mod, norm_w.reshape(1, D_MODEL))


def _ident(a):
    return a


def _gelu(a):
    return 0.5 * a * (1.0 + lax.erf(a * math.sqrt(0.5)))


def _proj_tile(n_rows):
    tp = n_rows // 8
    assert tp * 8 == n_rows and tp % (N_CHUNKS * HALO) == 0
    return tp


def _proj_kernel(h_ref, w_ref, *rest, epilogue, side):
    o_ref, w_scr = rest[side], rest[-1]
    for k in range(side):
        rest[side + 1 + k][...] = rest[k][...].astype(BF16)

    @pl.when(pl.program_id(1) == 0)
    def _():
        w_scr[...] = w_ref[...].astype(BF16)

    rc = h_ref.shape[0] // N_CHUNKS
    for c in range(N_CHUNKS):
        rows = slice(c * rc, (c + 1) * rc)
        acc = jnp.dot(h_ref[rows, :], w_scr[...], preferred_element_type=F32)
        o_ref[rows, :] = epilogue(acc).astype(BF16)


def _projection(h, w_all, layer, col0, n_cols, tn, epilogue, name, n_rows, side=None):
    tp = _proj_tile(n_rows)
    n_i = n_rows // tp
    in_specs = [
        pl.BlockSpec((tp, D_MODEL), lambda j, i: (i, 0)),
        pl.BlockSpec((pl.Squeezed(), pl.Element(D_MODEL), pl.Element(tn)),
                     lambda j, i: (layer, 0, pl.multiple_of(col0 + j * tn, 128))),
    ]
    out_specs = [pl.BlockSpec((tp, tn), lambda j, i: (i, j))]
    out_shape = [jax.ShapeDtypeStruct((n_rows, n_cols), BF16)]
    operands = [h, w_all]
    for w_side, slab in side or ():
        _, rows, cols = w_side.shape
        n_slabs = rows // slab
        assert n_slabs * slab == rows and n_slabs <= (n_cols // tn) * n_i

        def slab_of(j, i, n_slabs=n_slabs):
            return jnp.minimum(j * n_i + i, n_slabs - 1)

        in_specs.append(pl.BlockSpec((pl.Squeezed(), slab, cols), lambda j, i, f=slab_of: (layer, f(j, i), 0)))
        out_specs.append(pl.BlockSpec((slab, cols), lambda j, i, f=slab_of: (f(j, i), 0)))
        out_shape.append(jax.ShapeDtypeStruct((rows, cols), BF16))
        operands.append(w_side)
    outs = pl.pallas_call(
        functools.partial(_proj_kernel, epilogue=epilogue, side=len(side or ())),
        grid=(n_cols // tn, n_i),
        in_specs=in_specs,
        out_specs=out_specs,
        out_shape=out_shape,
        scratch_shapes=[pltpu.VMEM((D_MODEL, tn), BF16)],
        compiler_params=_cparams(("arbitrary", "arbitrary")),
        name=name,
    )(*operands)
    return outs if side else outs[0]


def _proj_f_kernel(h_ref, w_ref, *rest, with_tok):
    if with_tok:
        og_ref, ot_ref, w_scr, t_scr = rest
    else:
        og_ref, w_scr, t_scr = rest

    @pl.when(pl.program_id(0) == 0)
    def _():
        w_scr[...] = w_ref[...].astype(BF16)

    n1 = GRID_W
    acc = jnp.dot(h_ref[...], w_scr[...], preferred_element_type=F32)
    if with_tok:
        ot_ref[...] = acc.astype(BF16)
    for a in range(TM // n1):
        for g in range(F_GROUPS):
            t_scr[g, a * PITCH:a * PITCH + n1, :] = acc[a * n1:(a + 1) * n1, g * HEAD_DIM:(g + 1) * HEAD_DIM]
    for b in range(n1):
        for g in range(F_GROUPS):
            col = b * F_W + g * HEAD_DIM
            og_ref[:, col:col + HEAD_DIM] = t_scr[g, pl.ds(b, TM // n1, stride=PITCH), :].astype(BF16)


def _projection_f(h, w_all, layer, n_rows, with_tok):
    n1 = GRID_W
    out_specs = [pl.BlockSpec((TM // n1, n1 * F_W), lambda i: (i, 0))]
    out_shape = [jax.ShapeDtypeStruct((n_rows // n1, n1 * F_W), BF16)]
    if with_tok:
        out_specs.append(pl.BlockSpec((TM, F_W), lambda i: (i, 0)))
        out_shape.append(jax.ShapeDtypeStruct((n_rows, F_W), BF16))
    return pl.pallas_call(
        functools.partial(_proj_f_kernel, with_tok=with_tok),
        grid=(n_rows // TM,),
        in_specs=[
            pl.BlockSpec((TM, D_MODEL), lambda i: (i, 0)),
            pl.BlockSpec((pl.Squeezed(), D_MODEL, F_W), lambda i: (layer, 0, 0)),
        ],
        out_specs=out_specs,
        out_shape=out_shape,
        scratch_shapes=[pltpu.VMEM((D_MODEL, F_W), BF16),
                        pltpu.VMEM((F_GROUPS, (TM // n1) * PITCH, HEAD_DIM), F32)],
        compiler_params=_cparams(("arbitrary",)),
        name="in_proj_f",
    )(h, w_all)


_N_QBLK = ROWS // QROWS
_BIAS_VARIANTS = (0, 1, _N_QBLK - 1)


def _key_row0(nb):
    return min(max(QROWS * nb - NA_KH // 2, 0), ROWS - KROWS)


def _bias_kernel(rpb_ref, o_ref, pair_scr):
    h = pl.program_id(0)
    kc = lax.broadcasted_iota(jnp.int32, (GRID_W, 2 * GRID_W), 0)
    lane = lax.broadcasted_iota(jnp.int32, (GRID_W, 2 * GRID_W), 1)
    qc = lane & (GRID_W - 1)
    low = lane < GRID_W
    dc = jnp.clip(kc - qc + NA_KW - 1, 0, 2 * NA_KW - 2)
    c_start = jnp.clip(qc - NA_KW // 2, 0, GRID_W - NA_KW)
    col_in = (kc >= c_start) & (kc < c_start + NA_KW)
    n_dr = 2 * NA_KH - 1
    for dr in range(n_dr + 1):
        acc = jnp.zeros((GRID_W, 2 * GRID_W), F32)
        for d in range(2 * NA_KW - 1):
            lo = rpb_ref[h, dr, d] if 0 <= dr < n_dr else 0.0
            hi = rpb_ref[h, dr - 1, d] if 0 <= dr - 1 < n_dr else 0.0
            acc = jnp.where(dc == d, jnp.where(low, lo, hi), acc)
        pair_scr[dr] = acc * _LOG2E
    for vi, nb in enumerate(_BIAS_VARIANTS):
        r0 = _key_row0(nb)
        for t in range(KROWS):
            kr = r0 + t
            for ip in range(QROWS // 2):
                ok = []
                for i in (2 * ip, 2 * ip + 1):
                    r_start = min(max(QROWS * nb + i - NA_KH // 2, 0), ROWS - NA_KH)
                    ok.append(r_start <= kr < r_start + NA_KH)
                dr = kr - (QROWS * nb + 2 * ip) + NA_KH - 1
                if not (ok[0] or ok[1]):
                    tile = jnp.full((GRID_W, 2 * GRID_W), NEG_INF, F32)
                else:
                    valid = col_in
                    if not ok[0]:
                        valid = valid & jnp.logical_not(low)
                    if not ok[1]:
                        valid = valid & low
                    tile = jnp.where(valid, pair_scr[dr], NEG_INF)
                o_ref[vi, 0, t * GRID_W:(t + 1) * GRID_W, ip * 2 * GRID_W:(ip + 1) * 2 * GRID_W] = tile


def _bias_table(rpb):
    nv = len(_BIAS_VARIANTS)
    return pl.pallas_call(
        _bias_kernel,
        grid=(NA_HEADS,),
        in_specs=[pl.BlockSpec(memory_space=pltpu.SMEM)],
        out_specs=pl.BlockSpec((nv, 1, KROWS * GRID_W, QROWS * GRID_W), lambda h: (0, h, 0, 0)),
        out_shape=jax.ShapeDtypeStruct((nv, NA_HEADS, KROWS * GRID_W, QROWS * GRID_W), F32),
        scratch_shapes=[pltpu.VMEM((2 * NA_KH, GRID_W, 2 * GRID_W), F32)],
        compiler_params=_cparams(("arbitrary",)),
        name="attn_bias_table",
    )(rpb)


_QK_DIMS = (((1,), (1,)), ((), ()))
_LOG2E = math.log2(math.e)
_SCALE = HEAD_DIM ** -0.5 * _LOG2E
_QB = QROWS * GRID_W


def _attn_kernel(q_ref, k_ref, v_ref, kc_ref, vc_ref, bias_ref, *rest, with_ctx):
    if with_ctx:
        qc_ref, o_ref, oc_ref, sw_scr, sc_scr, vt_scr, vct_scr = rest
    else:
        o_ref, sw_scr, sc_scr, vt_scr, vct_scr = rest
    kc = kc_ref[...]
    vc = vc_ref[...]
    pair = 2 * GRID_W

    def transposed(x):
        return x.astype(F32).T.astype(BF16)

    for j in range(SEQ // pair):
        vt_scr[j] = transposed(v_ref[j * pair:(j + 1) * pair, :])
    for j in range(CTX_LEN // pair):
        vct_scr[:, j * pair:(j + 1) * pair] = transposed(vc[j * pair:(j + 1) * pair, :])

    def key_row0(nb):
        return jnp.clip(QROWS * nb - NA_KH // 2, 0, ROWS - KROWS)

    def scores(nb, slot):
        qrow = pl.multiple_of(nb * _QB, _QB)
        variant = jnp.where(nb == 0, 0, jnp.where(nb == _N_QBLK - 1, 2, 1))
        q = (q_ref[pl.ds(qrow, _QB), :].astype(F32) * _SCALE).astype(BF16)
        kw = k_ref[pl.ds(pl.multiple_of(key_row0(nb) * GRID_W, GRID_W), KROWS * GRID_W), :]
        sw_scr[slot] = lax.dot_general(kw, q, _QK_DIMS, preferred_element_type=F32) + bias_ref[variant, 0]
        sc_scr[slot] = lax.dot_general(kc, q, _QK_DIMS, preferred_element_type=F32)

    def finish(nb, slot):
        qrow = pl.multiple_of(nb * _QB, _QB)
        j0 = key_row0(nb) // 2
        s_w = sw_scr[slot]
        s_c = sc_scr[slot]
        m = jnp.maximum(jnp.max(s_w, axis=0, keepdims=True), jnp.max(s_c, axis=0, keepdims=True))
        p_w = jnp.exp2(s_w - m)
        p_c = jnp.exp2(s_c - m)
        denom = jnp.sum(p_w, axis=0, keepdims=True) + jnp.sum(p_c, axis=0, keepdims=True)
        vt_win = jnp.concatenate([vt_scr[j0 + t] for t in range(KROWS // 2)], axis=1)
        o_t = (jnp.dot(vt_win, p_w.astype(BF16), preferred_element_type=F32)
               + jnp.dot(vct_scr[...], p_c.astype(BF16), preferred_element_type=F32))
        o_ref[pl.ds(qrow, _QB), :] = (o_t / denom).T.astype(BF16)

    if with_ctx:
        q = (qc_ref[...].astype(F32) * _SCALE).astype(BF16)
        s = lax.dot_general(q, kc, _QK_DIMS, preferred_element_type=F32)
        p = jnp.exp2(s - jnp.max(s, axis=-1, keepdims=True))
        o = jnp.dot(p.astype(BF16), vc, preferred_element_type=F32)
        oc_ref[...] = (o / jnp.sum(p, axis=-1, keepdims=True)).astype(BF16)

    scores(0, 0)

    def pair_of_blocks(t, carry):
        nb = 2 * t
        scores(nb + 1, 1)
        finish(nb, 0)
        scores(jnp.minimum(nb + 2, _N_QBLK - 1), 0)
        finish(nb + 1, 1)
        return carry

    lax.fori_loop(0, _N_QBLK // 2, pair_of_blocks, 0, unroll=ATTN_UNROLL)


def _attention(qkv, bias, with_ctx):
    ctx0 = N_LAT // CTX_LEN
    nv = len(_BIAS_VARIANTS)
    in_specs = [
        pl.BlockSpec((SEQ, HEAD_DIM), lambda b, h: (b, h)),
        pl.BlockSpec((SEQ, HEAD_DIM), lambda b, h: (b, NA_HEADS + h)),
        pl.BlockSpec((SEQ, HEAD_DIM), lambda b, h: (b, 2 * NA_HEADS + h)),
        pl.BlockSpec((CTX_LEN, HEAD_DIM), lambda b, h: (ctx0 + b, NA_HEADS + h)),
        pl.BlockSpec((CTX_LEN, HEAD_DIM), lambda b, h: (ctx0 + b, 2 * NA_HEADS + h)),
        pl.BlockSpec((nv, 1, KROWS * GRID_W, _QB), lambda b, h: (0, h, 0, 0)),
    ]
    out_specs = [pl.BlockSpec((SEQ, HEAD_DIM), lambda b, h: (b, h))]
    out_shape = [jax.ShapeDtypeStruct((N_LAT, NA_W), BF16)]
    operands = [qkv, qkv, qkv, qkv, qkv, bias]
    if with_ctx:
        in_specs.append(pl.BlockSpec((CTX_LEN, HEAD_DIM), lambda b, h: (ctx0 + b, h)))
        out_specs.append(pl.BlockSpec((CTX_LEN, HEAD_DIM), lambda b, h: (b, h)))
        out_shape.append(jax.ShapeDtypeStruct((N_CTX, NA_W), BF16))
        operands.append(qkv)
    return pl.pallas_call(
        functools.partial(_attn_kernel, with_ctx=with_ctx),
        grid=(BATCH, NA_HEADS),
        in_specs=in_specs,
        out_specs=out_specs,
        out_shape=out_shape,
        scratch_shapes=[pltpu.VMEM((2, KROWS * GRID_W, _QB), F32), pltpu.VMEM((2, CTX_LEN, _QB), F32),
                        pltpu.VMEM((SEQ // (2 * GRID_W), HEAD_DIM, 2 * GRID_W), BF16),
                        pltpu.VMEM((HEAD_DIM, CTX_LEN), BF16)],
        compiler_params=_cparams(("arbitrary", "arbitrary")),
        name="neighborhood_attention",
    )(*operands)


def _dft_tables():
    n1 = GRID_W
    idx = np.arange(n1)
    ang1 = 2 * np.pi * np.outer(idx, idx) / n1
    w1 = np.concatenate([np.cos(ang1), -np.sin(ang1)], axis=0)
    q = idx[:, None, None]
    p = idx[None, :, None]
    b = idx[None, None, :]
    ang2 = 2 * np.pi * b * (n1 * p + q) / SEQ
    wc, ws = np.cos(ang2), np.sin(ang2)
    m2 = np.concatenate([np.concatenate([wc, ws], axis=2),
                         np.concatenate([-ws, wc], axis=2)], axis=1)
    d = np.arange(HEAD_DIM)
    ang3 = 2 * np.pi * np.outer(d, d) / HEAD_DIM
    cs = np.concatenate([np.cos(ang3), np.sin(ang3)], axis=0) / math.sqrt(SEQ * HEAD_DIM)
    n = np.arange(CTX_LEN)
    angc = 2 * np.pi * np.outer(n, n) / CTX_LEN
    t1 = np.concatenate([np.cos(angc), np.sin(angc)], axis=0)
    cs_ctx = np.concatenate([np.cos(ang3), -np.sin(ang3)], axis=0) / math.sqrt(CTX_LEN * HEAD_DIM)
    as_bf16 = lambda a: jnp.asarray(a, dtype=F32).astype(BF16)
    return as_bf16(w1), as_bf16(m2), as_bf16(cs), as_bf16(t1), as_bf16(cs_ctx)


def _fourier_kernel(x_ref, w1_ref, m2_ref, cs_ref, *rest, with_ctx):
    if with_ctx:
        xc_ref, t1_ref, csc_ref, o_ref, oc_ref, re_scr, im_scr, y_scr = rest
        t = jnp.dot(t1_ref[...], xc_ref[...], preferred_element_type=F32)
        csc = csc_ref[...]
        for g in range(F_GROUPS):
            sl = slice(g * HEAD_DIM, (g + 1) * HEAD_DIM)
            lhs = jnp.concatenate([t[:CTX_LEN, sl], t[CTX_LEN:, sl]], axis=1).astype(BF16)
            oc_ref[:, sl] = jnp.dot(lhs, csc, preferred_element_type=F32).astype(BF16)
    else:
        o_ref, re_scr, im_scr, y_scr = rest
    n1 = GRID_W
    w1 = w1_ref[...]
    for b in range(n1):
        a = jnp.dot(w1, x_ref[:, b * F_W:(b + 1) * F_W], preferred_element_type=F32)
        for g in range(F_GROUPS):
            sl = slice(g * HEAD_DIM, (g + 1) * HEAD_DIM)
            re_scr[g, pl.ds(b, n1, stride=PITCH), :] = a[:n1, sl]
            im_scr[g, pl.ds(b, n1, stride=PITCH), :] = a[n1:, sl]

    cs = cs_ref[...]

    def column_dft(q):
        row = pl.multiple_of(q * PITCH, 8)
        a = jnp.concatenate(
            [jnp.concatenate([re_scr[g, pl.ds(row, n1), :] for g in range(F_GROUPS)], axis=1),
             jnp.concatenate([im_scr[g, pl.ds(row, n1), :] for g in range(F_GROUPS)], axis=1)], axis=0)
        return jnp.dot(m2_ref[q], a.astype(BF16), preferred_element_type=F32)

    def channel_dft(q, z):
        lhs = jnp.concatenate(
            [jnp.concatenate([z[:n1, g * HEAD_DIM:(g + 1) * HEAD_DIM],
                              z[n1:, g * HEAD_DIM:(g + 1) * HEAD_DIM]], axis=1)
             for g in range(F_GROUPS)], axis=0).astype(BF16)
        y = jnp.dot(lhs, cs, preferred_element_type=F32)
        for g in range(F_GROUPS):
            y_scr[g, pl.ds(q, n1, stride=PITCH), :] = y[g * n1:(g + 1) * n1]

    def per_group(t, carry):
        qs = [t * DFT_UNROLL + k for k in range(DFT_UNROLL)]
        zs = [column_dft(q) for q in qs]
        for q, z in zip(qs, zs):
            channel_dft(q, z)
        return carry

    lax.fori_loop(0, n1 // DFT_UNROLL, per_group, 0)
    for p in range(n1):
        for g in range(F_GROUPS):
            o_ref[p * n1:(p + 1) * n1, g * HEAD_DIM:(g + 1) * HEAD_DIM] = (
                y_scr[g, p * PITCH:p * PITCH + n1, :].astype(BF16))


def _fourier(fx, f_ctx, tables, with_ctx):
    w1, m2, cs, t1, cs_ctx = tables
    n1 = GRID_W
    const = lambda shape: pl.BlockSpec(shape, lambda b: (0,) * len(shape))
    in_specs = [pl.BlockSpec((n1, n1 * F_W), lambda b: (b, 0)), const((2 * n1, n1)),
                const((n1, 2 * n1, 2 * n1)), const((2 * HEAD_DIM, HEAD_DIM))]
    out_specs = [pl.BlockSpec((SEQ, F_W), lambda b: (b, 0))]
    out_shape = [jax.ShapeDtypeStruct((N_LAT, F_W), BF16)]
    operands = [fx, w1, m2, cs]
    if with_ctx:
        ctx0 = N_LAT // CTX_LEN
        in_specs += [pl.BlockSpec((CTX_LEN, F_W), lambda b: (ctx0 + b, 0)), const((2 * CTX_LEN, CTX_LEN)),
                     const((2 * HEAD_DIM, HEAD_DIM))]
        out_specs.append(pl.BlockSpec((CTX_LEN, F_W), lambda b: (b, 0)))
        out_shape.append(jax.ShapeDtypeStruct((N_CTX, F_W), BF16))
        operands += [f_ctx, t1, cs_ctx]
    return pl.pallas_call(
        functools.partial(_fourier_kernel, with_ctx=with_ctx),
        grid=(BATCH,),
        in_specs=in_specs,
        out_specs=out_specs,
        out_shape=out_shape,
        scratch_shapes=[pltpu.VMEM((F_GROUPS, n1 * PITCH, HEAD_DIM), F32)] * 3,
        compiler_params=_cparams(("arbitrary",)),
        name="fourier_mix",
    )(*operands)


_Z_ROWS = 2 * CHUNK


def _proj_gating_kernel(h_ref, w_ref, nw_ref, ws_ref, bs_ref, o_ref, w_scr):
    @pl.when(pl.program_id(0) == 0)
    def _():
        w_scr[...] = w_ref[...].astype(BF16)

    nw = nw_ref[...]

    def project(r):
        rows = slice(r * _Z_ROWS, (r + 1) * _Z_ROWS)
        return _gelu(jnp.dot(h_ref[rows, :], w_scr[...], preferred_element_type=F32))

    n_r = TM // _Z_ROWS
    z_next = project(0)
    for r in range(n_r):
        z = z_next
        if r + 1 < n_r:
            z_next = project(r + 1)
        for g in range(C_GROUPS):
            sl = slice(g * HEAD_DIM, (g + 1) * HEAD_DIM)
            v = z[:, C_W + g * HEAD_DIM:C_W + (g + 1) * HEAD_DIM]
            mu = jnp.mean(v, axis=-1, keepdims=True)
            vc = v - mu
            var = jnp.mean(vc * vc, axis=-1, keepdims=True)
            vn = (vc * lax.rsqrt(var + EPS) * nw[:, sl]).astype(BF16)
            n_ch = _Z_ROWS // CHUNK
            vn2 = jnp.concatenate([vn[c * CHUNK:(c + 1) * CHUNK] for c in range(n_ch)], axis=1)
            sp2 = jnp.dot(ws_ref[g], vn2, preferred_element_type=F32)
            for c in range(n_ch):
                cr = slice(c * CHUNK, (c + 1) * CHUNK)
                sp = sp2[:, c * HEAD_DIM:(c + 1) * HEAD_DIM] + bs_ref[g]
                o_ref[r * _Z_ROWS + c * CHUNK:r * _Z_ROWS + (c + 1) * CHUNK, sl] = (
                    z[cr, sl] * sp).astype(BF16)


def _projection_gating(h, w_all, layer, norm_w, ws, bs, n_rows):
    bs_exp = jnp.broadcast_to(bs[:, :, None], (C_GROUPS, CHUNK, HEAD_DIM))
    col0 = F_W + 3 * NA_W
    return pl.pallas_call(
        _proj_gating_kernel,
        grid=(n_rows // TM,),
        in_specs=[
            pl.BlockSpec((TM, D_MODEL), lambda i: (i, 0)),
            pl.BlockSpec((pl.Squeezed(), pl.Element(D_MODEL), pl.Element(2 * C_W)),
                         lambda i: (layer, 0, col0)),
            pl.BlockSpec((1, C_W), lambda i: (0, 0)),
            pl.BlockSpec((C_GROUPS, CHUNK, CHUNK), lambda i: (0, 0, 0)),
            pl.BlockSpec((C_GROUPS, CHUNK, HEAD_DIM), lambda i: (0, 0, 0)),
        ],
        out_specs=pl.BlockSpec((TM, C_W), lambda i: (i, 0)),
        out_shape=jax.ShapeDtypeStruct((n_rows, C_W), BF16),
        scratch_shapes=[pltpu.VMEM((D_MODEL, 2 * C_W), BF16)],
        compiler_params=_cparams(("arbitrary",)),
        name="in_proj_gmlp",
    )(h, w_all, norm_w.reshape(1, C_W), ws.astype(BF16), bs_exp)


def _merge_kernel(ya_ref, yb_ref, atta_ref, attb_ref, sg_ref, gate_ref, xa_ref, xb_ref, g1_ref, sh_ref,
                  sc_ref, nw_ref, wf_ref, wna_ref, wc_ref, wo_ref, o_ref, h_ref, m_scr, *, n_a_tiles):
    i = pl.program_id(0)
    r = _mod_row(i, TM_RES)
    from_a = i < n_a_tiles
    y = jnp.where(from_a, ya_ref[...], yb_ref[...])
    att = jnp.where(from_a, atta_ref[...], attb_ref[...])
    a_f = jnp.dot(y, wf_ref[...], preferred_element_type=F32)
    a_na = jnp.dot(att, wna_ref[...], preferred_element_type=F32)
    a_c = jnp.dot(sg_ref[...], wc_ref[...], preferred_element_type=F32)
    g_f = gate_ref[:, :D_MODEL].astype(F32)
    g_na = gate_ref[:, D_MODEL:2 * D_MODEL].astype(F32)
    g_c = gate_ref[:, 2 * D_MODEL:].astype(F32)
    m_scr[...] = (g_f * a_f + g_na * a_na + g_c * a_c).astype(BF16)
    o = jnp.dot(m_scr[...], wo_ref[...], preferred_element_type=F32)
    x = jnp.where(from_a, xa_ref[...], xb_ref[...])
    xn = x + g1_ref[pl.ds(r, 1), :] * o
    o_ref[...] = xn
    ms = jnp.mean(xn * xn, axis=-1, keepdims=True)
    wmod = nw_ref[...] * (1.0 + sc_ref[pl.ds(r, 1), :])
    h_ref[...] = (xn * lax.rsqrt(ms + EPS) * wmod + sh_ref[pl.ds(r, 1), :]).astype(BF16)


def _merge(ya, yb, atta, attb, sg, gate, xa, xb, mod, norm2_w, wf, wna, wc, wo, n_rows):
    tm = TM_RES
    na = N_LAT // tm
    nbt = max((n_rows - N_LAT) // tm, 1)
    a_spec = lambda w: pl.BlockSpec((tm, w), lambda i: (jnp.minimum(i, na - 1), 0))
    b_spec = lambda w: pl.BlockSpec((tm, w), lambda i: (jnp.clip(i - na, 0, nbt - 1), 0))
    return pl.pallas_call(
        functools.partial(_merge_kernel, n_a_tiles=na),
        grid=(n_rows // tm,),
        in_specs=[
            a_spec(F_W), b_spec(F_W), a_spec(NA_W), b_spec(NA_W),
            pl.BlockSpec((tm, C_W), lambda i: (i, 0)),
            pl.BlockSpec((tm, 3 * D_MODEL), lambda i: (i, 0)),
            a_spec(D_MODEL), b_spec(D_MODEL),
            pl.BlockSpec((MOD_ROWS, D_MODEL), lambda i: (0, 2)),
            pl.BlockSpec((MOD_ROWS, D_MODEL), lambda i: (0, 3)),
            pl.BlockSpec((MOD_ROWS, D_MODEL), lambda i: (0, 4)),
            pl.BlockSpec((1, D_MODEL), lambda i: (0, 0)),
            _resident2d((F_W, D_MODEL)),
            _resident2d((NA_W, D_MODEL)),
            _resident2d((C_W, D_MODEL)),
            _resident2d((D_MODEL, D_MODEL)),
        ],
        out_specs=[pl.BlockSpec((tm, D_MODEL), lambda i: (i, 0)),
                   pl.BlockSpec((tm, D_MODEL), lambda i: (i, 0))],
        out_shape=[jax.ShapeDtypeStruct((n_rows, D_MODEL), F32),
                   jax.ShapeDtypeStruct((n_rows, D_MODEL), BF16)],
        scratch_shapes=[pltpu.VMEM((tm, D_MODEL), BF16)],
        compiler_params=_cparams(("arbitrary",)),
        name="gated_merge",
    )(ya, yb, atta, attb, sg, gate, xa, xb, mod, mod, mod, norm2_w.reshape(1, D_MODEL), wf, wna, wc, wo)


_FF_TILES = D_FF // TN


def _ffn_up_kernel(h_ref, hp_ref, hn_ref, wa_ref, wg_ref, cw_ref, cb_ref, o_ref, wa_scr, wg_scr, a_scr):
    i = pl.program_id(1)
    tp = h_ref.shape[0]
    rc = tp // N_CHUNKS

    @pl.when(i == 0)
    def _():
        wa_scr[...] = wa_ref[...].astype(BF16)
        wg_scr[...] = wg_ref[...].astype(BF16)

    n_chunks = N_CHUNKS
    for c in range(n_chunks):
        parts = [h_ref[c * rc:(c + 1) * rc, :]]
        lo, hi = HALO + c * rc, HALO + (c + 1) * rc
        if c == 0:
            parts.insert(0, hp_ref[...])
            lo -= HALO
        if c == n_chunks - 1:
            parts.append(hn_ref[...])
            hi += HALO
        lhs = parts[0] if len(parts) == 1 else jnp.concatenate(parts, axis=0)
        a_scr[lo:hi, :] = jnp.dot(lhs, wa_scr[...], preferred_element_type=F32)

    w0, w1, w2, cb = cw_ref[0:1], cw_ref[1:2], cw_ref[2:3], cb_ref[...]
    gate_proj = lambda c: jnp.dot(h_ref[c * rc:(c + 1) * rc, :], wg_scr[...], preferred_element_type=F32)
    gv_nxt = gate_proj(0)
    for c in range(n_chunks):
        r0 = HALO + c * rc
        gv = gv_nxt
        if c + 1 < n_chunks:
            gv_nxt = gate_proj(c + 1)
        row = i * tp + c * rc + lax.broadcasted_iota(jnp.int32, (rc, TN), 0)
        seq_mask = jnp.where(row < N_LAT, SEQ - 1, CTX_LEN - 1)
        pos = row & seq_mask
        prev = jnp.where(pos != 0, a_scr[r0 - 1:r0 - 1 + rc, :], 0.0)
        nxt = jnp.where(pos != seq_mask, a_scr[r0 + 1:r0 + 1 + rc, :], 0.0)
        a = cb + w0 * prev + w1 * a_scr[r0:r0 + rc, :] + w2 * nxt
        o_ref[c * rc:(c + 1) * rc, :] = (a * jax.nn.sigmoid(a) * gv).astype(BF16)


def _ffn_up(h2, w_up_all, layer, conv_w, conv_b, n_rows):
    tp = _proj_tile(n_rows)
    hb = tp // HALO
    last = h2.shape[0] // HALO - 1

    def wspec(off):
        return pl.BlockSpec((pl.Squeezed(), D_MODEL, TN), lambda j, i: (layer, 0, off + j))

    return pl.pallas_call(
        _ffn_up_kernel,
        grid=(_FF_TILES, n_rows // tp),
        in_specs=[
            pl.BlockSpec((tp, D_MODEL), lambda j, i: (i, 0)),
            pl.BlockSpec((HALO, D_MODEL), lambda j, i: (jnp.maximum(i * hb - 1, 0), 0)),
            pl.BlockSpec((HALO, D_MODEL), lambda j, i: (jnp.minimum((i + 1) * hb, last), 0)),
            wspec(0),
            wspec(_FF_TILES),
            pl.BlockSpec((CONV_W, TN), lambda j, i: (0, j)),
            pl.BlockSpec((1, TN), lambda j, i: (0, j)),
        ],
        out_specs=pl.BlockSpec((tp, TN), lambda j, i: (i, j)),
        out_shape=jax.ShapeDtypeStruct((n_rows, D_FF), BF16),
        scratch_shapes=[pltpu.VMEM((D_MODEL, TN), BF16), pltpu.VMEM((D_MODEL, TN), BF16),
                        pltpu.VMEM((tp + 2 * HALO, TN), F32)],
        compiler_params=_cparams(("arbitrary", "arbitrary")),
        name="ffn_up_conv",
    )(h2, h2, h2, w_up_all, w_up_all, conv_w, conv_b.reshape(1, D_FF))


def _ffn_down_kernel(a_ref, w_ref, x_ref, g2_ref, sh_ref, sc_ref, nw_ref, *refs, final):
    tm = a_ref.shape[0]
    r = _mod_row(pl.program_id(0), tm)
    xs_ref = refs[1] if final else refs[0]
    act = a_ref[...]
    ss = jnp.zeros((tm, 1), F32)
    n_c = D_MODEL // TN
    down = lambda c: jnp.dot(act, w_ref[:, c * TN:(c + 1) * TN], preferred_element_type=F32)
    o_nxt = down(0)
    for c in range(n_c):
        sl = slice(c * TN, (c + 1) * TN)
        o = o_nxt
        if c + 1 < n_c:
            o_nxt = down(c + 1)
        xn = x_ref[:, sl] + g2_ref[pl.ds(r, 1), sl] * o
        xs_ref[:, sl] = xn
        ss = ss + jnp.sum(xn * xn, axis=-1, keepdims=True)
    inv = lax.rsqrt(ss * (1.0 / D_MODEL) + EPS)
    if final:
        refs[0][...] = xs_ref[...] * inv * nw_ref[...]
    else:
        wmod = nw_ref[...] * (1.0 + sc_ref[pl.ds(r, 1), :])
        refs[1][...] = (xs_ref[...] * inv * wmod + sh_ref[pl.ds(r, 1), :]).astype(BF16)


def _ffn_down(act, w_down, x_all, mod, mod_next, norm_w, n_rows, final):
    tm = TM_DOWN
    assert n_rows % tm == 0 and N_LAT % tm == 0
    row = lambda i: (i, 0)
    if final:
        out_specs = pl.BlockSpec((tm, D_MODEL), row)
        out_shape = jax.ShapeDtypeStruct((n_rows, D_MODEL), F32)
        scratch = [pltpu.VMEM((tm, D_MODEL), F32)]
    else:
        out_specs = [pl.BlockSpec((tm, D_MODEL), row), pl.BlockSpec((tm, D_MODEL), row)]
        out_shape = [jax.ShapeDtypeStruct((n_rows, D_MODEL), F32),
                     jax.ShapeDtypeStruct((n_rows, D_MODEL), BF16)]
        scratch = []
    return pl.pallas_call(
        functools.partial(_ffn_down_kernel, final=final),
        grid=(n_rows // tm,),
        in_specs=[
            pl.BlockSpec((tm, D_FF), row),
            _resident2d((D_FF, D_MODEL)),
            pl.BlockSpec((tm, D_MODEL), row),
            pl.BlockSpec((MOD_ROWS, D_MODEL), lambda i: (0, 5)),
            pl.BlockSpec((MOD_ROWS, D_MODEL), lambda i: (0, 0)),
            pl.BlockSpec((MOD_ROWS, D_MODEL), lambda i: (0, 1)),
            pl.BlockSpec((1, D_MODEL), lambda i: (0, 0)),
        ],
        out_specs=out_specs,
        out_shape=out_shape,
        scratch_shapes=scratch,
        compiler_params=pltpu.CompilerParams(dimension_semantics=("arbitrary",),
                                             vmem_limit_bytes=60 * 1024 * 1024),
        name="ffn_down",
    )(act, w_down, x_all, mod, mod_next, mod_next, norm_w.reshape(1, D_MODEL))


def kernel(x, c, ctx, c_ctx, ada_w, ada_b, norm1_w, norm2_w, w_in, na_rpb, gmlp_norm_w, gmlp_ws,
           gmlp_bs, w_f_out, w_na_out, w_c_out, w_o, ffn_up, ffn_conv_w, ffn_conv_b, ffn_down,
           final_norm_w):
    assert x.shape == (BATCH, SEQ, D_MODEL) and ctx.shape == (BATCH, CTX_LEN, D_MODEL)
    cvec = jnp.concatenate(
        [c, c_ctx[None, :], jnp.zeros((MOD_ROWS - BATCH - 1, D_MODEL), F32)], axis=0)
    mod_all = _modulation(cvec, ada_w, ada_b)
    tables = _dft_tables()
    xa = x.reshape(N_LAT, D_MODEL)
    xb = ctx.reshape(N_CTX, D_MODEL)
    h1 = _prenorm(xa, xb, mod_all[0], norm1_w[0])

    for l in range(DEPTH):
        last = l == DEPTH - 1
        n_rows = N_LAT if last else N_TOK
        mod = mod_all[l]
        fs = _projection_f(h1, w_in, l, n_rows, not last)
        n_slab = 16
        qkv, *merge_w = _projection(h1, w_in, l, F_W, 3 * NA_W, 1024, _ident, "in_proj_qkv", N_TOK,
                                    side=[(w, w.shape[1] // n_slab) for w in (w_f_out, w_na_out, w_c_out, w_o)])
        sg = _projection_gating(h1, w_in, l, gmlp_norm_w[l], gmlp_ws[l], gmlp_bs[l], n_rows)
        gate, w_down = _projection(h1, w_in, l, F_W + 3 * NA_W + 2 * C_W, 3 * D_MODEL, 1024, jax.nn.sigmoid,
                                   "in_proj_gate", n_rows, side=[(ffn_down, CHUNK)])
        atts = _attention(qkv, _bias_table(na_rpb[l]), not last)
        ys = _fourier(fs[0], fs[-1], tables, not last)
        x_mid, h2 = _merge(ys[0], ys[-1], atts[0], atts[-1], sg, gate, xa, xb, mod, norm2_w[l],
                           *merge_w, n_rows)
        act = _ffn_up(h2, ffn_up, l, ffn_conv_w[l], ffn_conv_b[l], n_rows)
        if last:
            out = _ffn_down(act, w_down, x_mid, mod, mod, final_norm_w, n_rows, True)
        else:
            xa, h1 = _ffn_down(act, w_down, x_mid, mod, mod_all[l + 1], norm1_w[l + 1], n_rows, False)
            xb = xa
    return out.reshape(BATCH, SEQ, D_MODEL)
```
